```python
import jax, jax.numpy as jnp
from jax import lax
import numpy as np

D_MODEL = 1024
BATCH = 16
SEQ = 256
DEPTH = 2
DEC_BATCH = 8
DEC_SEQ = 4096
PAST_LEN = 512

GRID_W = 64
GLA_HEADS = 4
GLA_DK = 64
GLA_DV = 128
GLA_GATE_RANK = 16
GLA_TAU = 16.0
GLA_CHUNK = 64
MLA_HEADS = 8
MLA_Q_RANK = 256
MLA_KV_RANK = 128
MLA_NOPE = 64
MLA_ROPE = 32
MLA_DV = 64
ROPE_BASE = 10000.0
Q_BLOCK = 128
N_EXPERTS = 16
N_GROUPS = 4
EXPERTS_PER_GROUP = N_EXPERTS // N_GROUPS
GROUP_SCORE_K = 2
TOP_K = 2
EXPERT_FF = 512
SHARED_FF = 512
N_MOD = 6
EPS = 1e-6

IN_SIZES = (GLA_HEADS * GLA_DK, GLA_HEADS * GLA_DK, GLA_HEADS * GLA_DV, GLA_HEADS * GLA_DV,
            GLA_GATE_RANK, GLA_GATE_RANK, MLA_Q_RANK, MLA_KV_RANK, MLA_ROPE, D_MODEL, D_MODEL)
IN_COLS = sum(IN_SIZES)
IN_SPLITS = tuple(sum(IN_SIZES[:i + 1]) for i in range(len(IN_SIZES) - 1))

kernel_name = 'hybrid_gla_mla_moe_diffusion_step'


def rmsnorm(x, w):
    xf = x.astype(jnp.float32)
    y = xf * lax.rsqrt(jnp.mean(xf * xf, axis=-1, keepdims=True) + EPS)
    return (y * w.astype(jnp.float32)).astype(x.dtype)


def axial_rope_tables(n_tok):
    rows = n_tok // GRID_W
    r = jnp.repeat(jnp.arange(rows, dtype=jnp.float32), GRID_W)
    col = jnp.tile(jnp.arange(GRID_W, dtype=jnp.float32), rows)
    n_freq = MLA_ROPE // 4
    inv = ROPE_BASE ** (-jnp.arange(n_freq, dtype=jnp.float32) / n_freq)
    ang = jnp.stack([r[:, None] * inv, col[:, None] * inv], axis=1)
    return jnp.cos(ang), jnp.sin(ang)


def apply_rope(x, cos, sin):
    shp = x.shape
    nf = MLA_ROPE // 4
    xf = x.astype(jnp.float32).reshape(shp[:-1] + (2, 2, nf))
    x1, x2 = xf[..., 0, :], xf[..., 1, :]
    bshape = (shp[1],) + (1,) * (x.ndim - 3) + (2, nf)
    cos = cos.reshape(bshape)
    sin = sin.reshape(bshape)
    out = jnp.stack([x1 * cos - x2 * sin, x2 * cos + x1 * sin], axis=-2)
    return out.reshape(shp).astype(x.dtype)


def gla_chunked(q, k, v, log_a, s0):
    b_, t_, h_, dk = q.shape
    n = t_ // GLA_CHUNK
    f32 = jnp.float32

    def chunks(x):
        return x.astype(f32).reshape(b_, n, GLA_CHUNK, h_, x.shape[-1]).transpose(1, 0, 3, 2, 4)

    qc = chunks(q) * (dk ** -0.5)
    kc, vc, ac = chunks(k), chunks(v), chunks(log_a)
    cum = jnp.cumsum(ac, axis=-2)
    cum_last = cum[..., -1:, :]
    q_dec = qc * jnp.exp(cum)
    k_in = kc * jnp.exp(-cum)
    k_out = kc * jnp.exp(cum_last - cum)
    lower = jnp.tril(jnp.ones((GLA_CHUNK, GLA_CHUNK), bool))
    att = jnp.where(lower, jnp.einsum('nbhik,nbhjk->nbhij', q_dec, k_in), 0.0)
    o_intra = jnp.einsum('nbhij,nbhjv->nbhiv', att, vc)

    def step(state, inp):
        qd, ko, vv, cl = inp
        o = jnp.einsum('bhik,bhkv->bhiv', qd, state)
        state = jnp.exp(cl[..., 0, :])[..., None] * state + jnp.einsum('bhjk,bhjv->bhkv', ko, vv)
        return state, o

    s_fin, o_inter = lax.scan(step, s0.astype(f32), (q_dec, k_out, vc, cum_last))
    o = (o_intra + o_inter).transpose(1, 0, 3, 2, 4).reshape(b_, t_, h_, vc.shape[-1])
    return o.astype(q.dtype), s_fin.astype(q.dtype)


def gla_bidir(q, k, v, g, la_f, la_b, s0_f, s0_b, norm_w):
    o_f, s_f = gla_chunked(q, k, v, la_f, s0_f)
    rev = lambda t: jnp.flip(t, axis=1)
    o_b, s_b = gla_chunked(rev(q), rev(k), rev(v), rev(la_b), s0_b)
    o = rmsnorm(o_f + rev(o_b), norm_w) * jax.nn.silu(g)
    return o.reshape(o.shape[0], o.shape[1], -1), s_f, s_b


def mla_attend(qn, qr, kn, kr, v):
    b_, t_, h_, _ = qn.shape
    nb = t_ // Q_BLOCK
    scale = (MLA_NOPE + MLA_ROPE) ** -0.5

    def blocks(x):
        return x.reshape((b_, nb, Q_BLOCK) + x.shape[2:]).swapaxes(0, 1)

    def one_block(qs):
        qbn, qbr = qs
        s = jnp.einsum('bqhd,bkhd->bhqk', qbn, kn) + jnp.einsum('bqhd,bkd->bhqk', qbr, kr)
        p = jax.nn.softmax(s.astype(jnp.float32) * scale, axis=-1).astype(v.dtype)
        return jnp.einsum('bhqk,bkhd->bqhd', p, v)

    o = lax.map(one_block, (blocks(qn), blocks(qr)))
    return o.swapaxes(0, 1).reshape(b_, t_, h_ * MLA_DV)


def mixer(h, p, ctx):
    b_, t_, _ = h.shape
    (gq, gk, gv, gg, ga_f, ga_b, cq, ckv, kr, za, zb) = jnp.split(h @ p['w_in'], IN_SPLITS, axis=-1)
    la_f = jax.nn.log_sigmoid((ga_f @ p['w_alpha'][0] + p['b_alpha'][0]).astype(jnp.float32)) / GLA_TAU
    la_b = jax.nn.log_sigmoid((ga_b @ p['w_alpha'][1] + p['b_alpha'][1]).astype(jnp.float32)) / GLA_TAU
    heads = lambda x, d: x.reshape(b_, t_, -1, d)
    if ctx is None:
        s0_f = jnp.zeros((b_, GLA_HEADS, GLA_DK, GLA_DV), h.dtype)
        s0_b = s0_f
    else:
        s0_f, s0_b = ctx[2], ctx[3]
    o_a, s_f, s_b = gla_bidir(heads(gq, GLA_DK), heads(gk, GLA_DK), heads(gv, GLA_DV), heads(gg, GLA_DV),
                              heads(la_f, GLA_DK), heads(la_b, GLA_DK), s0_f, s0_b, p['gla_norm'])
    q = (rmsnorm(cq, p['q_norm']) @ p['w_uq']).reshape(b_, t_, MLA_HEADS, MLA_NOPE + MLA_ROPE)
    qn, qr = q[..., :MLA_NOPE], q[..., MLA_NOPE:]
    ckv = rmsnorm(ckv, p['kv_norm'])
    if ctx is None:
        ckv_all, kr_all = ckv, kr
    else:
        cos, sin = axial_rope_tables(t_)
        qr = apply_rope(qr, cos, sin)
        ckv_all = jnp.concatenate([ctx[0], ckv], axis=1)
        kr_all = jnp.concatenate([ctx[1], apply_rope(kr, cos, sin)], axis=1)
    s_ = ckv_all.shape[1]
    kn = (ckv_all @ p['w_uk']).reshape(b_, s_, MLA_HEADS, MLA_NOPE)
    vb = (ckv_all @ p['w_uv']).reshape(b_, s_, MLA_HEADS, MLA_DV)
    o_b = mla_attend(qn, qr, kn, kr_all, vb)
    y = jax.nn.sigmoid(za) * (o_a @ p['w_pa']) + jax.nn.sigmoid(zb) * (o_b @ p['w_pb'])
    out = y @ p['w_o']
    if ctx is None:
        return out, (ckv, kr, s_f, s_b)
    return out, None


def swiglu(x, wg, wu, wd):
    return (jax.nn.silu(x @ wg) * (x @ wu)) @ wd


def moe(h, p):
    b_, t_, d_ = h.shape
    x = h.reshape(-1, d_)
    n = x.shape[0]
    aff = jax.nn.sigmoid((x @ p['w_router']).astype(jnp.float32))
    biased = aff + p['b_router'].astype(jnp.float32)
    grp_score = lax.top_k(biased.reshape(n, N_GROUPS, EXPERTS_PER_GROUP), GROUP_SCORE_K)[0].sum(-1)
    sel_grp = jnp.argmax(grp_score, axis=-1)
    in_grp = (jnp.arange(N_EXPERTS) // EXPERTS_PER_GROUP)[None, :] == sel_grp[:, None]
    _, idx = lax.top_k(jnp.where(in_grp, biased, -jnp.inf), TOP_K)
    wts = jnp.take_along_axis(aff, idx, axis=-1)
    wts = wts / jnp.sum(wts, axis=-1, keepdims=True)
    combine = jnp.einsum('nke,nk->ne', jax.nn.one_hot(idx, N_EXPERTS, dtype=jnp.float32), wts).astype(x.dtype)
    out = swiglu(x, p['w_sh_gate'], p['w_sh_up'], p['w_sh_down'])
    for e in range(N_EXPERTS):
        out = out + combine[:, e:e + 1] * swiglu(x, p['w_exp_gate'][e], p['w_exp_up'][e], p['w_exp_down'][e])
    return out.reshape(b_, t_, d_)


def layer(x, cond, p, ctx):
    m = jax.nn.silu(cond) @ p['w_mod'] + p['b_mod']
    sh1, sc1, g1, sh2, sc2, g2 = jnp.split(m[:, None, :], N_MOD, axis=-1)
    h = rmsnorm(x, p['norm_mix']) * (1.0 + sc1) + sh1
    out, ctx_t = mixer(h, p, ctx)
    x = x + g1 * out
    h = rmsnorm(x, p['norm_ffn']) * (1.0 + sc2) + sh2
    x = x + g2 * moe(h, p)
    return x, ctx_t


def setup_inputs(seed: int = 0) -> dict:
    key = jax.random.key(seed)
    ks = iter(jax.random.split(key, 40))
    nrm = lambda shape, scale: scale * jax.random.normal(next(ks), shape, jnp.float32)
    gain = lambda shape: 1.0 + nrm(shape, 0.02)
    hk, hv, hb = GLA_HEADS * GLA_DK, GLA_HEADS * GLA_DV, MLA_HEADS * MLA_DV
    return {
        'x_prompt': nrm((BATCH, SEQ, D_MODEL), 1.0),
        'x_sample': nrm((DEC_BATCH, DEC_SEQ, D_MODEL), 1.0),
        'cache_ckv': nrm((DEC_BATCH, DEPTH, PAST_LEN, MLA_KV_RANK), 1.0),
        'cache_krope': nrm((DEC_BATCH, DEPTH, PAST_LEN, MLA_ROPE), 1.0),
        'state_gla_fwd': nrm((DEC_BATCH, DEPTH, GLA_HEADS, GLA_DK, GLA_DV), 0.5),
        'state_gla_bwd': nrm((DEC_BATCH, DEPTH, GLA_HEADS, GLA_DK, GLA_DV), 0.5),
        'c': nrm((DEC_BATCH, D_MODEL), 1.0),
        'c_ctx': nrm((D_MODEL,), 1.0),
        'w_mod': nrm((DEPTH, D_MODEL, N_MOD * D_MODEL), 0.5 * D_MODEL ** -0.5),
        'b_mod': nrm((DEPTH, N_MOD * D_MODEL), 0.02),
        'norm_mix': gain((DEPTH, D_MODEL)),
        'norm_ffn': gain((DEPTH, D_MODEL)),
        'w_in': nrm((DEPTH, D_MODEL, IN_COLS), D_MODEL ** -0.5),
        'w_alpha': nrm((DEPTH, 2, GLA_GATE_RANK, hk), GLA_GATE_RANK ** -0.5),
        'b_alpha': nrm((DEPTH, 2, hk), 0.1),
        'gla_norm': gain((DEPTH, GLA_DV)),
        'q_norm': gain((DEPTH, MLA_Q_RANK)),
        'kv_norm': gain((DEPTH, MLA_KV_RANK)),
        'w_uq': nrm((DEPTH, MLA_Q_RANK, MLA_HEADS * (MLA_NOPE + MLA_ROPE)), MLA_Q_RANK ** -0.5),
        'w_uk': nrm((DEPTH, MLA_KV_RANK, MLA_HEADS * MLA_NOPE), MLA_KV_RANK ** -0.5),
        'w_uv': nrm((DEPTH, MLA_KV_RANK, hb), MLA_KV_RANK ** -0.5),
        'w_pa': nrm((DEPTH, hv, D_MODEL), hv ** -0.5),
        'w_pb': nrm((DEPTH, hb, D_MODEL), hb ** -0.5),
        'w_o': nrm((DEPTH, D_MODEL, D_MODEL), D_MODEL ** -0.5),
        'w_router': nrm((D_MODEL, N_EXPERTS), D_MODEL ** -0.5),
        'b_router': nrm((N_EXPERTS,), 0.01),
        'w_exp_gate': nrm((DEPTH, N_EXPERTS, D_MODEL, EXPERT_FF), D_MODEL ** -0.5),
        'w_exp_up': nrm((DEPTH, N_EXPERTS, D_MODEL, EXPERT_FF), D_MODEL ** -0.5),
        'w_exp_down': nrm((DEPTH, N_EXPERTS, EXPERT_FF, D_MODEL), EXPERT_FF ** -0.5),
        'w_sh_gate': nrm((DEPTH, D_MODEL, SHARED_FF), D_MODEL ** -0.5),
        'w_sh_up': nrm((DEPTH, D_MODEL, SHARED_FF), D_MODEL ** -0.5),
        'w_sh_down': nrm((DEPTH, SHARED_FF, D_MODEL), SHARED_FF ** -0.5),
        'final_norm': gain((D_MODEL,)),
    }


def reference(x_prompt, x_sample, cache_ckv, cache_krope, state_gla_fwd, state_gla_bwd, c, c_ctx,
              w_mod, b_mod, norm_mix, norm_ffn, w_in, w_alpha, b_alpha, gla_norm, q_norm, kv_norm,
              w_uq, w_uk, w_uv, w_pa, w_pb, w_o, w_router, b_router, w_exp_gate, w_exp_up, w_exp_down,
              w_sh_gate, w_sh_up, w_sh_down, final_norm):
    def layer_params(l):
        return {'w_mod': w_mod[l], 'b_mod': b_mod[l], 'norm_mix': norm_mix[l], 'norm_ffn': norm_ffn[l],
                'w_in': w_in[l], 'w_alpha': w_alpha[l], 'b_alpha': b_alpha[l], 'gla_norm': gla_norm[l],
                'q_norm': q_norm[l], 'kv_norm': kv_norm[l], 'w_uq': w_uq[l], 'w_uk': w_uk[l], 'w_uv': w_uv[l],
                'w_pa': w_pa[l], 'w_pb': w_pb[l], 'w_o': w_o[l],
                'w_router': w_router, 'b_router': b_router,
                'w_exp_gate': w_exp_gate[l], 'w_exp_up': w_exp_up[l], 'w_exp_down': w_exp_down[l],
                'w_sh_gate': w_sh_gate[l], 'w_sh_up': w_sh_up[l], 'w_sh_down': w_sh_down[l]}

    xp = x_prompt
    ckvs, krs, sfs, sbs = [], [], [], []
    for l in range(DEPTH):
        xp, (ckv_l, kr_l, sf_l, sb_l) = layer(xp, c_ctx[None, :], layer_params(l), None)
        ckvs.append(ckv_l)
        krs.append(kr_l)
        sfs.append(sf_l)
        sbs.append(sb_l)
    y_prompt = rmsnorm(xp, final_norm)
    new_ckv = jnp.stack(ckvs, axis=1)
    new_krope = jnp.stack(krs, axis=1)
    new_state_fwd = jnp.stack(sfs, axis=1)
    new_state_bwd = jnp.stack(sbs, axis=1)

    xs = x_sample
    for l in range(DEPTH):
        xs, _ = layer(xs, c, layer_params(l),
                      (cache_ckv[:, l], cache_krope[:, l], state_gla_fwd[:, l], state_gla_bwd[:, l]))
    y_sample = rmsnorm(xs, final_norm)
    return (y_prompt, y_sample, new_ckv, new_krope, new_state_fwd, new_state_bwd)
```

```python
import functools

import jax
import jax.numpy as jnp
from jax import lax
from jax.experimental import pallas as pl
from jax.experimental.pallas import tpu as pltpu

F32 = jnp.float32
BF16 = jnp.bfloat16

D_MODEL = 1024
BATCH = 16
SEQ = 256
DEPTH = 2
DEC_BATCH = 8
DEC_SEQ = 4096
PAST_LEN = 512
GRID_W = 64
GLA_HEADS = 4
GLA_DK = 64
GLA_DV = 128
GLA_GATE_RANK = 16
GLA_TAU = 16.0
GLA_CHUNK = 64
MLA_HEADS = 8
MLA_Q_RANK = 256
MLA_KV_RANK = 128
MLA_NOPE = 64
MLA_ROPE = 32
MLA_DV = 64
ROPE_BASE = 10000.0
N_EXPERTS = 16
N_GROUPS = 4
EXPERTS_PER_GROUP = 4
EXPERT_FF = 512
SHARED_FF = 512
N_MOD = 6
EPS = 1e-6

N_CTX = BATCH * SEQ
N_LAT = DEC_BATCH * DEC_SEQ
N_TOK = N_CTX + N_LAT
N_SEQ = BATCH + DEC_BATCH
TM = 256
N_TILES = N_TOK // TM
CTX_TILES = N_CTX // TM
LAT_TILES_PER_SEQ = DEC_SEQ // TM
HK = GLA_HEADS * GLA_DK
HV = GLA_HEADS * GLA_DV
QCAT = MLA_KV_RANK + MLA_ROPE
ATT_SCALE = (MLA_NOPE + MLA_ROPE) ** -0.5
KV_BLOCK_LAT = 512
TM_MOE = 1024
VMEM_LIMIT = 56 * 1024 * 1024

C_GQ, C_GK, C_GV, C_GG, C_CQ, C_CKV, C_SMALL, C_ZA, C_ZB, C_END = (
    0, 256, 512, 1024, 1536, 1792, 1920, 2048, 3072, 4096)


def _sigmoid(x):
    return 1.0 / (1.0 + jnp.exp(-x))


def _rms(x, w):
    return x * lax.rsqrt(jnp.mean(x * x, axis=-1, keepdims=True) + EPS) * w


def _dot(a, b):
    return jnp.dot(a, b, preferred_element_type=F32)


def _dot_nt(a, b):
    return lax.dot_general(a, b, (((1,), (1,)), ((), ())), preferred_element_type=F32)


def _dot_tn(a, b):
    return lax.dot_general(a, b, (((0,), (0,)), ((), ())), preferred_element_type=F32)


def _tile_seq(i):
    return jnp.where(i < CTX_TILES, 0, 1 + (i - CTX_TILES) // LAT_TILES_PER_SEQ)


def _tile_pos(i):
    return jnp.where(i < CTX_TILES, LAT_TILES_PER_SEQ, (i - CTX_TILES) % LAT_TILES_PER_SEQ)


def _mod_kernel(c_ref, w_ref, b_ref, o_ref):
    c = c_ref[...]
    sc = (c * _sigmoid(c)).astype(BF16)
    o_ref[...] = _dot(sc, w_ref[...]) + b_ref[...]


def _modulation(cond, w_mod, b_mod):
    nb = 1024
    return pl.pallas_call(
        _mod_kernel,
        out_shape=jax.ShapeDtypeStruct((DEPTH, 16, N_MOD * D_MODEL), F32),
        grid=(DEPTH, N_MOD * D_MODEL // nb),
        in_specs=[
            pl.BlockSpec((16, D_MODEL), lambda l, j: (0, 0)),
            pl.BlockSpec((None, D_MODEL, nb), lambda l, j: (l, 0, j)),
            pl.BlockSpec((None, 1, nb), lambda l, j: (l, 0, j)),
        ],
        out_specs=pl.BlockSpec((None, 16, nb), lambda l, j: (l, 0, j)),
        name="modulation",
    )(cond, w_mod, b_mod)


def _inproj_kernel(x_ref, mod_ref, nw_ref, w1_ref, wa_ref, ba_ref, qnw_ref, kvnw_ref,
                   wuqn_ref, wuqr_ref, wuqrr_ref, wukt_ref, cq_ref, sq_ref, ck_ref, sk_ref,
                   gq_ref, gk_ref, gv_ref, gg_ref, la_ref, q_ref, ckv_ref, kr_ref, za_ref, zb_ref):
    x = x_ref[...]
    mod = mod_ref[0]
    h = (_rms(x, nw_ref[...]) * (1.0 + mod[1:2, :]) + mod[0:1, :]).astype(BF16)

    def proj(lo, hi):
        return _dot(h, w1_ref[:, lo:hi])

    gq_ref[...] = (proj(C_GQ, C_GK) * (GLA_DK ** -0.5)).astype(BF16)
    gk_ref[...] = proj(C_GK, C_GV).astype(BF16)
    gv_ref[...] = proj(C_GV, C_GG).astype(BF16)
    gg_ref[...] = proj(C_GG, C_CQ).astype(BF16)
    za_ref[...] = _sigmoid(proj(C_ZA, C_ZB)).astype(BF16)
    zb_ref[...] = _sigmoid(proj(C_ZB, C_END)).astype(BF16)

    small = proj(C_SMALL, C_ZA)
    lin = _dot(small.astype(BF16), wa_ref[...]) + ba_ref[...]
    la_ref[...] = (jnp.minimum(lin, 0.0) - jnp.log(1.0 + jnp.exp(-jnp.abs(lin)))) * (1.0 / GLA_TAU)
    kr_ref[...] = small[:, 32:64] * ck_ref[...] + small[:, 64:96] * sk_ref[...]

    ckv_ref[...] = _rms(proj(C_CKV, C_SMALL), kvnw_ref[...])

    cqn = _rms(proj(C_CQ, C_CKV), qnw_ref[...]).astype(BF16)
    qn = _dot(cqn, wuqn_ref[...]).astype(BF16)
    qr = (_dot(cqn, wuqr_ref[...]) * cq_ref[...]
          + _dot(cqn, wuqrr_ref[...]) * sq_ref[...])
    for hd in range(MLA_HEADS):
        qa = _dot(qn[:, hd * MLA_NOPE:(hd + 1) * MLA_NOPE], wukt_ref[hd]) * ATT_SCALE
        q_ref[hd, :, 0:MLA_KV_RANK] = qa.astype(BF16)
        q_ref[hd, :, MLA_KV_RANK:QCAT] = qr[:, hd * MLA_ROPE:(hd + 1) * MLA_ROPE].astype(BF16)


def _inproj(x, mod, nw, w1, wa, ba, qnw, kvnw, wuqn, wuqr, wuqrr, wukt, cq, sq, ck, sk):
    full = lambda a: pl.BlockSpec(a.shape, lambda i: (0,) * a.ndim)
    tok = lambda w: pl.BlockSpec((TM, w), lambda i: (i, 0))
    pos = lambda w: pl.BlockSpec((TM, w), lambda i: (_tile_pos(i), 0))
    out_shape = [
        jax.ShapeDtypeStruct((N_TOK, HK), BF16),
        jax.ShapeDtypeStruct((N_TOK, HK), BF16),
        jax.ShapeDtypeStruct((N_TOK, HV), BF16),
        jax.ShapeDtypeStruct((N_TOK, HV), BF16),
        jax.ShapeDtypeStruct((N_TOK, 2 * HK), F32),
        jax.ShapeDtypeStruct((MLA_HEADS, N_TOK, QCAT), BF16),
        jax.ShapeDtypeStruct((N_TOK, MLA_KV_RANK), F32),
        jax.ShapeDtypeStruct((N_TOK, MLA_ROPE), F32),
        jax.ShapeDtypeStruct((N_TOK, D_MODEL), BF16),
        jax.ShapeDtypeStruct((N_TOK, D_MODEL), BF16),
    ]
    out_specs = [tok(HK), tok(HK), tok(HV), tok(HV), tok(2 * HK),
                 pl.BlockSpec((MLA_HEADS, TM, QCAT), lambda i: (0, i, 0)),
                 tok(MLA_KV_RANK), tok(MLA_ROPE), tok(D_MODEL), tok(D_MODEL)]
    in_specs = [tok(D_MODEL),
                pl.BlockSpec((1, N_MOD, D_MODEL), lambda i: (_tile_seq(i), 0, 0)),
                full(nw), full(w1), full(wa), full(ba), full(qnw), full(kvnw),
                full(wuqn), full(wuqr), full(wuqrr), full(wukt),
                pos(HK), pos(HK), pos(MLA_ROPE), pos(MLA_ROPE)]
    return pl.pallas_call(
        _inproj_kernel, out_shape=out_shape, grid=(N_TILES,),
        in_specs=in_specs, out_specs=out_specs,
        compiler_params=pltpu.CompilerParams(dimension_semantics=("arbitrary",),
                                             vmem_limit_bytes=VMEM_LIMIT),
        name="inproj",
    )(x, mod, nw, w1, wa, ba, qnw, kvnw, wuqn, wuqr, wuqrr, wukt, cq, sq, ck, sk)


def _gla_kernel(blk_ref, first_ref, seq_ref, q_ref, k_ref, v_ref, g_ref, la_ref, s0_ref, tri_ref,
                *rest, reverse, final):
    if final:
        oprev_ref, nw_ref, o_ref, sfin_ref, st_ref = rest
    else:
        o_ref, sfin_ref, st_ref = rest
    step = pl.program_id(0)

    @pl.when(first_ref[step] == 1)
    def _():
        st_ref[...] = s0_ref[0]

    n_chunks = TM // GLA_CHUNK
    row = lax.broadcasted_iota(jnp.int32, (GLA_CHUNK, GLA_CHUNK), 0)
    col = lax.broadcasted_iota(jnp.int32, (GLA_CHUNK, GLA_CHUNK), 1)
    keep = (col >= row) if reverse else (col <= row)
    tri = tri_ref[...]

    def chunk(ci, carry):
        c = (n_chunks - 1 - ci) if reverse else ci
        r0 = pl.multiple_of(c * GLA_CHUNK, GLA_CHUNK)
        rows = pl.ds(r0, GLA_CHUNK)
        la = la_ref[rows, :]
        la_hi = la.astype(BF16)
        la_lo = (la - la_hi.astype(F32)).astype(BF16)
        cum = _dot(tri, la_hi) + _dot(tri, la_lo)
        edge = cum[0:1, :] if reverse else cum[GLA_CHUNK - 1:GLA_CHUNK, :]
        q = q_ref[rows, :].astype(F32)
        k = k_ref[rows, :].astype(F32)
        qd = (q * jnp.exp(cum)).astype(BF16)
        kin = (k * jnp.exp(-cum)).astype(BF16)
        kout = (k * jnp.exp(edge - cum)).astype(BF16)
        decay = jnp.exp(edge)
        st = st_ref[...]
        st_b = st.astype(BF16)
        for hd in range(GLA_HEADS):
            ks = slice(hd * GLA_DK, (hd + 1) * GLA_DK)
            vs = slice(hd * GLA_DV, (hd + 1) * GLA_DV)
            att = jnp.where(keep, _dot_nt(qd[:, ks], kin[:, ks]), 0.0)
            vh = v_ref[rows, vs]
            o = _dot(att.astype(BF16), vh) + _dot_nt(qd[:, ks], st_b[:, ks])
            st_ref[:, ks] = st[:, ks] * decay[:, ks] + _dot_tn(vh, kout[:, ks])
            if final:
                o = o + oprev_ref[rows, vs]
                o = _rms(o, nw_ref[...])
                gt = g_ref[rows, vs].astype(F32)
                o_ref[rows, vs] = (o * gt * _sigmoid(gt)).astype(o_ref.dtype)
            else:
                o_ref[rows, vs] = o
        return carry

    lax.fori_loop(0, n_chunks, chunk, 0)
    sfin_ref[0] = st_ref[...]


def _gla_direction(tables, q, k, v, g, la, s0t, tri, oprev, nw, *, reverse):
    final = oprev is not None
    blk, first, seq = tables
    tok = lambda w: pl.BlockSpec((TM, w), lambda s, b, f, q_: (b[s], 0))
    in_specs = [tok(HK), tok(HK), tok(HV), tok(HV),
                pl.BlockSpec((TM, HK), lambda s, b, f, q_: (b[s], 1 if reverse else 0)),
                pl.BlockSpec((1, GLA_DV, HK), lambda s, b, f, q_: (q_[s], 0, 0)),
                pl.BlockSpec((GLA_CHUNK, GLA_CHUNK), lambda s, b, f, q_: (0, 0))]
    args = [q, k, v, g, la, s0t, tri]
    if final:
        in_specs += [tok(HV), pl.BlockSpec((1, GLA_DV), lambda s, b, f, q_: (0, 0))]
        args += [oprev, nw]
    out_shape = [jax.ShapeDtypeStruct((N_TOK, HV), BF16 if final else F32),
                 jax.ShapeDtypeStruct((N_SEQ, GLA_DV, HK), F32)]
    out_specs = [tok(HV), pl.BlockSpec((1, GLA_DV, HK), lambda s, b, f, q_: (q_[s], 0, 0))]
    return pl.pallas_call(
        functools.partial(_gla_kernel, reverse=reverse, final=final),
        out_shape=out_shape,
        grid_spec=pltpu.PrefetchScalarGridSpec(
            num_scalar_prefetch=3, grid=(N_TILES,), in_specs=in_specs, out_specs=out_specs,
            scratch_shapes=[pltpu.VMEM((GLA_DV, HK), F32)]),
        compiler_params=pltpu.CompilerParams(dimension_semantics=("arbitrary",)),
        name="gla_bwd" if reverse else "gla_fwd",
    )(blk, first, seq, *args)


def _attn_kernel(q_ref, k_ref, vt_ref, wuvt_ref, o_ref, *, kv_block, n_kv):
    outs = []
    for hd in range(MLA_HEADS):
        qh = q_ref[hd]

        def kv_step(j, carry):
            m, l, acc = carry
            r0 = pl.multiple_of(j * kv_block, kv_block)
            s = _dot_nt(k_ref[0, pl.ds(r0, kv_block), :], qh)
            m_new = jnp.maximum(m, jnp.max(s, axis=0, keepdims=True))
            alpha = jnp.exp(m - m_new)
            p = jnp.exp(s - m_new)
            l = alpha * l + jnp.sum(p, axis=0, keepdims=True)
            acc = alpha * acc + _dot(vt_ref[0, j], p.astype(BF16))
            return m_new, l, acc

        init = (jnp.full((1, TM), -1e30, F32), jnp.zeros((1, TM), F32),
                jnp.zeros((MLA_KV_RANK, TM), F32))
        _, l, acc = lax.fori_loop(0, n_kv, kv_step, init)
        lat = (acc / l).astype(BF16)
        outs.append(_dot(wuvt_ref[hd], lat))
    o_ref[...] = jnp.concatenate(outs, axis=0).T.astype(o_ref.dtype)


def _attention(q, kcat, vt, wuvt, *, tile0, tiles_per_seq, prev=None):
    n_seq, s_len, _ = kcat.shape
    n_kv, kv_block = vt.shape[1], vt.shape[3]
    in_specs = [
        pl.BlockSpec((MLA_HEADS, TM, QCAT), lambda b, i: (0, tile0 + b * tiles_per_seq + i, 0)),
        pl.BlockSpec((1, s_len, QCAT), lambda b, i: (b, 0, 0)),
        pl.BlockSpec((1, n_kv, MLA_KV_RANK, kv_block), lambda b, i: (b, 0, 0, 0)),
        pl.BlockSpec(wuvt.shape, lambda b, i: (0, 0, 0)),
    ]
    args = [q, kcat, vt, wuvt]
    kernel = functools.partial(_attn_kernel, kv_block=kv_block, n_kv=n_kv)
    aliases = {}
    if prev is not None:
        in_specs.append(pl.BlockSpec(memory_space=pl.ANY))
        args.append(prev)
        aliases = {4: 0}
        kernel = lambda q_, k_, v_, w_, p_, o_, _f=kernel: _f(q_, k_, v_, w_, o_)
    return pl.pallas_call(
        kernel,
        out_shape=jax.ShapeDtypeStruct((N_TOK, MLA_HEADS * MLA_DV), BF16),
        grid=(n_seq, tiles_per_seq),
        in_specs=in_specs,
        out_specs=pl.BlockSpec((TM, MLA_HEADS * MLA_DV),
                               lambda b, i: (tile0 + b * tiles_per_seq + i, 0)),
        input_output_aliases=aliases,
        compiler_params=pltpu.CompilerParams(dimension_semantics=("arbitrary", "arbitrary"),
                                             vmem_limit_bytes=VMEM_LIMIT),
        name="mla_ctx" if prev is None else "mla_lat",
    )(*args)


def _merge_kernel(x_ref, mod_ref, oa_ref, ob_ref, za_ref, zb_ref, wpa_ref, wpb_ref, wo_ref, nw_ref,
                  wr1_ref, wr2_ref, x1_ref, h2_ref, lg_ref):
    mod = mod_ref[0]
    y = (za_ref[...].astype(F32) * _dot(oa_ref[...], wpa_ref[...])
         + zb_ref[...].astype(F32) * _dot(ob_ref[...], wpb_ref[...]))
    out = _dot(y.astype(BF16), wo_ref[...])
    x1 = x_ref[...] + mod[2:3, :] * out
    x1_ref[...] = x1
    h2 = _rms(x1, nw_ref[...]) * (1.0 + mod[4:5, :]) + mod[3:4, :]
    h2_hi = h2.astype(BF16)
    h2_lo = (h2 - h2_hi.astype(F32)).astype(BF16)
    h2_ref[...] = h2_hi
    d1 = _dot(h2_hi, wr1_ref[...])
    d2 = _dot(h2_lo, wr2_ref[...])
    lg_ref[...] = (d1[:, 0:N_EXPERTS] + d1[:, N_EXPERTS:2 * N_EXPERTS]) + d2[:, 0:N_EXPERTS]


def _merge(x, mod, oa, ob, za, zb, wpa, wpb, wo, nw, wr1, wr2):
    full = lambda a: pl.BlockSpec(a.shape, lambda i: (0,) * a.ndim)
    tok = lambda w: pl.BlockSpec((TM, w), lambda i: (i, 0))
    return pl.pallas_call(
        _merge_kernel,
        out_shape=[jax.ShapeDtypeStruct((N_TOK, D_MODEL), F32),
                   jax.ShapeDtypeStruct((N_TOK, D_MODEL), BF16),
                   jax.ShapeDtypeStruct((N_TOK, N_EXPERTS), F32)],
        grid=(N_TILES,),
        in_specs=[tok(D_MODEL), pl.BlockSpec((1, N_MOD, D_MODEL), lambda i: (_tile_seq(i), 0, 0)),
                  tok(HV), tok(MLA_HEADS * MLA_DV), tok(D_MODEL), tok(D_MODEL),
                  full(wpa), full(wpb), full(wo), full(nw), full(wr1), full(wr2)],
        out_specs=[tok(D_MODEL), tok(D_MODEL), tok(N_EXPERTS)],
        compiler_params=pltpu.CompilerParams(dimension_semantics=("arbitrary",),
                                             vmem_limit_bytes=VMEM_LIMIT),
        name="merge",
    )(x, mod, oa, ob, za, zb, wpa, wpb, wo, nw, wr1, wr2)


def _route_kernel(lg_ref, b_ref, comb_ref):
    aff = _sigmoid(lg_ref[...])
    biased = aff + b_ref[...]
    row = lambda a, e: a[e:e + 1, :]
    best = None
    sel = None
    for g in range(N_GROUPS):
        b = [row(biased, g * EXPERTS_PER_GROUP + i) for i in range(EXPERTS_PER_GROUP)]
        score = None
        for i in range(EXPERTS_PER_GROUP):
            for j in range(i + 1, EXPERTS_PER_GROUP):
                pair = b[i] + b[j]
                score = pair if score is None else jnp.maximum(score, pair)
        if g == 0:
            best, sel = score, jnp.zeros_like(score, dtype=jnp.int32)
        else:
            better = score > best
            best = jnp.where(better, score, best)
            sel = jnp.where(better, g, sel)
    cb, ca = [], []
    for i in range(EXPERTS_PER_GROUP):
        vb = row(biased, i)
        va = row(aff, i)
        for g in range(1, N_GROUPS):
            vb = jnp.where(sel == g, row(biased, g * EXPERTS_PER_GROUP + i), vb)
            va = jnp.where(sel == g, row(aff, g * EXPERTS_PER_GROUP + i), va)
        cb.append(vb)
        ca.append(va)
    picked = []
    for i in range(EXPERTS_PER_GROUP):
        rank = jnp.zeros_like(sel)
        for j in range(EXPERTS_PER_GROUP):
            if j == i:
                continue
            ahead = (cb[j] >= cb[i]) if j < i else (cb[j] > cb[i])
            rank = rank + ahead.astype(jnp.int32)
        picked.append(rank < 2)
    denom = None
    for i in range(EXPERTS_PER_GROUP):
        term = jnp.where(picked[i], ca[i], 0.0)
        denom = term if denom is None else denom + term
    for g in range(N_GROUPS):
        for i in range(EXPERTS_PER_GROUP):
            e = g * EXPERTS_PER_GROUP + i
            comb_ref[e:e + 1, :] = jnp.where((sel == g) & picked[i], ca[i] / denom, 0.0)


def _route(lg_t, b_router):
    w = 2048
    return pl.pallas_call(
        _route_kernel,
        out_shape=jax.ShapeDtypeStruct((N_EXPERTS, N_TOK), F32),
        grid=(N_TOK // w,),
        in_specs=[pl.BlockSpec((N_EXPERTS, w), lambda i: (0, i)),
                  pl.BlockSpec((N_EXPERTS, 1), lambda i: (0, 0))],
        out_specs=pl.BlockSpec((N_EXPERTS, w), lambda i: (0, i)),
        name="route",
    )(lg_t, b_router)


def _moe_kernel(x1_ref, h2_ref, comb_ref, mod_ref, wg_ref, wu_ref, wd_ref, fw_ref, o_ref, acc_ref, *, final):
    e = pl.program_id(1)
    h = h2_ref[...]
    gate = _dot(h, wg_ref[0])
    up = _dot(h, wu_ref[0])
    act = (gate * _sigmoid(gate) * up).astype(BF16)
    y = _dot(act, wd_ref[0])

    @pl.when(e == 0)
    def _():
        acc_ref[...] = y

    @pl.when(e > 0)
    def _():
        lane = lax.broadcasted_iota(jnp.int32, (TM_MOE, N_EXPERTS), 1)
        w = jnp.sum(jnp.where(lane == e - 1, comb_ref[...], 0.0), axis=1, keepdims=True)
        acc_ref[...] += w * y

    @pl.when(e == N_EXPERTS)
    def _():
        x2 = x1_ref[...] + mod_ref[0][5:6, :] * acc_ref[...]
        if final:
            x2 = _rms(x2, fw_ref[...])
        o_ref[...] = x2


def _moe(x1, h2, comb, mod, wg, wu, wd, fw, *, final):
    per_seq = DEC_SEQ // TM_MOE
    ctx_tiles = N_CTX // TM_MOE
    seq_of = lambda i: jnp.where(i < ctx_tiles, 0, 1 + (i - ctx_tiles) // per_seq)
    tok = lambda w: pl.BlockSpec((TM_MOE, w), lambda i, e: (i, 0))
    wspec = lambda a: pl.BlockSpec((1,) + a.shape[1:], lambda i, e: (e, 0, 0))
    return pl.pallas_call(
        functools.partial(_moe_kernel, final=final),
        out_shape=jax.ShapeDtypeStruct((N_TOK, D_MODEL), F32),
        grid=(N_TOK // TM_MOE, N_EXPERTS + 1),
        in_specs=[tok(D_MODEL), tok(D_MODEL), tok(N_EXPERTS),
                  pl.BlockSpec((1, N_MOD, D_MODEL), lambda i, e: (seq_of(i), 0, 0)),
                  wspec(wg), wspec(wu), wspec(wd),
                  pl.BlockSpec((1, D_MODEL), lambda i, e: (0, 0))],
        out_specs=tok(D_MODEL),
        scratch_shapes=[pltpu.VMEM((TM_MOE, D_MODEL), F32)],
        compiler_params=pltpu.CompilerParams(dimension_semantics=("arbitrary", "arbitrary"),
                                             vmem_limit_bytes=VMEM_LIMIT),
        name="moe_dense",
    )(x1, h2, comb, mod, wg, wu, wd, fw)


def _rope_tables():
    rows = DEC_SEQ // GRID_W
    r = jnp.repeat(jnp.arange(rows, dtype=F32), GRID_W)
    col = jnp.tile(jnp.arange(GRID_W, dtype=F32), rows)
    n_freq = MLA_ROPE // 4
    inv = ROPE_BASE ** (-jnp.arange(n_freq, dtype=F32) / n_freq)
    ang = jnp.stack([r[:, None] * inv, col[:, None] * inv], axis=1)
    expand = lambda t: jnp.broadcast_to(t[:, :, None, :], (DEC_SEQ, 2, 2, n_freq)).reshape(DEC_SEQ, MLA_ROPE)
    cos = jnp.concatenate([expand(jnp.cos(ang)), jnp.ones((SEQ, MLA_ROPE), F32)], axis=0)
    sin = jnp.concatenate([expand(jnp.sin(ang)), jnp.zeros((SEQ, MLA_ROPE), F32)], axis=0)
    return cos, sin


def _rot_cols(w):
    shp = w.shape
    w4 = w.reshape(shp[:-1] + (2, 2, MLA_ROPE // 4))
    return jnp.stack([-w4[..., 1, :], w4[..., 0, :]], axis=-2).reshape(shp)


def _gla_tables(reverse):
    blk, first, seq = [], [], []
    for s in range(N_SEQ):
        if s < BATCH:
            tiles = [s]
        else:
            base = CTX_TILES + (s - BATCH) * LAT_TILES_PER_SEQ
            tiles = list(range(base, base + LAT_TILES_PER_SEQ))
        if reverse:
            tiles = tiles[::-1]
        for n, t in enumerate(tiles):
            blk.append(t)
            first.append(1 if n == 0 else 0)
            seq.append(s)
    as_i32 = lambda v: jnp.asarray(v, dtype=jnp.int32)
    return as_i32(blk), as_i32(first), as_i32(seq)


def kernel(x_prompt, x_sample, cache_ckv, cache_krope, state_gla_fwd, state_gla_bwd, c, c_ctx, w_mod, b_mod, norm_mix, norm_ffn, w_in, w_alpha, b_alpha, gla_norm, q_norm, kv_norm, w_uq, w_uk, w_uv, w_pa, w_pb, w_o, w_router, b_router, w_exp_gate, w_exp_up, w_exp_down, w_sh_gate, w_sh_up, w_sh_down, final_norm):
    x = jnp.concatenate([x_prompt.reshape(N_CTX, D_MODEL), x_sample.reshape(N_LAT, D_MODEL)], axis=0)

    cond = jnp.concatenate([c_ctx[None, :], c, jnp.zeros((16 - 1 - DEC_BATCH, D_MODEL), F32)], axis=0)
    mod_all = _modulation(cond, w_mod.astype(BF16), b_mod[:, None, :])
    mod_all = mod_all.reshape(DEPTH, 16, N_MOD, D_MODEL)

    cos32, sin32 = _rope_tables()
    cq_tab = jnp.tile(cos32, (1, MLA_HEADS)) * ATT_SCALE
    sq_tab = jnp.tile(sin32, (1, MLA_HEADS)) * ATT_SCALE

    idx = jnp.arange(GLA_CHUNK)
    tri_f = (idx[None, :] <= idx[:, None]).astype(BF16)
    tri_b = (idx[None, :] >= idx[:, None]).astype(BF16)
    tab_f = _gla_tables(False)
    tab_b = _gla_tables(True)

    wr_hi = w_router.astype(BF16)
    wr_lo = (w_router - wr_hi.astype(F32)).astype(BF16)
    zpad = jnp.zeros((D_MODEL, 128 - 2 * N_EXPERTS), BF16)
    wr1 = jnp.concatenate([wr_hi, wr_lo, zpad], axis=1)
    wr2 = jnp.concatenate([wr_hi, jnp.zeros_like(wr_lo), zpad], axis=1)

    ckvs, krs, sfs, sbs = [], [], [], []
    for l in range(DEPTH):
        mod = mod_all[l]
        (w_gq, w_gk, w_gv, w_gg, w_gaf, w_gab, w_cq, w_ckv, w_kr, w_za, w_zb) = jnp.split(
            w_in[l], (256, 512, 1024, 1536, 1552, 1568, 1824, 1952, 1984, 3008), axis=1)
        w_small = jnp.concatenate([w_gaf, w_gab, w_kr, _rot_cols(w_kr), jnp.zeros((D_MODEL, 32), F32)], axis=1)
        w1 = jnp.concatenate([w_gq, w_gk, w_gv, w_gg, w_cq, w_ckv, w_small, w_za, w_zb], axis=1).astype(BF16)
        wa = jnp.zeros((128, 2 * HK), F32)
        wa = wa.at[0:GLA_GATE_RANK, 0:HK].set(w_alpha[l, 0])
        wa = wa.at[GLA_GATE_RANK:2 * GLA_GATE_RANK, HK:2 * HK].set(w_alpha[l, 1]).astype(BF16)
        ba = b_alpha[l].reshape(1, 2 * HK)
        wuq3 = w_uq[l].reshape(MLA_Q_RANK, MLA_HEADS, MLA_NOPE + MLA_ROPE)
        wuqn = wuq3[:, :, :MLA_NOPE].reshape(MLA_Q_RANK, MLA_HEADS * MLA_NOPE).astype(BF16)
        wuq_rope = wuq3[:, :, MLA_NOPE:]
        wuqr = wuq_rope.reshape(MLA_Q_RANK, MLA_HEADS * MLA_ROPE).astype(BF16)
        wuqrr = _rot_cols(wuq_rope).reshape(MLA_Q_RANK, MLA_HEADS * MLA_ROPE).astype(BF16)
        wukt = w_uk[l].reshape(MLA_KV_RANK, MLA_HEADS, MLA_NOPE).transpose(1, 2, 0).astype(BF16)
        wuvt = w_uv[l].reshape(MLA_KV_RANK, MLA_HEADS, MLA_DV).transpose(1, 2, 0).astype(BF16)

        gq, gk, gv, gg, la, qcat, ckvn, kr, za, zb = _inproj(
            x, mod, norm_mix[l][None, :], w1, wa, ba, q_norm[l][None, :], kv_norm[l][None, :],
            wuqn, wuqr, wuqrr, wukt, cq_tab, sq_tab, cos32, sin32)

        to_t = lambda s: s.transpose(0, 3, 1, 2).reshape(s.shape[0], GLA_DV, HK)
        zeros_ctx = jnp.zeros((BATCH, GLA_DV, HK), F32)
        s0f = jnp.concatenate([zeros_ctx, to_t(state_gla_fwd[:, l])], axis=0)
        s0b = jnp.concatenate([zeros_ctx, to_t(state_gla_bwd[:, l])], axis=0)
        o_f, sf = _gla_direction(tab_f, gq, gk, gv, gg, la, s0f, tri_f, None, None, reverse=False)
        o_a, sb = _gla_direction(tab_b, gq, gk, gv, gg, la, s0b, tri_b, o_f, gla_norm[l][None, :], reverse=True)

        ckv_ctx = ckvn[:N_CTX].reshape(BATCH, SEQ, MLA_KV_RANK)
        kr_ctx = kr[:N_CTX].reshape(BATCH, SEQ, MLA_ROPE)
        ckv_lat = jnp.concatenate([cache_ckv[:, l], ckvn[N_CTX:].reshape(DEC_BATCH, DEC_SEQ, MLA_KV_RANK)], axis=1)
        kr_lat = jnp.concatenate([cache_krope[:, l], kr[N_CTX:].reshape(DEC_BATCH, DEC_SEQ, MLA_ROPE)], axis=1)

        def kv_operands(ckv_all, kr_all, kv_block):
            b_, s_, _ = ckv_all.shape
            kcat = jnp.concatenate([ckv_all, kr_all], axis=-1).astype(BF16)
            vt = ckv_all.astype(BF16).reshape(b_, s_ // kv_block, kv_block, MLA_KV_RANK).transpose(0, 1, 3, 2)
            return kcat, vt

        kc, vt = kv_operands(ckv_ctx, kr_ctx, SEQ)
        o_b = _attention(qcat, kc, vt, wuvt, tile0=0, tiles_per_seq=1)
        kc, vt = kv_operands(ckv_lat, kr_lat, KV_BLOCK_LAT)
        o_b = _attention(qcat, kc, vt, wuvt, tile0=CTX_TILES, tiles_per_seq=LAT_TILES_PER_SEQ, prev=o_b)

        x1, h2, lg = _merge(x, mod, o_a, o_b, za, zb, w_pa[l].astype(BF16), w_pb[l].astype(BF16),
                            w_o[l].astype(BF16), norm_ffn[l][None, :], wr1, wr2)
        comb = _route(lg.T, b_router[:, None]).T

        wg = jnp.concatenate([w_sh_gate[l][None], w_exp_gate[l]], axis=0).astype(BF16)
        wu = jnp.concatenate([w_sh_up[l][None], w_exp_up[l]], axis=0).astype(BF16)
        wd = jnp.concatenate([w_sh_down[l][None], w_exp_down[l]], axis=0).astype(BF16)
        x = _moe(x1, h2, comb, mod, wg, wu, wd, final_norm[None, :], final=(l == DEPTH - 1))

        ckvs.append(ckv_ctx)
        krs.append(kr_ctx)
        from_t = lambda s: s[:BATCH].reshape(BATCH, GLA_DV, GLA_HEADS, GLA_DK).transpose(0, 2, 3, 1)
        sfs.append(from_t(sf))
        sbs.append(from_t(sb))

    y_prompt = x[:N_CTX].reshape(BATCH, SEQ, D_MODEL)
    y_sample = x[N_CTX:].reshape(DEC_BATCH, DEC_SEQ, D_MODEL)
    return (y_prompt, y_sample, jnp.stack(ckvs, axis=1), jnp.stack(krs, axis=1),
            jnp.stack(sfs, axis=1), jnp.stack(sbs, axis=1))
```

```python
import functools

import jax
import jax.numpy as jnp
from jax import lax
from jax.experimental import pallas as pl
from jax.experimental.pallas import tpu as pltpu

F32 = jnp.float32
BF16 = jnp.bfloat16

D_MODEL = 1024
BATCH = 16
SEQ = 256
DEPTH = 2
DEC_BATCH = 8
DEC_SEQ = 4096
PAST_LEN = 512
GRID_W = 64
GLA_HEADS = 4
GLA_DK = 64
GLA_DV = 128
GLA_GATE_RANK = 16
GLA_TAU = 16.0
GLA_CHUNK = 64
MLA_HEADS = 8
MLA_Q_RANK = 256
MLA_KV_RANK = 128
MLA_NOPE = 64
MLA_ROPE = 32
MLA_DV = 64
ROPE_BASE = 10000.0
N_EXPERTS = 16
N_GROUPS = 4
EXPERTS_PER_GROUP = 4
EXPERT_FF = 512
SHARED_FF = 512
N_MOD = 6
EPS = 1e-6

N_CTX = BATCH * SEQ
N_LAT = DEC_BATCH * DEC_SEQ
N_TOK = N_CTX + N_LAT
N_SEQ = BATCH + DEC_BATCH
TM = 256
N_TILES = N_TOK // TM
CTX_TILES = N_CTX // TM
LAT_TILES_PER_SEQ = DEC_SEQ // TM
HK = GLA_HEADS * GLA_DK
HV = GLA_HEADS * GLA_DV
QCAT = MLA_KV_RANK + MLA_ROPE
Q_SCALE = (MLA_NOPE + MLA_ROPE) ** -0.5 * 1.4426950408889634
TM_MOE = 1024
VMEM_LIMIT = 56 * 1024 * 1024

C_GQ, C_GK, C_GV, C_GG, C_CQ, C_CKV, C_SMALL, C_ZA, C_ZB, C_END = (
    0, 256, 512, 1024, 1536, 1792, 1920, 2048, 3072, 4096)


def _sigmoid(x):
    return 1.0 / (1.0 + jnp.exp(-x))


def _rms(x, w):
    return x * lax.rsqrt(jnp.mean(x * x, axis=-1, keepdims=True) + EPS) * w


def _dot(a, b):
    return jnp.dot(a, b, preferred_element_type=F32)


def _dot_nt(a, b):
    return lax.dot_general(a, b, (((1,), (1,)), ((), ())), preferred_element_type=F32)


def _dot_tn(a, b):
    return lax.dot_general(a, b, (((0,), (0,)), ((), ())), preferred_element_type=F32)


def _tile_seq(i):
    return jnp.where(i < CTX_TILES, 0, 1 + (i - CTX_TILES) // LAT_TILES_PER_SEQ)


def _tile_pos(i):
    return jnp.where(i < CTX_TILES, LAT_TILES_PER_SEQ, (i - CTX_TILES) % LAT_TILES_PER_SEQ)


def _mod_kernel(c_ref, w_ref, b_ref, o_ref):
    c = c_ref[...]
    sc = (c * _sigmoid(c)).astype(BF16)
    o_ref[...] = _dot(sc, w_ref[...]) + b_ref[...]


def _modulation(cond, w_mod, b_mod):
    nb = 1024
    return pl.pallas_call(
        _mod_kernel,
        out_shape=jax.ShapeDtypeStruct((DEPTH, 16, N_MOD * D_MODEL), F32),
        grid=(DEPTH, N_MOD * D_MODEL // nb),
        in_specs=[
            pl.BlockSpec((16, D_MODEL), lambda l, j: (0, 0)),
            pl.BlockSpec((None, D_MODEL, nb), lambda l, j: (l, 0, j)),
            pl.BlockSpec((None, 1, nb), lambda l, j: (l, 0, j)),
        ],
        out_specs=pl.BlockSpec((None, 16, nb), lambda l, j: (l, 0, j)),
        name="modulation",
    )(cond, w_mod, b_mod)


def _inproj_kernel(x_ref, mod_ref, nw_ref, w1_ref, wa_ref, ba_ref, qnw_ref, kvnw_ref,
                   wuqn_ref, wuqr_ref, wuqrr_ref, wukt_ref, cq_ref, sq_ref, ck_ref, sk_ref,
                   gq_ref, gk_ref, gv_ref, gg_ref, la_ref, q_ref, ckv_ref, kr_ref, za_ref, zb_ref):
    x = x_ref[...]
    mod = mod_ref[0]
    h = (_rms(x, nw_ref[...]) * (1.0 + mod[1:2, :]) + mod[0:1, :]).astype(BF16)

    def proj(lo, hi):
        return _dot(h, w1_ref[:, lo:hi])

    gq_ref[...] = (proj(C_GQ, C_GK) * (GLA_DK ** -0.5)).astype(BF16)
    gk_ref[...] = proj(C_GK, C_GV).astype(BF16)
    gv_ref[...] = proj(C_GV, C_GG).astype(BF16)
    gg_ref[...] = proj(C_GG, C_CQ).astype(BF16)
    za_ref[...] = _sigmoid(proj(C_ZA, C_ZB)).astype(BF16)
    zb_ref[...] = _sigmoid(proj(C_ZB, C_END)).astype(BF16)

    small = proj(C_SMALL, C_ZA)
    lin = _dot(small.astype(BF16), wa_ref[...]) + ba_ref[...]
    la_ref[...] = (jnp.minimum(lin, 0.0) - jnp.log(1.0 + jnp.exp(-jnp.abs(lin)))) * (1.0 / GLA_TAU)
    kr_ref[...] = small[:, 32:64] * ck_ref[...] + small[:, 64:96] * sk_ref[...]

    ckv_ref[...] = _rms(proj(C_CKV, C_SMALL), kvnw_ref[...])

    cqn = _rms(proj(C_CQ, C_CKV), qnw_ref[...]).astype(BF16)
    qn = _dot(cqn, wuqn_ref[...]).astype(BF16)
    qr = (_dot(cqn, wuqr_ref[...]) * cq_ref[...]
          + _dot(cqn, wuqrr_ref[...]) * sq_ref[...])
    for hd in range(MLA_HEADS):
        qa = _dot(qn[:, hd * MLA_NOPE:(hd + 1) * MLA_NOPE], wukt_ref[hd]) * Q_SCALE
        q_ref[hd, :, 0:MLA_KV_RANK] = qa.astype(BF16)
        q_ref[hd, :, MLA_KV_RANK:QCAT] = qr[:, hd * MLA_ROPE:(hd + 1) * MLA_ROPE].astype(BF16)


def _inproj(x, mod, nw, w1, wa, ba, qnw, kvnw, wuqn, wuqr, wuqrr, wukt, cq, sq, ck, sk):
    full = lambda a: pl.BlockSpec(a.shape, lambda i: (0,) * a.ndim)
    tok = lambda w: pl.BlockSpec((TM, w), lambda i: (i, 0))
    pos = lambda w: pl.BlockSpec((TM, w), lambda i: (_tile_pos(i), 0))
    out_shape = [
        jax.ShapeDtypeStruct((N_TOK, HK), BF16),
        jax.ShapeDtypeStruct((N_TOK, HK), BF16),
        jax.ShapeDtypeStruct((N_TOK, HV), BF16),
        jax.ShapeDtypeStruct((N_TOK, HV), BF16),
        jax.ShapeDtypeStruct((N_TOK, 2 * HK), F32),
        jax.ShapeDtypeStruct((MLA_HEADS, N_TOK, QCAT), BF16),
        jax.ShapeDtypeStruct((N_TOK, MLA_KV_RANK), F32),
        jax.ShapeDtypeStruct((N_TOK, MLA_ROPE), F32),
        jax.ShapeDtypeStruct((N_TOK, D_MODEL), BF16),
        jax.ShapeDtypeStruct((N_TOK, D_MODEL), BF16),
    ]
    out_specs = [tok(HK), tok(HK), tok(HV), tok(HV), tok(2 * HK),
                 pl.BlockSpec((MLA_HEADS, TM, QCAT), lambda i: (0, i, 0)),
                 tok(MLA_KV_RANK), tok(MLA_ROPE), tok(D_MODEL), tok(D_MODEL)]
    in_specs = [tok(D_MODEL),
                pl.BlockSpec((1, N_MOD, D_MODEL), lambda i: (_tile_seq(i), 0, 0)),
                full(nw), full(w1), full(wa), full(ba), full(qnw), full(kvnw),
                full(wuqn), full(wuqr), full(wuqrr), full(wukt),
                pos(HK), pos(HK), pos(MLA_ROPE), pos(MLA_ROPE)]
    return pl.pallas_call(
        _inproj_kernel, out_shape=out_shape, grid=(N_TILES,),
        in_specs=in_specs, out_specs=out_specs,
        compiler_params=pltpu.CompilerParams(dimension_semantics=("arbitrary",),
                                             vmem_limit_bytes=VMEM_LIMIT),
        name="inproj",
    )(x, mod, nw, w1, wa, ba, qnw, kvnw, wuqn, wuqr, wuqrr, wukt, cq, sq, ck, sk)


def _gla_kernel(blk_ref, first_ref, seq_ref, q_ref, k_ref, v_ref, g_ref, la_ref, s0_ref, tri_ref,
                *rest, reverse, final):
    if final:
        oprev_ref, nw_ref, o_ref, sfin_ref, st_ref = rest
    else:
        o_ref, sfin_ref, st_ref = rest
    step = pl.program_id(0)

    @pl.when(first_ref[step] == 1)
    def _():
        st_ref[...] = s0_ref[0]

    n_chunks = TM // GLA_CHUNK
    row = lax.broadcasted_iota(jnp.int32, (GLA_CHUNK, GLA_CHUNK), 0)
    col = lax.broadcasted_iota(jnp.int32, (GLA_CHUNK, GLA_CHUNK), 1)
    keep = (col >= row) if reverse else (col <= row)
    tri = tri_ref[...]

    def chunk(ci, carry):
        c = (n_chunks - 1 - ci) if reverse else ci
        r0 = pl.multiple_of(c * GLA_CHUNK, GLA_CHUNK)
        rows = pl.ds(r0, GLA_CHUNK)
        la = la_ref[rows, :]
        la_hi = la.astype(BF16)
        la_lo = (la - la_hi.astype(F32)).astype(BF16)
        cum = _dot(tri, la_hi) + _dot(tri, la_lo)
        edge = cum[0:1, :] if reverse else cum[GLA_CHUNK - 1:GLA_CHUNK, :]
        q = q_ref[rows, :].astype(F32)
        k = k_ref[rows, :].astype(F32)
        qd = (q * jnp.exp(cum)).astype(BF16)
        kin = (k * jnp.exp(-cum)).astype(BF16)
        kout = (k * jnp.exp(edge - cum)).astype(BF16)
        decay = jnp.exp(edge)
        st = st_ref[...]
        st_b = st.astype(BF16)
        for hd in range(GLA_HEADS):
            ks = slice(hd * GLA_DK, (hd + 1) * GLA_DK)
            vs = slice(hd * GLA_DV, (hd + 1) * GLA_DV)
            att = jnp.where(keep, _dot_nt(qd[:, ks], kin[:, ks]), 0.0)
            vh = v_ref[rows, vs]
            o = _dot(att.astype(BF16), vh) + _dot_nt(qd[:, ks], st_b[:, ks])
            st_ref[:, ks] = st[:, ks] * decay[:, ks] + _dot_tn(vh, kout[:, ks])
            if final:
                o = o + oprev_ref[rows, vs]
                o = _rms(o, nw_ref[...])
                gt = g_ref[rows, vs].astype(F32)
                o_ref[rows, vs] = (o * gt * _sigmoid(gt)).astype(o_ref.dtype)
            else:
                o_ref[rows, vs] = o
        return carry

    lax.fori_loop(0, n_chunks, chunk, 0)
    sfin_ref[0] = st_ref[...]


def _gla_direction(tables, q, k, v, g, la, s0t, tri, oprev, nw, *, reverse):
    final = oprev is not None
    blk, first, seq = tables
    tok = lambda w: pl.BlockSpec((TM, w), lambda s, b, f, q_: (b[s], 0))
    in_specs = [tok(HK), tok(HK), tok(HV), tok(HV),
                pl.BlockSpec((TM, HK), lambda s, b, f, q_: (b[s], 1 if reverse else 0)),
                pl.BlockSpec((1, GLA_DV, HK), lambda s, b, f, q_: (q_[s], 0, 0)),
                pl.BlockSpec((GLA_CHUNK, GLA_CHUNK), lambda s, b, f, q_: (0, 0))]
    args = [q, k, v, g, la, s0t, tri]
    if final:
        in_specs += [tok(HV), pl.BlockSpec((1, GLA_DV), lambda s, b, f, q_: (0, 0))]
        args += [oprev, nw]
    out_shape = [jax.ShapeDtypeStruct((N_TOK, HV), BF16 if final else F32),
                 jax.ShapeDtypeStruct((N_SEQ, GLA_DV, HK), F32)]
    out_specs = [tok(HV), pl.BlockSpec((1, GLA_DV, HK), lambda s, b, f, q_: (q_[s], 0, 0))]
    return pl.pallas_call(
        functools.partial(_gla_kernel, reverse=reverse, final=final),
        out_shape=out_shape,
        grid_spec=pltpu.PrefetchScalarGridSpec(
            num_scalar_prefetch=3, grid=(N_TILES,), in_specs=in_specs, out_specs=out_specs,
            scratch_shapes=[pltpu.VMEM((GLA_DV, HK), F32)]),
        compiler_params=pltpu.CompilerParams(dimension_semantics=("arbitrary",)),
        name="gla_bwd" if reverse else "gla_fwd",
    )(blk, first, seq, *args)


def _attn_kernel(q_ref, k_ref, vt_ref, wuvt_ref, o_ref, ot_ref):
    def head(hd, carry):
        s = _dot_nt(k_ref[0], q_ref[hd])
        m = jnp.max(s, axis=0, keepdims=True)
        p = jnp.exp2(s - m)
        l = jnp.sum(p, axis=0, keepdims=True)
        acc = _dot(vt_ref[0], p.astype(BF16))
        lat = (acc / l).astype(BF16)
        ot_ref[hd] = _dot(wuvt_ref[hd], lat)
        return carry

    lax.fori_loop(0, MLA_HEADS, head, 0)
    o_ref[...] = ot_ref[...].reshape(MLA_HEADS * MLA_DV, TM).T.astype(o_ref.dtype)


def _attention(q, kcat, vt, wuvt, *, tile0, tiles_per_seq, prev=None):
    n_seq, s_len, _ = kcat.shape
    in_specs = [
        pl.BlockSpec((MLA_HEADS, TM, QCAT), lambda b, i: (0, tile0 + b * tiles_per_seq + i, 0)),
        pl.BlockSpec((1, s_len, QCAT), lambda b, i: (b, 0, 0)),
        pl.BlockSpec((1, MLA_KV_RANK, s_len), lambda b, i: (b, 0, 0)),
        pl.BlockSpec(wuvt.shape, lambda b, i: (0, 0, 0)),
    ]
    args = [q, kcat, vt, wuvt]
    kernel = _attn_kernel
    aliases = {}
    if prev is not None:
        in_specs.append(pl.BlockSpec(memory_space=pl.ANY))
        args.append(prev)
        aliases = {4: 0}
        kernel = lambda q_, k_, v_, w_, p_, o_, t_: _attn_kernel(q_, k_, v_, w_, o_, t_)
    return pl.pallas_call(
        kernel,
        out_shape=jax.ShapeDtypeStruct((N_TOK, MLA_HEADS * MLA_DV), BF16),
        grid=(n_seq, tiles_per_seq),
        in_specs=in_specs,
        out_specs=pl.BlockSpec((TM, MLA_HEADS * MLA_DV),
                               lambda b, i: (tile0 + b * tiles_per_seq + i, 0)),
        scratch_shapes=[pltpu.VMEM((MLA_HEADS, MLA_DV, TM), F32)],
        input_output_aliases=aliases,
        compiler_params=pltpu.CompilerParams(dimension_semantics=("arbitrary", "arbitrary"),
                                             vmem_limit_bytes=VMEM_LIMIT),
        name="mla_ctx" if prev is None else "mla_lat",
    )(*args)


def _merge_kernel(x_ref, mod_ref, oa_ref, ob_ref, za_ref, zb_ref, wpa_ref, wpb_ref, wo_ref, nw_ref,
                  wr1_ref, wr2_ref, x1_ref, h2_ref, lg_ref):
    mod = mod_ref[0]
    y = (za_ref[...].astype(F32) * _dot(oa_ref[...], wpa_ref[...])
         + zb_ref[...].astype(F32) * _dot(ob_ref[...], wpb_ref[...]))
    out = _dot(y.astype(BF16), wo_ref[...])
    x1 = x_ref[...] + mod[2:3, :] * out
    x1_ref[...] = x1
    h2 = _rms(x1, nw_ref[...]) * (1.0 + mod[4:5, :]) + mod[3:4, :]
    h2_hi = h2.astype(BF16)
    h2_lo = (h2 - h2_hi.astype(F32)).astype(BF16)
    h2_ref[...] = h2_hi
    d1 = _dot(h2_hi, wr1_ref[...])
    d2 = _dot(h2_lo, wr2_ref[...])
    lg_ref[...] = (d1[:, 0:N_EXPERTS] + d1[:, N_EXPERTS:2 * N_EXPERTS]) + d2[:, 0:N_EXPERTS]


def _merge(x, mod, oa, ob, za, zb, wpa, wpb, wo, nw, wr1, wr2):
    full = lambda a: pl.BlockSpec(a.shape, lambda i: (0,) * a.ndim)
    tok = lambda w: pl.BlockSpec((TM, w), lambda i: (i, 0))
    return pl.pallas_call(
        _merge_kernel,
        out_shape=[jax.ShapeDtypeStruct((N_TOK, D_MODEL), F32),
                   jax.ShapeDtypeStruct((N_TOK, D_MODEL), BF16),
                   jax.ShapeDtypeStruct((N_TOK, N_EXPERTS), F32)],
        grid=(N_TILES,),
        in_specs=[tok(D_MODEL), pl.BlockSpec((1, N_MOD, D_MODEL), lambda i: (_tile_seq(i), 0, 0)),
                  tok(HV), tok(MLA_HEADS * MLA_DV), tok(D_MODEL), tok(D_MODEL),
                  full(wpa), full(wpb), full(wo), full(nw), full(wr1), full(wr2)],
        out_specs=[tok(D_MODEL), tok(D_MODEL), tok(N_EXPERTS)],
        compiler_params=pltpu.CompilerParams(dimension_semantics=("arbitrary",),
                                             vmem_limit_bytes=VMEM_LIMIT),
        name="merge",
    )(x, mod, oa, ob, za, zb, wpa, wpb, wo, nw, wr1, wr2)


def _route_kernel(lg_ref, b_ref, comb_ref):
    aff = _sigmoid(lg_ref[...])
    biased = aff + b_ref[...]
    row = lambda a, e: a[e:e + 1, :]
    best = None
    sel = None
    for g in range(N_GROUPS):
        b = [row(biased, g * EXPERTS_PER_GROUP + i) for i in range(EXPERTS_PER_GROUP)]
        score = None
        for i in range(EXPERTS_PER_GROUP):
            for j in range(i + 1, EXPERTS_PER_GROUP):
                pair = b[i] + b[j]
                score = pair if score is None else jnp.maximum(score, pair)
        if g == 0:
            best, sel = score, jnp.zeros_like(score, dtype=jnp.int32)
        else:
            better = score > best
            best = jnp.where(better, score, best)
            sel = jnp.where(better, g, sel)
    cb, ca = [], []
    for i in range(EXPERTS_PER_GROUP):
        vb = row(biased, i)
        va = row(aff, i)
        for g in range(1, N_GROUPS):
            vb = jnp.where(sel == g, row(biased, g * EXPERTS_PER_GROUP + i), vb)
            va = jnp.where(sel == g, row(aff, g * EXPERTS_PER_GROUP + i), va)
        cb.append(vb)
        ca.append(va)
    picked = []
    for i in range(EXPERTS_PER_GROUP):
        rank = jnp.zeros_like(sel)
        for j in range(EXPERTS_PER_GROUP):
            if j == i:
                continue
            ahead = (cb[j] >= cb[i]) if j < i else (cb[j] > cb[i])
            rank = rank + ahead.astype(jnp.int32)
        picked.append(rank < 2)
    denom = None
    for i in range(EXPERTS_PER_GROUP):
        term = jnp.where(picked[i], ca[i], 0.0)
        denom = term if denom is None else denom + term
    for g in range(N_GROUPS):
        for i in range(EXPERTS_PER_GROUP):
            e = g * EXPERTS_PER_GROUP + i
            comb_ref[e:e + 1, :] = jnp.where((sel == g) & picked[i], ca[i] / denom, 0.0)


def _route(lg_t, b_router):
    w = 2048
    return pl.pallas_call(
        _route_kernel,
        out_shape=jax.ShapeDtypeStruct((N_EXPERTS, N_TOK), F32),
        grid=(N_TOK // w,),
        in_specs=[pl.BlockSpec((N_EXPERTS, w), lambda i: (0, i)),
                  pl.BlockSpec((N_EXPERTS, 1), lambda i: (0, 0))],
        out_specs=pl.BlockSpec((N_EXPERTS, w), lambda i: (0, i)),
        name="route",
    )(lg_t, b_router)


def _moe_kernel(x1_ref, h2_ref, comb_ref, mod_ref, wg_ref, wu_ref, wd_ref, fw_ref, o_ref, acc_ref, *, final):
    e = pl.program_id(1)
    h = h2_ref[...]
    gate = _dot(h, wg_ref[0])
    up = _dot(h, wu_ref[0])
    act = (gate * _sigmoid(gate) * up).astype(BF16)
    y = _dot(act, wd_ref[0])

    @pl.when(e == 0)
    def _():
        acc_ref[...] = y

    @pl.when(e > 0)
    def _():
        lane = lax.broadcasted_iota(jnp.int32, (TM_MOE, N_EXPERTS), 1)
        w = jnp.sum(jnp.where(lane == e - 1, comb_ref[...], 0.0), axis=1, keepdims=True)
        acc_ref[...] += w * y

    @pl.when(e == N_EXPERTS)
    def _():
        x2 = x1_ref[...] + mod_ref[0][5:6, :] * acc_ref[...]
        if final:
            x2 = _rms(x2, fw_ref[...])
        o_ref[...] = x2


def _moe(x1, h2, comb, mod, wg, wu, wd, fw, *, final):
    per_seq = DEC_SEQ // TM_MOE
    ctx_tiles = N_CTX // TM_MOE
    seq_of = lambda i: jnp.where(i < ctx_tiles, 0, 1 + (i - ctx_tiles) // per_seq)
    tok = lambda w: pl.BlockSpec((TM_MOE, w), lambda i, e: (i, 0))
    wspec = lambda a: pl.BlockSpec((1,) + a.shape[1:], lambda i, e: (e, 0, 0))
    return pl.pallas_call(
        functools.partial(_moe_kernel, final=final),
        out_shape=jax.ShapeDtypeStruct((N_TOK, D_MODEL), F32),
        grid=(N_TOK // TM_MOE, N_EXPERTS + 1),
        in_specs=[tok(D_MODEL), tok(D_MODEL), tok(N_EXPERTS),
                  pl.BlockSpec((1, N_MOD, D_MODEL), lambda i, e: (seq_of(i), 0, 0)),
                  wspec(wg), wspec(wu), wspec(wd),
                  pl.BlockSpec((1, D_MODEL), lambda i, e: (0, 0))],
        out_specs=tok(D_MODEL),
        scratch_shapes=[pltpu.VMEM((TM_MOE, D_MODEL), F32)],
        compiler_params=pltpu.CompilerParams(dimension_semantics=("arbitrary", "arbitrary"),
                                             vmem_limit_bytes=VMEM_LIMIT),
        name="moe_dense",
    )(x1, h2, comb, mod, wg, wu, wd, fw)


def _rope_tables():
    rows = DEC_SEQ // GRID_W
    r = jnp.repeat(jnp.arange(rows, dtype=F32), GRID_W)
    col = jnp.tile(jnp.arange(GRID_W, dtype=F32), rows)
    n_freq = MLA_ROPE // 4
    inv = ROPE_BASE ** (-jnp.arange(n_freq, dtype=F32) / n_freq)
    ang = jnp.stack([r[:, None] * inv, col[:, None] * inv], axis=1)
    expand = lambda t: jnp.broadcast_to(t[:, :, None, :], (DEC_SEQ, 2, 2, n_freq)).reshape(DEC_SEQ, MLA_ROPE)
    cos = jnp.concatenate([expand(jnp.cos(ang)), jnp.ones((SEQ, MLA_ROPE), F32)], axis=0)
    sin = jnp.concatenate([expand(jnp.sin(ang)), jnp.zeros((SEQ, MLA_ROPE), F32)], axis=0)
    return cos, sin


def _rot_cols(w):
    shp = w.shape
    w4 = w.reshape(shp[:-1] + (2, 2, MLA_ROPE // 4))
    return jnp.stack([-w4[..., 1, :], w4[..., 0, :]], axis=-2).reshape(shp)


def _gla_tables(reverse):
    blk, first, seq = [], [], []
    for s in range(N_SEQ):
        if s < BATCH:
            tiles = [s]
        else:
            base = CTX_TILES + (s - BATCH) * LAT_TILES_PER_SEQ
            tiles = list(range(base, base + LAT_TILES_PER_SEQ))
        if reverse:
            tiles = tiles[::-1]
        for n, t in enumerate(tiles):
            blk.append(t)
            first.append(1 if n == 0 else 0)
            seq.append(s)
    as_i32 = lambda v: jnp.asarray(v, dtype=jnp.int32)
    return as_i32(blk), as_i32(first), as_i32(seq)


def kernel(x_prompt, x_sample, cache_ckv, cache_krope, state_gla_fwd, state_gla_bwd, c, c_ctx, w_mod, b_mod, norm_mix, norm_ffn, w_in, w_alpha, b_alpha, gla_norm, q_norm, kv_norm, w_uq, w_uk, w_uv, w_pa, w_pb, w_o, w_router, b_router, w_exp_gate, w_exp_up, w_exp_down, w_sh_gate, w_sh_up, w_sh_down, final_norm):
    x = jnp.concatenate([x_prompt.reshape(N_CTX, D_MODEL), x_sample.reshape(N_LAT, D_MODEL)], axis=0)

    cond = jnp.concatenate([c_ctx[None, :], c, jnp.zeros((16 - 1 - DEC_BATCH, D_MODEL), F32)], axis=0)
    mod_all = _modulation(cond, w_mod.astype(BF16), b_mod[:, None, :])
    mod_all = mod_all.reshape(DEPTH, 16, N_MOD, D_MODEL)

    cos32, sin32 = _rope_tables()
    cq_tab = jnp.tile(cos32, (1, MLA_HEADS)) * Q_SCALE
    sq_tab = jnp.tile(sin32, (1, MLA_HEADS)) * Q_SCALE

    idx = jnp.arange(GLA_CHUNK)
    tri_f = (idx[None, :] <= idx[:, None]).astype(BF16)
    tri_b = (idx[None, :] >= idx[:, None]).astype(BF16)
    tab_f = _gla_tables(False)
    tab_b = _gla_tables(True)

    wr_hi = w_router.astype(BF16)
    wr_lo = (w_router - wr_hi.astype(F32)).astype(BF16)
    zpad = jnp.zeros((D_MODEL, 128 - 2 * N_EXPERTS), BF16)
    wr1 = jnp.concatenate([wr_hi, wr_lo, zpad], axis=1)
    wr2 = jnp.concatenate([wr_hi, jnp.zeros_like(wr_lo), zpad], axis=1)

    ckvs, krs, sfs, sbs = [], [], [], []
    for l in range(DEPTH):
        mod = mod_all[l]
        (w_gq, w_gk, w_gv, w_gg, w_gaf, w_gab, w_cq, w_ckv, w_kr, w_za, w_zb) = jnp.split(
            w_in[l], (256, 512, 1024, 1536, 1552, 1568, 1824, 1952, 1984, 3008), axis=1)
        w_small = jnp.concatenate([w_gaf, w_gab, w_kr, _rot_cols(w_kr), jnp.zeros((D_MODEL, 32), F32)], axis=1)
        w1 = jnp.concatenate([w_gq, w_gk, w_gv, w_gg, w_cq, w_ckv, w_small, w_za, w_zb], axis=1).astype(BF16)
        wa = jnp.zeros((128, 2 * HK), F32)
        wa = wa.at[0:GLA_GATE_RANK, 0:HK].set(w_alpha[l, 0])
        wa = wa.at[GLA_GATE_RANK:2 * GLA_GATE_RANK, HK:2 * HK].set(w_alpha[l, 1]).astype(BF16)
        ba = b_alpha[l].reshape(1, 2 * HK)
        wuq3 = w_uq[l].reshape(MLA_Q_RANK, MLA_HEADS, MLA_NOPE + MLA_ROPE)
        wuqn = wuq3[:, :, :MLA_NOPE].reshape(MLA_Q_RANK, MLA_HEADS * MLA_NOPE).astype(BF16)
        wuq_rope = wuq3[:, :, MLA_NOPE:]
        wuqr = wuq_rope.reshape(MLA_Q_RANK, MLA_HEADS * MLA_ROPE).astype(BF16)
        wuqrr = _rot_cols(wuq_rope).reshape(MLA_Q_RANK, MLA_HEADS * MLA_ROPE).astype(BF16)
        wukt = w_uk[l].reshape(MLA_KV_RANK, MLA_HEADS, MLA_NOPE).transpose(1, 2, 0).astype(BF16)
        wuvt = w_uv[l].reshape(MLA_KV_RANK, MLA_HEADS, MLA_DV).transpose(1, 2, 0).astype(BF16)

        gq, gk, gv, gg, la, qcat, ckvn, kr, za, zb = _inproj(
            x, mod, norm_mix[l][None, :], w1, wa, ba, q_norm[l][None, :], kv_norm[l][None, :],
            wuqn, wuqr, wuqrr, wukt, cq_tab, sq_tab, cos32, sin32)

        to_t = lambda s: s.transpose(0, 3, 1, 2).reshape(s.shape[0], GLA_DV, HK)
        zeros_ctx = jnp.zeros((BATCH, GLA_DV, HK), F32)
        s0f = jnp.concatenate([zeros_ctx, to_t(state_gla_fwd[:, l])], axis=0)
        s0b = jnp.concatenate([zeros_ctx, to_t(state_gla_bwd[:, l])], axis=0)
        o_f, sf = _gla_direction(tab_f, gq, gk, gv, gg, la, s0f, tri_f, None, None, reverse=False)
        o_a, sb = _gla_direction(tab_b, gq, gk, gv, gg, la, s0b, tri_b, o_f, gla_norm[l][None, :], reverse=True)

        ckv_ctx = ckvn[:N_CTX].reshape(BATCH, SEQ, MLA_KV_RANK)
        kr_ctx = kr[:N_CTX].reshape(BATCH, SEQ, MLA_ROPE)
        ckv_lat = jnp.concatenate([cache_ckv[:, l], ckvn[N_CTX:].reshape(DEC_BATCH, DEC_SEQ, MLA_KV_RANK)], axis=1)
        kr_lat = jnp.concatenate([cache_krope[:, l], kr[N_CTX:].reshape(DEC_BATCH, DEC_SEQ, MLA_ROPE)], axis=1)

        def kv_operands(ckv_all, kr_all):
            kcat = jnp.concatenate([ckv_all, kr_all], axis=-1).astype(BF16)
            return kcat, ckv_all.astype(BF16).transpose(0, 2, 1)

        kc, vt = kv_operands(ckv_ctx, kr_ctx)
        o_b = _attention(qcat, kc, vt, wuvt, tile0=0, tiles_per_seq=1)
        kc, vt = kv_operands(ckv_lat, kr_lat)
        o_b = _attention(qcat, kc, vt, wuvt, tile0=CTX_TILES, tiles_per_seq=LAT_TILES_PER_SEQ, prev=o_b)

        x1, h2, lg = _merge(x, mod, o_a, o_b, za, zb, w_pa[l].astype(BF16), w_pb[l].astype(BF16),
                            w_o[l].astype(BF16), norm_ffn[l][None, :], wr1, wr2)
        comb = _route(lg.T, b_router[:, None]).T

        wg = jnp.concatenate([w_sh_gate[l][None], w_exp_gate[l]], axis=0).astype(BF16)
        wu = jnp.concatenate([w_sh_up[l][None], w_exp_up[l]], axis=0).astype(BF16)
        wd = jnp.concatenate([w_sh_down[l][None], w_exp_down[l]], axis=0).astype(BF16)
        x = _moe(x1, h2, comb, mod, wg, wu, wd, final_norm[None, :], final=(l == DEPTH - 1))

        ckvs.append(ckv_ctx)
        krs.append(kr_ctx)
        from_t = lambda s: s[:BATCH].reshape(BATCH, GLA_DV, GLA_HEADS, GLA_DK).transpose(0, 2, 3, 1)
        sfs.append(from_t(sf))
        sbs.append(from_t(sb))

    y_prompt = x[:N_CTX].reshape(BATCH, SEQ, D_MODEL)
    y_sample = x[N_CTX:].reshape(DEC_BATCH, DEC_SEQ, D_MODEL)
    return (y_prompt, y_sample, jnp.stack(ckvs, axis=1), jnp.stack(krs, axis=1),
            jnp.stack(sfs, axis=1), jnp.stack(sbs, axis=1))
```

```python
import functools

import jax
import jax.numpy as jnp
from jax import lax
from jax.experimental import pallas as pl
from jax.experimental.pallas import tpu as pltpu

F32 = jnp.float32
BF16 = jnp.bfloat16

D_MODEL = 1024
BATCH = 16
SEQ = 256
DEPTH = 2
DEC_BATCH = 8
DEC_SEQ = 4096
PAST_LEN = 512
GRID_W = 64
GLA_HEADS = 4
GLA_DK = 64
GLA_DV = 128
GLA_GATE_RANK = 16
GLA_TAU = 16.0
GLA_CHUNK = 64
MLA_HEADS = 8
MLA_Q_RANK = 256
MLA_KV_RANK = 128
MLA_NOPE = 64
MLA_ROPE = 32
MLA_DV = 64
ROPE_BASE = 10000.0
N_EXPERTS = 16
N_GROUPS = 4
EXPERTS_PER_GROUP = 4
EXPERT_FF = 512
SHARED_FF = 512
N_MOD = 6
EPS = 1e-6

N_CTX = BATCH * SEQ
N_LAT = DEC_BATCH * DEC_SEQ
N_TOK = N_CTX + N_LAT
N_SEQ = BATCH + DEC_BATCH
TM = 256
N_TILES = N_TOK // TM
CTX_TILES = N_CTX // TM
LAT_TILES_PER_SEQ = DEC_SEQ // TM
HK = GLA_HEADS * GLA_DK
HV = GLA_HEADS * GLA_DV
QCAT = MLA_KV_RANK + MLA_ROPE
Q_SCALE = (MLA_NOPE + MLA_ROPE) ** -0.5 * 1.4426950408889634
TE = 512
N_SLOT_TILES = N_TOK // TE + N_GROUPS
N_SLOTS = N_SLOT_TILES * TE
XW = D_MODEL + 128
ROUTE_W = 2048
VMEM_LIMIT = 56 * 1024 * 1024

C_GQ, C_GK, C_GV, C_GG, C_CQ, C_CKV, C_SMALL, C_ZA, C_ZB, C_END = (
    0, 256, 512, 1024, 1536, 1792, 1920, 2048, 3072, 4096)


def _sigmoid(x):
    return 1.0 / (1.0 + jnp.exp(-x))


def _rms(x, w):
    return x * lax.rsqrt(jnp.mean(x * x, axis=-1, keepdims=True) + EPS) * w


def _dot(a, b):
    return jnp.dot(a, b, preferred_element_type=F32)


def _dot_nt(a, b):
    return lax.dot_general(a, b, (((1,), (1,)), ((), ())), preferred_element_type=F32)


def _dot_tn(a, b):
    return lax.dot_general(a, b, (((0,), (0,)), ((), ())), preferred_element_type=F32)


def _tile_seq(i):
    return jnp.where(i < CTX_TILES, 0, 1 + (i - CTX_TILES) // LAT_TILES_PER_SEQ)


def _tile_pos(i):
    return jnp.where(i < CTX_TILES, LAT_TILES_PER_SEQ, (i - CTX_TILES) % LAT_TILES_PER_SEQ)


def _mod_kernel(c_ref, w_ref, b_ref, o_ref):
    c = c_ref[...]
    sc = (c * _sigmoid(c)).astype(BF16)
    o_ref[...] = _dot(sc, w_ref[...]) + b_ref[...]


def _modulation(cond, w_mod, b_mod):
    nb = 1024
    return pl.pallas_call(
        _mod_kernel,
        out_shape=jax.ShapeDtypeStruct((DEPTH, 16, N_MOD * D_MODEL), F32),
        grid=(DEPTH, N_MOD * D_MODEL // nb),
        in_specs=[
            pl.BlockSpec((16, D_MODEL), lambda l, j: (0, 0)),
            pl.BlockSpec((None, D_MODEL, nb), lambda l, j: (l, 0, j)),
            pl.BlockSpec((None, 1, nb), lambda l, j: (l, 0, j)),
        ],
        out_specs=pl.BlockSpec((None, 16, nb), lambda l, j: (l, 0, j)),
        name="modulation",
    )(cond, w_mod, b_mod)


def _inproj_kernel(x_ref, mod_ref, nw_ref, w1_ref, wa_ref, ba_ref, qnw_ref, kvnw_ref,
                   wuqn_ref, wuqr_ref, wuqrr_ref, wukt_ref, cq_ref, sq_ref, ck_ref, sk_ref,
                   gq_ref, gk_ref, gv_ref, gg_ref, la_ref, q_ref, ckv_ref, kr_ref, za_ref, zb_ref):
    x = x_ref[...]
    mod = mod_ref[0]
    h = (_rms(x, nw_ref[...]) * (1.0 + mod[1:2, :]) + mod[0:1, :]).astype(BF16)

    def proj(lo, hi):
        return _dot(h, w1_ref[:, lo:hi])

    gq_ref[...] = (proj(C_GQ, C_GK) * (GLA_DK ** -0.5)).astype(BF16)
    gk_ref[...] = proj(C_GK, C_GV).astype(BF16)
    gv_ref[...] = proj(C_GV, C_GG).astype(BF16)
    gg_ref[...] = proj(C_GG, C_CQ).astype(BF16)
    za_ref[...] = _sigmoid(proj(C_ZA, C_ZB)).astype(BF16)
    zb_ref[...] = _sigmoid(proj(C_ZB, C_END)).astype(BF16)

    small = proj(C_SMALL, C_ZA)
    lin = _dot(small.astype(BF16), wa_ref[...]) + ba_ref[...]
    la_ref[...] = (jnp.minimum(lin, 0.0) - jnp.log(1.0 + jnp.exp(-jnp.abs(lin)))) * (1.0 / GLA_TAU)
    kr_ref[...] = small[:, 32:64] * ck_ref[...] + small[:, 64:96] * sk_ref[...]

    ckv_ref[...] = _rms(proj(C_CKV, C_SMALL), kvnw_ref[...])

    cqn = _rms(proj(C_CQ, C_CKV), qnw_ref[...]).astype(BF16)
    qn = _dot(cqn, wuqn_ref[...]).astype(BF16)
    qr = (_dot(cqn, wuqr_ref[...]) * cq_ref[...]
          + _dot(cqn, wuqrr_ref[...]) * sq_ref[...])
    for hd in range(MLA_HEADS):
        qa = _dot(qn[:, hd * MLA_NOPE:(hd + 1) * MLA_NOPE], wukt_ref[hd]) * Q_SCALE
        q_ref[hd, :, 0:MLA_KV_RANK] = qa.astype(BF16)
        q_ref[hd, :, MLA_KV_RANK:QCAT] = qr[:, hd * MLA_ROPE:(hd + 1) * MLA_ROPE].astype(BF16)


def _inproj(x, mod, nw, w1, wa, ba, qnw, kvnw, wuqn, wuqr, wuqrr, wukt, cq, sq, ck, sk):
    full = lambda a: pl.BlockSpec(a.shape, lambda i: (0,) * a.ndim)
    tok = lambda w: pl.BlockSpec((TM, w), lambda i: (i, 0))
    pos = lambda w: pl.BlockSpec((TM, w), lambda i: (_tile_pos(i), 0))
    out_shape = [
        jax.ShapeDtypeStruct((N_TOK, HK), BF16),
        jax.ShapeDtypeStruct((N_TOK, HK), BF16),
        jax.ShapeDtypeStruct((N_TOK, HV), BF16),
        jax.ShapeDtypeStruct((N_TOK, HV), BF16),
        jax.ShapeDtypeStruct((N_TOK, 2 * HK), F32),
        jax.ShapeDtypeStruct((MLA_HEADS, N_TOK, QCAT), BF16),
        jax.ShapeDtypeStruct((N_TOK, MLA_KV_RANK), F32),
        jax.ShapeDtypeStruct((N_TOK, MLA_ROPE), F32),
        jax.ShapeDtypeStruct((N_TOK, D_MODEL), BF16),
        jax.ShapeDtypeStruct((N_TOK, D_MODEL), BF16),
    ]
    out_specs = [tok(HK), tok(HK), tok(HV), tok(HV), tok(2 * HK),
                 pl.BlockSpec((MLA_HEADS, TM, QCAT), lambda i: (0, i, 0)),
                 tok(MLA_KV_RANK), tok(MLA_ROPE), tok(D_MODEL), tok(D_MODEL)]
    in_specs = [tok(D_MODEL),
                pl.BlockSpec((1, N_MOD, D_MODEL), lambda i: (_tile_seq(i), 0, 0)),
                full(nw), full(w1), full(wa), full(ba), full(qnw), full(kvnw),
                full(wuqn), full(wuqr), full(wuqrr), full(wukt),
                pos(HK), pos(HK), pos(MLA_ROPE), pos(MLA_ROPE)]
    return pl.pallas_call(
        _inproj_kernel, out_shape=out_shape, grid=(N_TILES,),
        in_specs=in_specs, out_specs=out_specs,
        compiler_params=pltpu.CompilerParams(dimension_semantics=("arbitrary",),
                                             vmem_limit_bytes=VMEM_LIMIT),
        name="inproj",
    )(x, mod, nw, w1, wa, ba, qnw, kvnw, wuqn, wuqr, wuqrr, wukt, cq, sq, ck, sk)


def _gla_kernel(blk_ref, first_ref, seq_ref, q_ref, k_ref, v_ref, g_ref, la_ref, s0_ref, tri_ref,
                *rest, reverse, final):
    if final:
        oprev_ref, nw_ref, o_ref, sfin_ref, st_ref = rest
    else:
        o_ref, sfin_ref, st_ref = rest
    step = pl.program_id(0)

    @pl.when(first_ref[step] == 1)
    def _():
        st_ref[...] = s0_ref[0]

    n_chunks = TM // GLA_CHUNK
    row = lax.broadcasted_iota(jnp.int32, (GLA_CHUNK, GLA_CHUNK), 0)
    col = lax.broadcasted_iota(jnp.int32, (GLA_CHUNK, GLA_CHUNK), 1)
    keep = (col >= row) if reverse else (col <= row)
    tri = tri_ref[...]

    def chunk(ci, carry):
        c = (n_chunks - 1 - ci) if reverse else ci
        r0 = pl.multiple_of(c * GLA_CHUNK, GLA_CHUNK)
        rows = pl.ds(r0, GLA_CHUNK)
        la = la_ref[rows, :]
        la_hi = la.astype(BF16)
        la_lo = (la - la_hi.astype(F32)).astype(BF16)
        cum = _dot(tri, la_hi) + _dot(tri, la_lo)
        edge = cum[0:1, :] if reverse else cum[GLA_CHUNK - 1:GLA_CHUNK, :]
        q = q_ref[rows, :].astype(F32)
        k = k_ref[rows, :].astype(F32)
        qd = (q * jnp.exp(cum)).astype(BF16)
        kin = (k * jnp.exp(-cum)).astype(BF16)
        kout = (k * jnp.exp(edge - cum)).astype(BF16)
        decay = jnp.exp(edge)
        st = st_ref[...]
        st_b = st.astype(BF16)
        for hd in range(GLA_HEADS):
            ks = slice(hd * GLA_DK, (hd + 1) * GLA_DK)
            vs = slice(hd * GLA_DV, (hd + 1) * GLA_DV)
            att = jnp.where(keep, _dot_nt(qd[:, ks], kin[:, ks]), 0.0)
            vh = v_ref[rows, vs]
            o = _dot(att.astype(BF16), vh) + _dot_nt(qd[:, ks], st_b[:, ks])
            st_ref[:, ks] = st[:, ks] * decay[:, ks] + _dot_tn(vh, kout[:, ks])
            if final:
                o = o + oprev_ref[rows, vs]
                o = _rms(o, nw_ref[...])
                gt = g_ref[rows, vs].astype(F32)
                o_ref[rows, vs] = (o * gt * _sigmoid(gt)).astype(o_ref.dtype)
            else:
                o_ref[rows, vs] = o
        return carry

    lax.fori_loop(0, n_chunks, chunk, 0)
    sfin_ref[0] = st_ref[...]


def _gla_direction(tables, q, k, v, g, la, s0t, tri, oprev, nw, *, reverse):
    final = oprev is not None
    blk, first, seq = tables
    tok = lambda w: pl.BlockSpec((TM, w), lambda s, b, f, q_: (b[s], 0))
    in_specs = [tok(HK), tok(HK), tok(HV), tok(HV),
                pl.BlockSpec((TM, HK), lambda s, b, f, q_: (b[s], 1 if reverse else 0)),
                pl.BlockSpec((1, GLA_DV, HK), lambda s, b, f, q_: (q_[s], 0, 0)),
                pl.BlockSpec((GLA_CHUNK, GLA_CHUNK), lambda s, b, f, q_: (0, 0))]
    args = [q, k, v, g, la, s0t, tri]
    if final:
        in_specs += [tok(HV), pl.BlockSpec((1, GLA_DV), lambda s, b, f, q_: (0, 0))]
        args += [oprev, nw]
    out_shape = [jax.ShapeDtypeStruct((N_TOK, HV), BF16 if final else F32),
                 jax.ShapeDtypeStruct((N_SEQ, GLA_DV, HK), F32)]
    out_specs = [tok(HV), pl.BlockSpec((1, GLA_DV, HK), lambda s, b, f, q_: (q_[s], 0, 0))]
    return pl.pallas_call(
        functools.partial(_gla_kernel, reverse=reverse, final=final),
        out_shape=out_shape,
        grid_spec=pltpu.PrefetchScalarGridSpec(
            num_scalar_prefetch=3, grid=(N_TILES,), in_specs=in_specs, out_specs=out_specs,
            scratch_shapes=[pltpu.VMEM((GLA_DV, HK), F32)]),
        compiler_params=pltpu.CompilerParams(dimension_semantics=("arbitrary",)),
        name="gla_bwd" if reverse else "gla_fwd",
    )(blk, first, seq, *args)


def _attn_kernel(q_ref, k_ref, vt_ref, wuvt_ref, o_ref, s_ref, p_ref, ot_ref):
    def scores(hd):
        s = _dot_nt(k_ref[0], q_ref[hd])
        s_ref[hd % 2] = s
        return jnp.max(s, axis=0, keepdims=True)

    m = scores(0)
    for hd in range(MLA_HEADS):
        m_next = scores(hd + 1) if hd + 1 < MLA_HEADS else None
        p = jnp.exp2(s_ref[hd % 2] - m)
        l = jnp.sum(p, axis=0, keepdims=True)
        p_ref[hd % 2] = p.astype(BF16)
        acc = _dot(vt_ref[0], p_ref[hd % 2])
        lat = (acc / l).astype(BF16)
        ot_ref[hd] = _dot(wuvt_ref[hd], lat)
        m = m_next
    o_ref[...] = ot_ref[...].reshape(MLA_HEADS * MLA_DV, TM).T.astype(o_ref.dtype)


def _attention(q, kcat, vt, wuvt, *, tile0, tiles_per_seq, name):
    n_seq, s_len, _ = kcat.shape
    return pl.pallas_call(
        _attn_kernel,
        out_shape=jax.ShapeDtypeStruct((n_seq * tiles_per_seq * TM, MLA_HEADS * MLA_DV), BF16),
        grid=(n_seq, tiles_per_seq),
        in_specs=[
            pl.BlockSpec((MLA_HEADS, TM, QCAT), lambda b, i: (0, tile0 + b * tiles_per_seq + i, 0)),
            pl.BlockSpec((1, s_len, QCAT), lambda b, i: (b, 0, 0)),
            pl.BlockSpec((1, MLA_KV_RANK, s_len), lambda b, i: (b, 0, 0)),
            pl.BlockSpec(wuvt.shape, lambda b, i: (0, 0, 0)),
        ],
        out_specs=pl.BlockSpec((TM, MLA_HEADS * MLA_DV), lambda b, i: (b * tiles_per_seq + i, 0)),
        scratch_shapes=[pltpu.VMEM((2, s_len, TM), F32), pltpu.VMEM((2, s_len, TM), BF16),
                        pltpu.VMEM((MLA_HEADS, MLA_DV, TM), F32)],
        compiler_params=pltpu.CompilerParams(dimension_semantics=("arbitrary", "arbitrary"),
                                             vmem_limit_bytes=VMEM_LIMIT),
        name=name,
    )(q, kcat, vt, wuvt)


def _merge_kernel(x_ref, mod_ref, oa_ref, obc_ref, obl_ref, za_ref, zb_ref, wpa_ref, wpb_ref, wo_ref, nw_ref,
                  wr1_ref, wr2_ref, x1_ref, h2x_ref, lg_ref):
    mod = mod_ref[0]
    ob = jnp.where(pl.program_id(0) < CTX_TILES, obc_ref[...], obl_ref[...])
    y = (za_ref[...].astype(F32) * _dot(oa_ref[...], wpa_ref[...])
         + zb_ref[...].astype(F32) * _dot(ob, wpb_ref[...]))
    out = _dot(y.astype(BF16), wo_ref[...])
    x1 = x_ref[...] + mod[2:3, :] * out
    x1_ref[...] = x1
    h2 = _rms(x1, nw_ref[...]) * (1.0 + mod[4:5, :]) + mod[3:4, :]
    h2_hi = h2.astype(BF16)
    h2_lo = (h2 - h2_hi.astype(F32)).astype(BF16)
    h2x_ref[:, 0:D_MODEL] = h2
    h2x_ref[:, D_MODEL:XW] = jnp.zeros((TM, XW - D_MODEL), F32)
    d1 = _dot(h2_hi, wr1_ref[...])
    d2 = _dot(h2_lo, wr2_ref[...])
    lg_ref[...] = (d1[:, 0:N_EXPERTS] + d1[:, N_EXPERTS:2 * N_EXPERTS]) + d2[:, 0:N_EXPERTS]


def _merge(x, mod, oa, ob_ctx, ob_lat, za, zb, wpa, wpb, wo, nw, wr1, wr2):
    full = lambda a: pl.BlockSpec(a.shape, lambda i: (0,) * a.ndim)
    tok = lambda w: pl.BlockSpec((TM, w), lambda i: (i, 0))
    hb = MLA_HEADS * MLA_DV
    ctx_spec = pl.BlockSpec((TM, hb), lambda i: (jnp.minimum(i, CTX_TILES - 1), 0))
    lat_spec = pl.BlockSpec((TM, hb), lambda i: (jnp.maximum(i - CTX_TILES, 0), 0))
    return pl.pallas_call(
        _merge_kernel,
        out_shape=[jax.ShapeDtypeStruct((N_TOK, D_MODEL), F32),
                   jax.ShapeDtypeStruct((N_TOK, XW), F32),
                   jax.ShapeDtypeStruct((N_TOK, N_EXPERTS), F32)],
        grid=(N_TILES,),
        in_specs=[tok(D_MODEL), pl.BlockSpec((1, N_MOD, D_MODEL), lambda i: (_tile_seq(i), 0, 0)),
                  tok(HV), ctx_spec, lat_spec, tok(D_MODEL), tok(D_MODEL),
                  full(wpa), full(wpb), full(wo), full(nw), full(wr1), full(wr2)],
        out_specs=[tok(D_MODEL), tok(XW), tok(N_EXPERTS)],
        compiler_params=pltpu.CompilerParams(dimension_semantics=("arbitrary",),
                                             vmem_limit_bytes=VMEM_LIMIT),
        name="merge",
    )(x, mod, oa, ob_ctx, ob_lat, za, zb, wpa, wpb, wo, nw, wr1, wr2)


def _route_kernel(lg_ref, b_ref, tri_ref, h2x_in_ref, h2x_ref, grp_ref, rank_ref, cnt_ref, carry_ref):
    del h2x_in_ref
    step = pl.program_id(0)

    @pl.when(step == 0)
    def _():
        carry_ref[...] = jnp.zeros_like(carry_ref)

    aff = _sigmoid(lg_ref[...])
    biased = aff + b_ref[...]
    row = lambda a, e: a[e:e + 1, :]
    best = None
    sel = None
    for g in range(N_GROUPS):
        b = [row(biased, g * EXPERTS_PER_GROUP + i) for i in range(EXPERTS_PER_GROUP)]
        score = None
        for i in range(EXPERTS_PER_GROUP):
            for j in range(i + 1, EXPERTS_PER_GROUP):
                pair = b[i] + b[j]
                score = pair if score is None else jnp.maximum(score, pair)
        if g == 0:
            best, sel = score, jnp.zeros_like(score, dtype=jnp.int32)
        else:
            better = score > best
            best = jnp.where(better, score, best)
            sel = jnp.where(better, g, sel)
    cb, ca = [], []
    for i in range(EXPERTS_PER_GROUP):
        vb = row(biased, i)
        va = row(aff, i)
        for g in range(1, N_GROUPS):
            vb = jnp.where(sel == g, row(biased, g * EXPERTS_PER_GROUP + i), vb)
            va = jnp.where(sel == g, row(aff, g * EXPERTS_PER_GROUP + i), va)
        cb.append(vb)
        ca.append(va)
    picked = []
    for i in range(EXPERTS_PER_GROUP):
        rank = jnp.zeros_like(sel)
        for j in range(EXPERTS_PER_GROUP):
            if j == i:
                continue
            ahead = (cb[j] >= cb[i]) if j < i else (cb[j] > cb[i])
            rank = rank + ahead.astype(jnp.int32)
        picked.append(rank < 2)
    denom = None
    for i in range(EXPERTS_PER_GROUP):
        term = jnp.where(picked[i], ca[i], 0.0)
        denom = term if denom is None else denom + term
    cw = [jnp.where(picked[i], ca[i] / denom, 0.0) for i in range(EXPERTS_PER_GROUP)]
    cw_t = jnp.concatenate(cw + [jnp.zeros((128 - EXPERTS_PER_GROUP, ROUTE_W), F32)], axis=0)
    h2x_ref[...] = cw_t.T
    grp_ref[...] = sel

    onehot = jnp.concatenate([(sel == g).astype(F32) for g in range(N_GROUPS)]
                             + [jnp.zeros((8 - N_GROUPS, ROUTE_W), F32)], axis=0)
    carry = carry_ref[...]
    for c in range(ROUTE_W // 256):
        lanes = slice(c * 256, (c + 1) * 256)
        oh = onehot[:, lanes]
        before = _dot(oh.astype(BF16), tri_ref[...]) + carry
        sel_c = sel[:, lanes]
        r = before[N_GROUPS - 1:N_GROUPS, :]
        for g in range(N_GROUPS - 2, -1, -1):
            r = jnp.where(sel_c == g, before[g:g + 1, :], r)
        rank_ref[:, lanes] = r.astype(jnp.int32)
        carry = carry + jnp.sum(oh, axis=1, keepdims=True)
    carry_ref[...] = carry
    cnt_ref[...] = carry[:, 0:128]


def _route(lg_t, b_router, tri, h2x):
    return pl.pallas_call(
        _route_kernel,
        out_shape=[jax.ShapeDtypeStruct((N_TOK, XW), F32),
                   jax.ShapeDtypeStruct((1, N_TOK), jnp.int32),
                   jax.ShapeDtypeStruct((1, N_TOK), jnp.int32),
                   jax.ShapeDtypeStruct((8, 128), F32)],
        grid=(N_TOK // ROUTE_W,),
        in_specs=[pl.BlockSpec((N_EXPERTS, ROUTE_W), lambda i: (0, i)),
                  pl.BlockSpec((N_EXPERTS, 1), lambda i: (0, 0)),
                  pl.BlockSpec((256, 256), lambda i: (0, 0)),
                  pl.BlockSpec(memory_space=pl.ANY)],
        out_specs=[pl.BlockSpec((ROUTE_W, XW - D_MODEL), lambda i: (i, D_MODEL // (XW - D_MODEL))),
                   pl.BlockSpec((1, ROUTE_W), lambda i: (0, i)),
                   pl.BlockSpec((1, ROUTE_W), lambda i: (0, i)),
                   pl.BlockSpec((8, 128), lambda i: (0, 0))],
        scratch_shapes=[pltpu.VMEM((8, 256), F32)],
        input_output_aliases={3: 0},
        compiler_params=pltpu.CompilerParams(dimension_semantics=("arbitrary",)),
        name="route",
    )(lg_t, b_router, tri, h2x)


def _gather_rows(idx_ref, src_ref, dst_ref, sem, n_rows):
    def row_copy(r, src_row):
        return pltpu.make_async_copy(src_ref.at[pl.ds(src_row, 1), :], dst_ref.at[pl.ds(r, 1), :], sem)

    def issue(r, c):
        row_copy(r, idx_ref[0, 0, r]).start()
        return c

    def drain(r, c):
        row_copy(r, 0).wait()
        return c

    lax.fori_loop(0, n_rows, issue, 0, unroll=8)
    lax.fori_loop(0, n_rows, drain, 0, unroll=8)


def _dispatch_kernel(idx_ref, src_ref, o_ref, sem):
    _gather_rows(idx_ref, src_ref, o_ref, sem, TE)


def _dispatch(tok_of_slot, h2x):
    return pl.pallas_call(
        _dispatch_kernel,
        out_shape=jax.ShapeDtypeStruct((N_SLOTS, XW), F32),
        grid=(N_SLOT_TILES,),
        in_specs=[pl.BlockSpec((1, 1, TE), lambda t: (t, 0, 0), memory_space=pltpu.SMEM),
                  pl.BlockSpec(memory_space=pl.ANY)],
        out_specs=pl.BlockSpec((TE, XW), lambda t: (t, 0)),
        scratch_shapes=[pltpu.SemaphoreType.DMA],
        compiler_params=pltpu.CompilerParams(dimension_semantics=("arbitrary",)),
        name="moe_dispatch",
    )(tok_of_slot.reshape(N_SLOT_TILES, 1, TE), h2x)


def _combine_kernel(idx_ref, ys_ref, x1_ref, mod_ref, fw_ref, o_ref, rows_ref, sem, *, final):
    _gather_rows(idx_ref, ys_ref, rows_ref, sem, TE)
    x2 = x1_ref[...] + mod_ref[0][5:6, :] * rows_ref[...]
    if final:
        x2 = _rms(x2, fw_ref[...])
    o_ref[...] = x2


def _combine(pos, ys, x1, mod, fw, *, final):
    per_seq = DEC_SEQ // TE
    ctx_tiles = N_CTX // TE
    seq_of = lambda i: jnp.where(i < ctx_tiles, 0, 1 + (i - ctx_tiles) // per_seq)
    return pl.pallas_call(
        functools.partial(_combine_kernel, final=final),
        out_shape=jax.ShapeDtypeStruct((N_TOK, D_MODEL), F32),
        grid=(N_TOK // TE,),
        in_specs=[pl.BlockSpec((1, 1, TE), lambda t: (t, 0, 0), memory_space=pltpu.SMEM),
                  pl.BlockSpec(memory_space=pl.ANY),
                  pl.BlockSpec((TE, D_MODEL), lambda t: (t, 0)),
                  pl.BlockSpec((1, N_MOD, D_MODEL), lambda t: (seq_of(t), 0, 0)),
                  pl.BlockSpec((1, D_MODEL), lambda t: (0, 0))],
        out_specs=pl.BlockSpec((TE, D_MODEL), lambda t: (t, 0)),
        scratch_shapes=[pltpu.VMEM((TE, D_MODEL), F32), pltpu.SemaphoreType.DMA],
        compiler_params=pltpu.CompilerParams(dimension_semantics=("arbitrary",)),
        name="moe_combine",
    )(pos.reshape(N_TOK // TE, 1, TE), ys, x1, mod, fw)


def _moe_kernel(tg_ref, used_ref, xs_ref, wg_ref, wu_ref, wd_ref, wsg_ref, wsu_ref, wsd_ref, o_ref):
    t = pl.program_id(0)

    @pl.when(t < used_ref[0])
    def _():
        x = xs_ref[:, 0:D_MODEL].astype(BF16)
        cw = xs_ref[:, D_MODEL:XW]
        acts = []
        for j in range(EXPERTS_PER_GROUP):
            gate = _dot(x, wg_ref[j])
            up = _dot(x, wu_ref[j])
            acts.append((gate * _sigmoid(gate) * up * cw[:, j:j + 1]).astype(BF16))
        y = _dot(jnp.concatenate(acts, axis=1), wd_ref[0])
        gate = _dot(x, wsg_ref[...])
        up = _dot(x, wsu_ref[...])
        o_ref[...] = y + _dot((gate * _sigmoid(gate) * up).astype(BF16), wsd_ref[...])

    @pl.when(t >= used_ref[0])
    def _():
        o_ref[...] = jnp.zeros_like(o_ref)


def _moe(tile_grp, n_used, xs, wg, wu, wd, wsg, wsu, wsd):
    full = lambda a: pl.BlockSpec(a.shape, lambda t, tg, nu: (0,) * a.ndim)
    grp_w = lambda a: pl.BlockSpec((EXPERTS_PER_GROUP,) + a.shape[1:], lambda t, tg, nu: (tg[t], 0, 0))
    return pl.pallas_call(
        _moe_kernel,
        out_shape=jax.ShapeDtypeStruct((N_SLOTS, D_MODEL), F32),
        grid_spec=pltpu.PrefetchScalarGridSpec(
            num_scalar_prefetch=2, grid=(N_SLOT_TILES,),
            in_specs=[pl.BlockSpec((TE, XW), lambda t, tg, nu: (t, 0)),
                      grp_w(wg), grp_w(wu),
                      pl.BlockSpec((1,) + wd.shape[1:], lambda t, tg, nu: (tg[t], 0, 0)),
                      full(wsg), full(wsu), full(wsd)],
            out_specs=pl.BlockSpec((TE, D_MODEL), lambda t, tg, nu: (t, 0))),
        compiler_params=pltpu.CompilerParams(dimension_semantics=("arbitrary",),
                                             vmem_limit_bytes=VMEM_LIMIT),
        name="moe_experts",
    )(tile_grp, n_used, xs, wg, wu, wd, wsg, wsu, wsd)


def _rope_tables():
    rows = DEC_SEQ // GRID_W
    r = jnp.repeat(jnp.arange(rows, dtype=F32), GRID_W)
    col = jnp.tile(jnp.arange(GRID_W, dtype=F32), rows)
    n_freq = MLA_ROPE // 4
    inv = ROPE_BASE ** (-jnp.arange(n_freq, dtype=F32) / n_freq)
    ang = jnp.stack([r[:, None] * inv, col[:, None] * inv], axis=1)
    expand = lambda t: jnp.broadcast_to(t[:, :, None, :], (DEC_SEQ, 2, 2, n_freq)).reshape(DEC_SEQ, MLA_ROPE)
    cos = jnp.concatenate([expand(jnp.cos(ang)), jnp.ones((SEQ, MLA_ROPE), F32)], axis=0)
    sin = jnp.concatenate([expand(jnp.sin(ang)), jnp.zeros((SEQ, MLA_ROPE), F32)], axis=0)
    return cos, sin


def _rot_cols(w):
    shp = w.shape
    w4 = w.reshape(shp[:-1] + (2, 2, MLA_ROPE // 4))
    return jnp.stack([-w4[..., 1, :], w4[..., 0, :]], axis=-2).reshape(shp)


def _gla_tables(reverse):
    blk, first, seq = [], [], []
    for s in range(N_SEQ):
        if s < BATCH:
            tiles = [s]
        else:
            base = CTX_TILES + (s - BATCH) * LAT_TILES_PER_SEQ
            tiles = list(range(base, base + LAT_TILES_PER_SEQ))
        if reverse:
            tiles = tiles[::-1]
        for n, t in enumerate(tiles):
            blk.append(t)
            first.append(1 if n == 0 else 0)
            seq.append(s)
    as_i32 = lambda v: jnp.asarray(v, dtype=jnp.int32)
    return as_i32(blk), as_i32(first), as_i32(seq)


def kernel(x_prompt, x_sample, cache_ckv, cache_krope, state_gla_fwd, state_gla_bwd, c, c_ctx, w_mod, b_mod, norm_mix, norm_ffn, w_in, w_alpha, b_alpha, gla_norm, q_norm, kv_norm, w_uq, w_uk, w_uv, w_pa, w_pb, w_o, w_router, b_router, w_exp_gate, w_exp_up, w_exp_down, w_sh_gate, w_sh_up, w_sh_down, final_norm):
    x = jnp.concatenate([x_prompt.reshape(N_CTX, D_MODEL), x_sample.reshape(N_LAT, D_MODEL)], axis=0)

    cond = jnp.concatenate([c_ctx[None, :], c, jnp.zeros((16 - 1 - DEC_BATCH, D_MODEL), F32)], axis=0)
    mod_all = _modulation(cond, w_mod.astype(BF16), b_mod[:, None, :])
    mod_all = mod_all.reshape(DEPTH, 16, N_MOD, D_MODEL)

    cos32, sin32 = _rope_tables()
    cq_tab = jnp.tile(cos32, (1, MLA_HEADS)) * Q_SCALE
    sq_tab = jnp.tile(sin32, (1, MLA_HEADS)) * Q_SCALE

    idx = jnp.arange(GLA_CHUNK)
    tri_f = (idx[None, :] <= idx[:, None]).astype(BF16)
    tri_b = (idx[None, :] >= idx[:, None]).astype(BF16)
    tab_f = _gla_tables(False)
    tab_b = _gla_tables(True)
    idx256 = jnp.arange(256)
    tri_route = (idx256[:, None] < idx256[None, :]).astype(BF16)

    wr_hi = w_router.astype(BF16)
    wr_lo = (w_router - wr_hi.astype(F32)).astype(BF16)
    zpad = jnp.zeros((D_MODEL, 128 - 2 * N_EXPERTS), BF16)
    wr1 = jnp.concatenate([wr_hi, wr_lo, zpad], axis=1)
    wr2 = jnp.concatenate([wr_hi, jnp.zeros_like(wr_lo), zpad], axis=1)

    ckvs, krs, sfs, sbs = [], [], [], []
    for l in range(DEPTH):
        mod = mod_all[l]
        (w_gq, w_gk, w_gv, w_gg, w_gaf, w_gab, w_cq, w_ckv, w_kr, w_za, w_zb) = jnp.split(
            w_in[l], (256, 512, 1024, 1536, 1552, 1568, 1824, 1952, 1984, 3008), axis=1)
        w_small = jnp.concatenate([w_gaf, w_gab, w_kr, _rot_cols(w_kr), jnp.zeros((D_MODEL, 32), F32)], axis=1)
        w1 = jnp.concatenate([w_gq, w_gk, w_gv, w_gg, w_cq, w_ckv, w_small, w_za, w_zb], axis=1).astype(BF16)
        wa = jnp.zeros((128, 2 * HK), F32)
        wa = wa.at[0:GLA_GATE_RANK, 0:HK].set(w_alpha[l, 0])
        wa = wa.at[GLA_GATE_RANK:2 * GLA_GATE_RANK, HK:2 * HK].set(w_alpha[l, 1]).astype(BF16)
        ba = b_alpha[l].reshape(1, 2 * HK)
        wuq3 = w_uq[l].reshape(MLA_Q_RANK, MLA_HEADS, MLA_NOPE + MLA_ROPE)
        wuqn = wuq3[:, :, :MLA_NOPE].reshape(MLA_Q_RANK, MLA_HEADS * MLA_NOPE).astype(BF16)
        wuq_rope = wuq3[:, :, MLA_NOPE:]
        wuqr = wuq_rope.reshape(MLA_Q_RANK, MLA_HEADS * MLA_ROPE).astype(BF16)
        wuqrr = _rot_cols(wuq_rope).reshape(MLA_Q_RANK, MLA_HEADS * MLA_ROPE).astype(BF16)
        wukt = w_uk[l].reshape(MLA_KV_RANK, MLA_HEADS, MLA_NOPE).transpose(1, 2, 0).astype(BF16)
        wuvt = w_uv[l].reshape(MLA_KV_RANK, MLA_HEADS, MLA_DV).transpose(1, 2, 0).astype(BF16)

        gq, gk, gv, gg, la, qcat, ckvn, kr, za, zb = _inproj(
            x, mod, norm_mix[l][None, :], w1, wa, ba, q_norm[l][None, :], kv_norm[l][None, :],
            wuqn, wuqr, wuqrr, wukt, cq_tab, sq_tab, cos32, sin32)

        to_t = lambda s: s.transpose(0, 3, 1, 2).reshape(s.shape[0], GLA_DV, HK)
        zeros_ctx = jnp.zeros((BATCH, GLA_DV, HK), F32)
        s0f = jnp.concatenate([zeros_ctx, to_t(state_gla_fwd[:, l])], axis=0)
        s0b = jnp.concatenate([zeros_ctx, to_t(state_gla_bwd[:, l])], axis=0)
        o_f, sf = _gla_direction(tab_f, gq, gk, gv, gg, la, s0f, tri_f, None, None, reverse=False)
        o_a, sb = _gla_direction(tab_b, gq, gk, gv, gg, la, s0b, tri_b, o_f, gla_norm[l][None, :], reverse=True)

        ckv_ctx = ckvn[:N_CTX].reshape(BATCH, SEQ, MLA_KV_RANK)
        kr_ctx = kr[:N_CTX].reshape(BATCH, SEQ, MLA_ROPE)
        ckv_lat = jnp.concatenate([cache_ckv[:, l], ckvn[N_CTX:].reshape(DEC_BATCH, DEC_SEQ, MLA_KV_RANK)], axis=1)
        kr_lat = jnp.concatenate([cache_krope[:, l], kr[N_CTX:].reshape(DEC_BATCH, DEC_SEQ, MLA_ROPE)], axis=1)

        def kv_operands(ckv_all, kr_all):
            kcat = jnp.concatenate([ckv_all, kr_all], axis=-1).astype(BF16)
            return kcat, ckv_all.astype(BF16).transpose(0, 2, 1)

        kc, vt = kv_operands(ckv_ctx, kr_ctx)
        ob_ctx = _attention(qcat, kc, vt, wuvt, tile0=0, tiles_per_seq=1, name="mla_ctx")
        kc, vt = kv_operands(ckv_lat, kr_lat)
        ob_lat = _attention(qcat, kc, vt, wuvt, tile0=CTX_TILES, tiles_per_seq=LAT_TILES_PER_SEQ, name="mla_lat")

        x1, h2x, lg = _merge(x, mod, o_a, ob_ctx, ob_lat, za, zb, w_pa[l].astype(BF16), w_pb[l].astype(BF16),
                             w_o[l].astype(BF16), norm_ffn[l][None, :], wr1, wr2)
        h2x, grp, rank, counts = _route(lg.T, b_router[:, None], tri_route, h2x)

        cnt = counts[:N_GROUPS, 0].astype(jnp.int32)
        padded = (cnt + (TE - 1)) // TE * TE
        g_end = jnp.cumsum(padded)
        pos = (g_end - padded)[grp[0]] + rank[0]
        tok_of_slot = jnp.zeros((N_SLOTS,), jnp.int32).at[pos].set(
            jnp.arange(N_TOK, dtype=jnp.int32), unique_indices=True)
        tile_start = jnp.arange(N_SLOT_TILES, dtype=jnp.int32) * TE
        tile_grp = jnp.minimum(jnp.sum(tile_start[:, None] >= g_end[None, :], axis=1), N_GROUPS - 1).astype(jnp.int32)
        n_used = (g_end[N_GROUPS - 1:] // TE).astype(jnp.int32)

        xs = _dispatch(tok_of_slot, h2x)
        ys = _moe(tile_grp, n_used, xs, w_exp_gate[l].astype(BF16), w_exp_up[l].astype(BF16),
                  w_exp_down[l].astype(BF16).reshape(N_GROUPS, EXPERTS_PER_GROUP * EXPERT_FF, D_MODEL),
                  w_sh_gate[l].astype(BF16), w_sh_up[l].astype(BF16), w_sh_down[l].astype(BF16))
        x = _combine(pos, ys, x1, mod, final_norm[None, :], final=(l == DEPTH - 1))

        ckvs.append(ckv_ctx)
        krs.append(kr_ctx)
        from_t = lambda s: s[:BATCH].reshape(BATCH, GLA_DV, GLA_HEADS, GLA_DK).transpose(0, 2, 3, 1)
        sfs.append(from_t(sf))
        sbs.append(from_t(sb))

    y_prompt = x[:N_CTX].reshape(BATCH, SEQ, D_MODEL)
    y_sample = x[N_CTX:].reshape(DEC_BATCH, DEC_SEQ, D_MODEL)
    return (y_prompt, y_sample, jnp.stack(ckvs, axis=1), jnp.stack(krs, axis=1),
            jnp.stack(sfs, axis=1), jnp.stack(sbs, axis=1))
```

```python
import functools

import jax
import jax.numpy as jnp
from jax import lax
from jax.experimental import pallas as pl
from jax.experimental.pallas import tpu as pltpu

F32 = jnp.float32
BF16 = jnp.bfloat16

D_MODEL = 1024
BATCH = 16
SEQ = 256
DEPTH = 2
DEC_BATCH = 8
DEC_SEQ = 4096
PAST_LEN = 512
GRID_W = 64
GLA_HEADS = 4
GLA_DK = 64
GLA_DV = 128
GLA_GATE_RANK = 16
GLA_TAU = 16.0
GLA_CHUNK = 64
MLA_HEADS = 8
MLA_Q_RANK = 256
MLA_KV_RANK = 128
MLA_NOPE = 64
MLA_ROPE = 32
MLA_DV = 64
ROPE_BASE = 10000.0
N_EXPERTS = 16
N_GROUPS = 4
EXPERTS_PER_GROUP = 4
EXPERT_FF = 512
SHARED_FF = 512
N_MOD = 6
EPS = 1e-6

N_CTX = BATCH * SEQ
N_LAT = DEC_BATCH * DEC_SEQ
N_TOK = N_CTX + N_LAT
N_SEQ = BATCH + DEC_BATCH
TM = 256
N_TILES = N_TOK // TM
CTX_TILES = N_CTX // TM
LAT_TILES_PER_SEQ = DEC_SEQ // TM
HK = GLA_HEADS * GLA_DK
HV = GLA_HEADS * GLA_DV
QCAT = MLA_KV_RANK + MLA_ROPE
Q_SCALE =(MLA_NOPE + MLA_ROPE) ** -0.5 * 1.4426950408889634
TE = 512
N_SLOT_TILES = N_TOK // TE + N_GROUPS
N_SLOTS = N_SLOT_TILES * TE
XW = D_MODEL + 128
ROUTE_W = 2048
VMEM_LIMIT = 56 * 1024 * 1024

C_GQ, C_GK, C_GV, C_GG, C_CQ, C_CKV, C_SMALL, C_ZA, C_ZB, C_END = (
    0, 256, 512, 1024, 1536, 1792, 1920, 2048, 3072, 4096)


def _sigmoid(x):
    return 1.0 / (1.0 + jnp.exp(-x))


def _rms(x, w):
    return x * lax.rsqrt(jnp.mean(x * x, axis=-1, keepdims=True) + EPS) * w


def _dot(a, b):
    return jnp.dot(a, b, preferred_element_type=F32)


def _dot_nt(a, b):
    return lax.dot_general(a, b, (((1,), (1,)), ((), ())), preferred_element_type=F32)


def _dot_tn(a, b):
    return lax.dot_general(a, b, (((0,), (0,)), ((), ())), preferred_element_type=F32)


def _tile_seq(i):
    return jnp.where(i < CTX_TILES, 0, 1 + (i - CTX_TILES) // LAT_TILES_PER_SEQ)


def _tile_pos(i):
    return jnp.where(i < CTX_TILES, LAT_TILES_PER_SEQ, (i - CTX_TILES) % LAT_TILES_PER_SEQ)


def _mod_kernel(c_ref, w_ref, b_ref, o_ref):
    c = c_ref[...]
    sc = (c * _sigmoid(c)).astype(BF16)
    o_ref[...] = _dot(sc, w_ref[...]) + b_ref[...]


def _modulation(cond, w_mod, b_mod):
    nb = 1024
    return pl.pallas_call(
        _mod_kernel,
        out_shape=jax.ShapeDtypeStruct((DEPTH, 16, N_MOD * D_MODEL), F32),
        grid=(DEPTH, N_MOD * D_MODEL // nb),
        in_specs=[
            pl.BlockSpec((16, D_MODEL), lambda l, j: (0, 0)),
            pl.BlockSpec((None, D_MODEL, nb), lambda l, j: (l, 0, j)),
            pl.BlockSpec((None, 1, nb), lambda l, j: (l, 0, j)),
        ],
        out_specs=pl.BlockSpec((None, 16, nb), lambda l, j: (l, 0, j)),
        name="modulation",
    )(cond, w_mod, b_mod)


def _inproj_kernel(x_ref, mod_ref, nw_ref, w1_ref, wa_ref, ba_ref, qnw_ref, kvnw_ref,
                   wuqn_ref, wuqr_ref, wuqrr_ref, wukt_ref, cq_ref, sq_ref, ck_ref, sk_ref,
                   gq_ref, gk_ref, gv_ref, gg_ref, la_ref, q_ref, ckv_ref, kr_ref, za_ref, zb_ref):
    x = x_ref[...]
    mod = mod_ref[0]
    h = (_rms(x, nw_ref[...]) * (1.0 + mod[1:2, :]) + mod[0:1, :]).astype(BF16)

    def proj(lo, hi):
        return _dot(h, w1_ref[:, lo:hi])

    gq_ref[...] = (proj(C_GQ, C_GK) * (GLA_DK ** -0.5)).astype(BF16)
    gk_ref[...] = proj(C_GK, C_GV).astype(BF16)
    gv_ref[...] = proj(C_GV, C_GG).astype(BF16)
    gg_ref[...] = proj(C_GG, C_CQ).astype(BF16)
    za_ref[...] = _sigmoid(proj(C_ZA, C_ZB)).astype(BF16)
    zb_ref[...] = _sigmoid(proj(C_ZB, C_END)).astype(BF16)

    small = proj(C_SMALL, C_ZA)
    lin = _dot(small.astype(BF16), wa_ref[...]) + ba_ref[...]
    la_ref[...] = (jnp.minimum(lin, 0.0) - jnp.log(1.0 + jnp.exp(-jnp.abs(lin)))) * (1.0 / GLA_TAU)
    kr_ref[...] = small[:, 32:64] * ck_ref[...] + small[:, 64:96] * sk_ref[...]

    ckv_ref[...] = _rms(proj(C_CKV, C_SMALL), kvnw_ref[...])

    cqn = _rms(proj(C_CQ, C_CKV), qnw_ref[...]).astype(BF16)
    qn = _dot(cqn, wuqn_ref[...]).astype(BF16)
    qr = (_dot(cqn, wuqr_ref[...]) * cq_ref[...]
          + _dot(cqn, wuqrr_ref[...]) * sq_ref[...])
    for hd in range(MLA_HEADS):
        qa = _dot(qn[:, hd * MLA_NOPE:(hd + 1) * MLA_NOPE], wukt_ref[hd]) * Q_SCALE
        q_ref[hd, :, 0:MLA_KV_RANK] = qa.astype(BF16)
        q_ref[hd, :, MLA_KV_RANK:QCAT] = qr[:, hd * MLA_ROPE:(hd + 1) * MLA_ROPE].astype(BF16)


def _inproj(x, mod, nw, w1, wa, ba, qnw, kvnw, wuqn, wuqr, wuqrr, wukt, cq, sq, ck, sk):
    full = lambda a: pl.BlockSpec(a.shape, lambda i: (0,) * a.ndim)
    tok = lambda w: pl.BlockSpec((TM, w), lambda i: (i, 0))
    pos = lambda w: pl.BlockSpec((TM, w), lambda i: (_tile_pos(i), 0))
    out_shape = [
        jax.ShapeDtypeStruct((N_TOK, HK), BF16),
        jax.ShapeDtypeStruct((N_TOK, HK), BF16),
        jax.ShapeDtypeStruct((N_TOK, HV), BF16),
        jax.ShapeDtypeStruct((N_TOK, HV), BF16),
        jax.ShapeDtypeStruct((N_TOK, 2 * HK), F32),
        jax.ShapeDtypeStruct((MLA_HEADS, N_TOK, QCAT), BF16),
        jax.ShapeDtypeStruct((N_TOK, MLA_KV_RANK), F32),
        jax.ShapeDtypeStruct((N_TOK, MLA_ROPE), F32),
        jax.ShapeDtypeStruct((N_TOK, D_MODEL), BF16),
        jax.ShapeDtypeStruct((N_TOK, D_MODEL), BF16),
    ]
    out_specs = [tok(HK), tok(HK), tok(HV), tok(HV), tok(2 * HK),
                 pl.BlockSpec((MLA_HEADS, TM, QCAT), lambda i: (0, i, 0)),
                 tok(MLA_KV_RANK), tok(MLA_ROPE), tok(D_MODEL), tok(D_MODEL)]
    in_specs = [tok(D_MODEL),
                pl.BlockSpec((1, N_MOD, D_MODEL), lambda i: (_tile_seq(i), 0, 0)),
                full(nw), full(w1), full(wa), full(ba), full(qnw), full(kvnw),
                full(wuqn), full(wuqr), full(wuqrr), full(wukt),
                pos(HK), pos(HK), pos(MLA_ROPE), pos(MLA_ROPE)]
    return pl.pallas_call(
        _inproj_kernel, out_shape=out_shape, grid=(N_TILES,),
        in_specs=in_specs, out_specs=out_specs,
        compiler_params=pltpu.CompilerParams(dimension_semantics=("arbitrary",),
                                             vmem_limit_bytes=VMEM_LIMIT),
        name="inproj",
    )(x, mod, nw, w1, wa, ba, qnw, kvnw, wuqn, wuqr, wuqrr, wukt, cq, sq, ck, sk)


def _gla_tile(q_ref, k_ref, v_ref, la_ref, st_ref, o_ref, tri, ones_blk, *, reverse):
    n_chunks = TM // GLA_CHUNK
    la = la_ref[...]
    la_hi = la.astype(BF16)
    la_lo = (la - la_hi.astype(F32)).astype(BF16)
    cum = _dot(tri, la_hi) + _dot(tri, la_lo)
    tot = _dot(ones_blk, la_hi) + _dot(ones_blk, la_lo)
    q = q_ref[...].astype(F32)
    k = k_ref[...].astype(F32)
    qd = q * jnp.exp(cum)
    kin = (k * jnp.exp(-cum)).astype(BF16)
    kout = k * jnp.exp(tot - cum)
    decay = jnp.exp(tot)
    lane = lax.broadcasted_iota(jnp.int32, (TM, HK), 1)
    head_of_lane = lane // GLA_DK
    qd_h = [jnp.where(head_of_lane == hd, qd, 0.0).astype(BF16) for hd in range(GLA_HEADS)]
    kout_h = [jnp.where(head_of_lane == hd, kout, 0.0).astype(BF16) for hd in range(GLA_HEADS)]
    row = lax.broadcasted_iota(jnp.int32, (TM, TM), 0)
    col = lax.broadcasted_iota(jnp.int32, (TM, TM), 1)
    keep = (row // GLA_CHUNK == col // GLA_CHUNK) & ((col >= row) if reverse else (col <= row))
    head_vs = [slice(hd * GLA_DV, (hd + 1) * GLA_DV) for hd in range(GLA_HEADS)]

    att_all = _dot_nt(jnp.concatenate(qd_h, axis=0), kin)
    o_intra = [_dot(jnp.where(keep, att_all[hd * TM:(hd + 1) * TM], 0.0).astype(BF16), v_ref[:, head_vs[hd]])
               for hd in range(GLA_HEADS)]

    st = st_ref[...]
    for ci in range(n_chunks):
        c = (n_chunks - 1 - ci) if reverse else ci
        rows = slice(c * GLA_CHUNK, (c + 1) * GLA_CHUNK)
        q_stack = jnp.concatenate([qd_h[hd][rows] for hd in range(GLA_HEADS)], axis=0)
        o_inter = _dot_nt(q_stack, st.astype(BF16))
        for hd in range(GLA_HEADS):
            o = o_intra[hd][rows] + o_inter[hd * GLA_CHUNK:(hd + 1) * GLA_CHUNK]
            o_ref[rows, head_vs[hd]] = o.astype(o_ref.dtype)
        v_stack = jnp.concatenate([v_ref[rows, head_vs[hd]] for hd in range(GLA_HEADS)], axis=0)
        k_stack = jnp.concatenate([kout_h[hd][rows] for hd in range(GLA_HEADS)], axis=0)
        st = st * decay[c * GLA_CHUNK:c * GLA_CHUNK + 1, :] + _dot_tn(v_stack, k_stack)
    st_ref[...] = st


def _gla_kernel(blkf_ref, blkb_ref, first_ref, seq_ref,
                qf_ref, kf_ref, vf_ref, laf_ref, qb_ref, kb_ref, vb_ref, lab_ref, s0f_ref, s0b_ref,
                trif_ref, trib_ref, ones_ref,
                of_ref, ob_ref, sff_ref, sfb_ref, stf_ref, stb_ref):
    step = pl.program_id(0)

    @pl.when(first_ref[step] == 1)
    def _():
        stf_ref[...] = s0f_ref[0]
        stb_ref[...] = s0b_ref[0]

    _gla_tile(qf_ref, kf_ref, vf_ref, laf_ref, stf_ref, of_ref, trif_ref[...], ones_ref[...], reverse=False)
    _gla_tile(qb_ref, kb_ref, vb_ref, lab_ref, stb_ref, ob_ref, trib_ref[...], ones_ref[...], reverse=True)
    sff_ref[0] = stf_ref[...]
    sfb_ref[0] = stb_ref[...]


def _gla(tables, q, k, v, la, s0f, s0b, trif, trib, ones_blk):
    blkf, blkb, first, seq = tables
    fwd = lambda w, j=0: pl.BlockSpec((TM, w), lambda s, bf, bb, f, q_: (bf[s], j))
    bwd = lambda w, j=0: pl.BlockSpec((TM, w), lambda s, bf, bb, f, q_: (bb[s], j))
    per_seq = pl.BlockSpec((1, GLA_DV, HK), lambda s, bf, bb, f, q_: (q_[s], 0, 0))
    const = pl.BlockSpec((TM, TM), lambda s, bf, bb, f, q_: (0, 0))
    return pl.pallas_call(
        _gla_kernel,
        out_shape=[jax.ShapeDtypeStruct((N_TOK, HV), BF16), jax.ShapeDtypeStruct((N_TOK, HV), BF16),
                   jax.ShapeDtypeStruct((N_SEQ, GLA_DV, HK), F32), jax.ShapeDtypeStruct((N_SEQ, GLA_DV, HK), F32)],
        grid_spec=pltpu.PrefetchScalarGridSpec(
            num_scalar_prefetch=4, grid=(N_TILES,),
            in_specs=[fwd(HK), fwd(HK), fwd(HV), fwd(HK, 0), bwd(HK), bwd(HK), bwd(HV), bwd(HK, 1),
                      per_seq, per_seq, const, const, const],
            out_specs=[fwd(HV), bwd(HV), per_seq, per_seq],
            scratch_shapes=[pltpu.VMEM((GLA_DV, HK), F32), pltpu.VMEM((GLA_DV, HK), F32)]),
        compiler_params=pltpu.CompilerParams(dimension_semantics=("arbitrary",)),
        name="gla",
    )(blkf, blkb, first, seq, q, k, v, la, q, k, v, la, s0f, s0b, trif, trib, ones_blk)


def _attn_kernel(q_ref, k_ref, vt_ref, wuvt_ref, o_ref, s_ref, p_ref, ot_ref):
    def scores(hd):
        s = _dot_nt(k_ref[0], q_ref[hd])
        s_ref[hd % 2] = s
        return jnp.max(s, axis=0, keepdims=True)

    m = scores(0)
    for hd in range(MLA_HEADS):
        m_next = scores(hd + 1) if hd + 1 < MLA_HEADS else None
        p = jnp.exp2(s_ref[hd % 2] - m)
        l = jnp.sum(p, axis=0, keepdims=True)
        p_ref[hd % 2] = p.astype(BF16)
        acc = _dot(vt_ref[0], p_ref[hd % 2])
        lat = (acc / l).astype(BF16)
        ot_ref[hd] = _dot(wuvt_ref[hd], lat)
        m = m_next
    o_ref[...] = ot_ref[...].reshape(MLA_HEADS * MLA_DV, TM).T.astype(o_ref.dtype)


def _attention(q, kcat, vt, wuvt, *, tile0, tiles_per_seq, name):
    n_seq, s_len, _ = kcat.shape
    return pl.pallas_call(
        _attn_kernel,
        out_shape=jax.ShapeDtypeStruct((n_seq * tiles_per_seq * TM, MLA_HEADS * MLA_DV), BF16),
        grid=(n_seq, tiles_per_seq),
        in_specs=[
            pl.BlockSpec((MLA_HEADS, TM, QCAT), lambda b, i: (0, tile0 + b * tiles_per_seq + i, 0)),
            pl.BlockSpec((1, s_len, QCAT), lambda b, i: (b, 0, 0)),
            pl.BlockSpec((1, MLA_KV_RANK, s_len), lambda b, i: (b, 0, 0)),
            pl.BlockSpec(wuvt.shape, lambda b, i: (0, 0, 0)),
        ],
        out_specs=pl.BlockSpec((TM, MLA_HEADS * MLA_DV), lambda b, i: (b * tiles_per_seq + i, 0)),
        scratch_shapes=[pltpu.VMEM((2, s_len, TM), F32), pltpu.VMEM((2, s_len, TM), BF16),
                        pltpu.VMEM((MLA_HEADS, MLA_DV, TM), F32)],
        compiler_params=pltpu.CompilerParams(dimension_semantics=("arbitrary", "arbitrary"),
                                             vmem_limit_bytes=VMEM_LIMIT),
        name=name,
    )(q, kcat, vt, wuvt)


def _merge_kernel(x_ref, mod_ref, of_ref, ob_ref, gg_ref, gnw_ref, obc_ref, obl_ref, za_ref, zb_ref,
                  wpa_ref, wpb_ref, wo_ref, nw_ref, wr1_ref, wr2_ref, x1_ref, h2x_ref, lg_ref):
    mod = mod_ref[0]
    o_sum = of_ref[...].astype(F32) + ob_ref[...].astype(F32)
    gate = gg_ref[...].astype(F32)
    gate = gate * _sigmoid(gate)
    oa = jnp.concatenate(
        [(_rms(o_sum[:, hd * GLA_DV:(hd + 1) * GLA_DV], gnw_ref[...])
          * gate[:, hd * GLA_DV:(hd + 1) * GLA_DV]).astype(BF16) for hd in range(GLA_HEADS)], axis=1)
    ob = jnp.where(pl.program_id(0) < CTX_TILES, obc_ref[...], obl_ref[...])
    y = (za_ref[...].astype(F32) * _dot(oa, wpa_ref[...])
         + zb_ref[...].astype(F32) * _dot(ob, wpb_ref[...]))
    out = _dot(y.astype(BF16), wo_ref[...])
    x1 = x_ref[...] + mod[2:3, :] * out
    x1_ref[...] = x1
    h2 = _rms(x1, nw_ref[...]) * (1.0 + mod[4:5, :]) + mod[3:4, :]
    h2_hi = h2.astype(BF16)
    h2_lo = (h2 - h2_hi.astype(F32)).astype(BF16)
    h2x_ref[:, 0:D_MODEL] = h2
    h2x_ref[:, D_MODEL:XW] = jnp.zeros((TM, XW - D_MODEL), F32)
    d1 = _dot(h2_hi, wr1_ref[...])
    d2 = _dot(h2_lo, wr2_ref[...])
    lg_ref[...] = (d1[:, 0:N_EXPERTS] + d1[:, N_EXPERTS:2 * N_EXPERTS]) + d2[:, 0:N_EXPERTS]


def _merge(x, mod, o_f, o_b, gg, gnw, ob_ctx, ob_lat, za, zb, wpa, wpb, wo, nw, wr1, wr2):
    full = lambda a: pl.BlockSpec(a.shape, lambda i: (0,) * a.ndim)
    tok = lambda w: pl.BlockSpec((TM, w), lambda i: (i, 0))
    hb = MLA_HEADS * MLA_DV
    ctx_spec = pl.BlockSpec((TM, hb), lambda i: (jnp.minimum(i, CTX_TILES - 1), 0))
    lat_spec = pl.BlockSpec((TM, hb), lambda i: (jnp.maximum(i - CTX_TILES, 0), 0))
    return pl.pallas_call(
        _merge_kernel,
        out_shape=[jax.ShapeDtypeStruct((N_TOK, D_MODEL), F32),
                   jax.ShapeDtypeStruct((N_TOK, XW), F32),
                   jax.ShapeDtypeStruct((N_TOK, N_EXPERTS), F32)],
        grid=(N_TILES,),
        in_specs=[tok(D_MODEL), pl.BlockSpec((1, N_MOD, D_MODEL), lambda i: (_tile_seq(i), 0, 0)),
                  tok(HV), tok(HV), tok(HV), full(gnw), ctx_spec, lat_spec, tok(D_MODEL), tok(D_MODEL),
                  full(wpa), full(wpb), full(wo), full(nw), full(wr1), full(wr2)],
        out_specs=[tok(D_MODEL), tok(XW), tok(N_EXPERTS)],
        compiler_params=pltpu.CompilerParams(dimension_semantics=("arbitrary",),
                                             vmem_limit_bytes=VMEM_LIMIT),
        name="merge",
    )(x, mod, o_f, o_b, gg, gnw, ob_ctx, ob_lat, za, zb, wpa, wpb, wo, nw, wr1, wr2)


def _route_kernel(lg_ref, b_ref, tri_ref, h2x_in_ref, h2x_ref, grp_ref, rank_ref, cnt_ref, carry_ref):
    del h2x_in_ref
    step = pl.program_id(0)

    @pl.when(step == 0)
    def _():
        carry_ref[...] = jnp.zeros_like(carry_ref)

    aff = _sigmoid(lg_ref[...])
    biased = aff + b_ref[...]
    row = lambda a, e: a[e:e + 1, :]
    best = None
    sel = None
    for g in range(N_GROUPS):
        b = [row(biased, g * EXPERTS_PER_GROUP + i) for i in range(EXPERTS_PER_GROUP)]
        score = None
        for i in range(EXPERTS_PER_GROUP):
            for j in range(i + 1, EXPERTS_PER_GROUP):
                pair = b[i] + b[j]
                score = pair if score is None else jnp.maximum(score, pair)
        if g == 0:
            best, sel = score, jnp.zeros_like(score, dtype=jnp.int32)
        else:
            better = score > best
            best = jnp.where(better, score, best)
            sel = jnp.where(better, g, sel)
    cb, ca = [], []
    for i in range(EXPERTS_PER_GROUP):
        vb = row(biased, i)
        va = row(aff, i)
        for g in range(1, N_GROUPS):
            vb = jnp.where(sel == g, row(biased, g * EXPERTS_PER_GROUP + i), vb)
            va = jnp.where(sel == g, row(aff, g * EXPERTS_PER_GROUP + i), va)
        cb.append(vb)
        ca.append(va)
    picked = []
    for i in range(EXPERTS_PER_GROUP):
        rank = jnp.zeros_like(sel)
        for j in range(EXPERTS_PER_GROUP):
            if j == i:
                continue
            ahead = (cb[j] >= cb[i]) if j < i else (cb[j] > cb[i])
            rank = rank + ahead.astype(jnp.int32)
        picked.append(rank < 2)
    denom = None
    for i in range(EXPERTS_PER_GROUP):
        term = jnp.where(picked[i], ca[i], 0.0)
        denom = term if denom is None else denom + term
    cw = [jnp.where(picked[i], ca[i] / denom, 0.0) for i in range(EXPERTS_PER_GROUP)]
    cw_t = jnp.concatenate(cw + [jnp.zeros((128 - EXPERTS_PER_GROUP, ROUTE_W), F32)], axis=0)
    h2x_ref[...] = cw_t.T
    grp_ref[...] = sel

    onehot = jnp.concatenate([(sel == g).astype(F32) for g in range(N_GROUPS)]
                             + [jnp.zeros((8 - N_GROUPS, ROUTE_W), F32)], axis=0)
    carry = carry_ref[...]
    for c in range(ROUTE_W // 256):
        lanes = slice(c * 256, (c + 1) * 256)
        oh = onehot[:, lanes]
        before = _dot(oh.astype(BF16), tri_ref[...]) + carry
        sel_c = sel[:, lanes]
        r = before[N_GROUPS - 1:N_GROUPS, :]
        for g in range(N_GROUPS - 2, -1, -1):
            r = jnp.where(sel_c == g, before[g:g + 1, :], r)
        rank_ref[:, lanes] = r.astype(jnp.int32)
        carry = carry + jnp.sum(oh, axis=1, keepdims=True)
    carry_ref[...] = carry
    cnt_ref[...] = carry[:, 0:128]


def _route(lg_t, b_router, tri, h2x):
    return pl.pallas_call(
        _route_kernel,
        out_shape=[jax.ShapeDtypeStruct((N_TOK, XW), F32),
                   jax.ShapeDtypeStruct((1, N_TOK), jnp.int32),
                   jax.ShapeDtypeStruct((1, N_TOK), jnp.int32),
                   jax.ShapeDtypeStruct((8, 128), F32)],
        grid=(N_TOK // ROUTE_W,),
        in_specs=[pl.BlockSpec((N_EXPERTS, ROUTE_W), lambda i: (0, i)),
                  pl.BlockSpec((N_EXPERTS, 1), lambda i: (0, 0)),
                  pl.BlockSpec((256, 256), lambda i: (0, 0)),
                  pl.BlockSpec(memory_space=pl.ANY)],
        out_specs=[pl.BlockSpec((ROUTE_W, XW - D_MODEL), lambda i: (i, D_MODEL // (XW - D_MODEL))),
                   pl.BlockSpec((1, ROUTE_W), lambda i: (0, i)),
                   pl.BlockSpec((1, ROUTE_W), lambda i: (0, i)),
                   pl.BlockSpec((8, 128), lambda i: (0, 0))],
        scratch_shapes=[pltpu.VMEM((8, 256), F32)],
        input_output_aliases={3: 0},
        compiler_params=pltpu.CompilerParams(dimension_semantics=("arbitrary",)),
        name="route",
    )(lg_t, b_router, tri, h2x)


def _gather_rows(idx_ref, src_ref, dst_ref, sem, n_rows):
    def row_copy(r, src_row):
        return pltpu.make_async_copy(src_ref.at[pl.ds(src_row, 1), :], dst_ref.at[pl.ds(r, 1), :], sem)

    def issue(r, c):
        row_copy(r, idx_ref[0, 0, r]).start()
        return c

    def drain(r, c):
        row_copy(r, 0).wait()
        return c

    lax.fori_loop(0, n_rows, issue, 0, unroll=8)
    lax.fori_loop(0, n_rows, drain, 0, unroll=8)


def _dispatch_kernel(idx_ref, src_ref, o_ref, sem):
    _gather_rows(idx_ref, src_ref, o_ref, sem, TE)


def _dispatch(tok_of_slot, h2x):
    return pl.pallas_call(
        _dispatch_kernel,
        out_shape=jax.ShapeDtypeStruct((N_SLOTS, XW), F32),
        grid=(N_SLOT_TILES,),
        in_specs=[pl.BlockSpec((1, 1, TE), lambda t: (t, 0, 0), memory_space=pltpu.SMEM),
                  pl.BlockSpec(memory_space=pl.ANY)],
        out_specs=pl.BlockSpec((TE, XW), lambda t: (t, 0)),
        scratch_shapes=[pltpu.SemaphoreType.DMA],
        compiler_params=pltpu.CompilerParams(dimension_semantics=("arbitrary",)),
        name="moe_dispatch",
    )(tok_of_slot.reshape(N_SLOT_TILES, 1, TE), h2x)


def _combine_kernel(idx_ref, ys_ref, x1_ref, mod_ref, fw_ref, *rest, final):
    if final:
        yc_ref, yl_ref, rows_ref, sem = rest
    else:
        o_ref, rows_ref, sem = rest
    _gather_rows(idx_ref, ys_ref, rows_ref, sem, TE)
    x2 = x1_ref[...] + mod_ref[0][5:6, :] * rows_ref[...]
    if not final:
        o_ref[...] = x2
        return
    y = _rms(x2, fw_ref[...])
    is_ctx = pl.program_id(0) < N_CTX // TE

    @pl.when(is_ctx)
    def _():
        yc_ref[...] = y

    @pl.when(jnp.logical_not(is_ctx))
    def _():
        yl_ref[...] = y


def _combine(pos, ys, x1, mod, fw, *, final):
    per_seq = DEC_SEQ // TE
    ctx_tiles = N_CTX // TE
    seq_of = lambda i: jnp.where(i < ctx_tiles, 0, 1 + (i - ctx_tiles) // per_seq)
    if final:
        out_shape = [jax.ShapeDtypeStruct((N_CTX, D_MODEL), F32), jax.ShapeDtypeStruct((N_LAT, D_MODEL), F32)]
        out_specs = [pl.BlockSpec((TE, D_MODEL), lambda t: (jnp.minimum(t, ctx_tiles - 1), 0)),
                     pl.BlockSpec((TE, D_MODEL), lambda t: (jnp.maximum(t - ctx_tiles, 0), 0))]
    else:
        out_shape = jax.ShapeDtypeStruct((N_TOK, D_MODEL), F32)
        out_specs = pl.BlockSpec((TE, D_MODEL), lambda t: (t, 0))
    return pl.pallas_call(
        functools.partial(_combine_kernel, final=final),
        out_shape=out_shape,
        grid=(N_TOK // TE,),
        in_specs=[pl.BlockSpec((1, 1, TE), lambda t: (t, 0, 0), memory_space=pltpu.SMEM),
                  pl.BlockSpec(memory_space=pl.ANY),
                  pl.BlockSpec((TE, D_MODEL), lambda t: (t, 0)),
                  pl.BlockSpec((1, N_MOD, D_MODEL), lambda t: (seq_of(t), 0, 0)),
                  pl.BlockSpec((1, D_MODEL), lambda t: (0, 0))],
        out_specs=out_specs,
        scratch_shapes=[pltpu.VMEM((TE, D_MODEL), F32), pltpu.SemaphoreType.DMA],
        compiler_params=pltpu.CompilerParams(dimension_semantics=("arbitrary",)),
        name="moe_combine",
    )(pos.reshape(N_TOK // TE, 1, TE), ys, x1, mod, fw)


def _moe_kernel(tg_ref, used_ref, xs_ref, wg_ref, wu_ref, wd_ref, wsg_ref, wsu_ref, wsd_ref, o_ref):
    t = pl.program_id(0)

    @pl.when(t < used_ref[0])
    def _():
        x = xs_ref[:, 0:D_MODEL].astype(BF16)
        cw = xs_ref[:, D_MODEL:XW]
        acts = []
        for j in range(EXPERTS_PER_GROUP):
            gate = _dot(x, wg_ref[j])
            up = _dot(x, wu_ref[j])
            acts.append((gate * _sigmoid(gate) * up * cw[:, j:j + 1]).astype(BF16))
        y = _dot(jnp.concatenate(acts, axis=1), wd_ref[0])
        gate = _dot(x, wsg_ref[...])
        up = _dot(x, wsu_ref[...])
        o_ref[...] = y + _dot((gate * _sigmoid(gate) * up).astype(BF16), wsd_ref[...])

    @pl.when(t >= used_ref[0])
    def _():
        o_ref[...] = jnp.zeros_like(o_ref)


def _moe(tile_grp, n_used, xs, wg, wu, wd, wsg, wsu, wsd):
    full = lambda a: pl.BlockSpec(a.shape, lambda t, tg, nu: (0,) * a.ndim)
    grp_w = lambda a: pl.BlockSpec((EXPERTS_PER_GROUP,) + a.shape[1:], lambda t, tg, nu: (tg[t], 0, 0))
    return pl.pallas_call(
        _moe_kernel,
        out_shape=jax.ShapeDtypeStruct((N_SLOTS, D_MODEL), F32),
        grid_spec=pltpu.PrefetchScalarGridSpec(
            num_scalar_prefetch=2, grid=(N_SLOT_TILES,),
            in_specs=[pl.BlockSpec((TE, XW), lambda t, tg, nu: (t, 0)),
                      grp_w(wg), grp_w(wu),
                      pl.BlockSpec((1,) + wd.shape[1:], lambda t, tg, nu: (tg[t], 0, 0)),
                      full(wsg), full(wsu), full(wsd)],
            out_specs=pl.BlockSpec((TE, D_MODEL), lambda t, tg, nu: (t, 0))),
        compiler_params=pltpu.CompilerParams(dimension_semantics=("arbitrary",),
                                             vmem_limit_bytes=VMEM_LIMIT),
        name="moe_experts",
    )(tile_grp, n_used, xs, wg, wu, wd, wsg, wsu, wsd)


def _rope_tables():
    rows = DEC_SEQ // GRID_W
    r = jnp.repeat(jnp.arange(rows, dtype=F32), GRID_W)
    col = jnp.tile(jnp.arange(GRID_W, dtype=F32), rows)
    n_freq = MLA_ROPE // 4
    inv = ROPE_BASE ** (-jnp.arange(n_freq, dtype=F32) / n_freq)
    ang = jnp.stack([r[:, None] * inv, col[:, None] * inv], axis=1)
    expand = lambda t: jnp.broadcast_to(t[:, :, None, :], (DEC_SEQ, 2, 2, n_freq)).reshape(DEC_SEQ, MLA_ROPE)
    cos = jnp.concatenate([expand(jnp.cos(ang)), jnp.ones((SEQ, MLA_ROPE), F32)], axis=0)
    sin = jnp.concatenate([expand(jnp.sin(ang)), jnp.zeros((SEQ, MLA_ROPE), F32)], axis=0)
    return cos, sin


def _rot_cols(w):
    shp = w.shape
    w4 = w.reshape(shp[:-1] + (2, 2, MLA_ROPE // 4))
    return jnp.stack([-w4[..., 1, :], w4[..., 0, :]], axis=-2).reshape(shp)


def _gla_tables():
    blk_f, blk_b, first, seq = [], [], [], []
    for s in range(N_SEQ):
        if s < BATCH:
            tiles = [s]
        else:
            base = CTX_TILES + (s - BATCH) * LAT_TILES_PER_SEQ
            tiles = list(range(base, base + LAT_TILES_PER_SEQ))
        for n, t in enumerate(tiles):
            blk_f.append(t)
            blk_b.append(tiles[len(tiles) - 1 - n])
            first.append(1 if n == 0 else 0)
            seq.append(s)
    as_i32 = lambda v: jnp.asarray(v, dtype=jnp.int32)
    return as_i32(blk_f), as_i32(blk_b), as_i32(first), as_i32(seq)


def kernel(x_prompt, x_sample, cache_ckv, cache_krope, state_gla_fwd, state_gla_bwd, c, c_ctx, w_mod, b_mod, norm_mix, norm_ffn, w_in, w_alpha, b_alpha, gla_norm, q_norm, kv_norm, w_uq, w_uk, w_uv, w_pa, w_pb, w_o, w_router, b_router, w_exp_gate, w_exp_up, w_exp_down, w_sh_gate, w_sh_up, w_sh_down, final_norm):
    x = jnp.concatenate([x_prompt.reshape(N_CTX, D_MODEL), x_sample.reshape(N_LAT, D_MODEL)], axis=0)

    cond = jnp.concatenate([c_ctx[None, :], c, jnp.zeros((16 - 1 - DEC_BATCH, D_MODEL), F32)], axis=0)
    mod_all = _modulation(cond, w_mod.astype(BF16), b_mod[:, None, :])
    mod_all = mod_all.reshape(DEPTH, 16, N_MOD, D_MODEL)

    cos32, sin32 = _rope_tables()
    cq_tab = jnp.tile(cos32, (1, MLA_HEADS)) * Q_SCALE
    sq_tab = jnp.tile(sin32, (1, MLA_HEADS)) * Q_SCALE

    idx256 = jnp.arange(TM)
    same_chunk = (idx256[:, None] // GLA_CHUNK) == (idx256[None, :] // GLA_CHUNK)
    tri_f = (same_chunk & (idx256[None, :] <= idx256[:, None])).astype(BF16)
    tri_b = (same_chunk & (idx256[None, :] >= idx256[:, None])).astype(BF16)
    ones_blk = same_chunk.astype(BF16)
    gla_tab = _gla_tables()
    tri_route = (idx256[:, None] < idx256[None, :]).astype(BF16)

    wr_hi = w_router.astype(BF16)
    wr_lo = (w_router - wr_hi.astype(F32)).astype(BF16)
    zpad = jnp.zeros((D_MODEL, 128 - 2 * N_EXPERTS), BF16)
    wr1 = jnp.concatenate([wr_hi, wr_lo, zpad], axis=1)
    wr2 = jnp.concatenate([wr_hi, jnp.zeros_like(wr_lo), zpad], axis=1)

    ckvs, krs, sfs, sbs = [], [], [], []
    for l in range(DEPTH):
        mod = mod_all[l]
        (w_gq, w_gk, w_gv, w_gg, w_gaf, w_gab, w_cq, w_ckv, w_kr, w_za, w_zb) = jnp.split(
            w_in[l], (256, 512, 1024, 1536, 1552, 1568, 1824, 1952, 1984, 3008), axis=1)
        w_small = jnp.concatenate([w_gaf, w_gab, w_kr, _rot_cols(w_kr), jnp.zeros((D_MODEL, 32), F32)], axis=1)
        w1 = jnp.concatenate([w_gq, w_gk, w_gv, w_gg, w_cq, w_ckv, w_small, w_za, w_zb], axis=1).astype(BF16)
        wa = jnp.zeros((128, 2 * HK), F32)
        wa = wa.at[0:GLA_GATE_RANK, 0:HK].set(w_alpha[l, 0])
        wa = wa.at[GLA_GATE_RANK:2 * GLA_GATE_RANK, HK:2 * HK].set(w_alpha[l, 1]).astype(BF16)
        ba = b_alpha[l].reshape(1, 2 * HK)
        wuq3 = w_uq[l].reshape(MLA_Q_RANK, MLA_HEADS, MLA_NOPE + MLA_ROPE)
        wuqn = wuq3[:, :, :MLA_NOPE].reshape(MLA_Q_RANK, MLA_HEADS * MLA_NOPE).astype(BF16)
        wuq_rope = wuq3[:, :, MLA_NOPE:]
        wuqr = wuq_rope.reshape(MLA_Q_RANK, MLA_HEADS * MLA_ROPE).astype(BF16)
        wuqrr = _rot_cols(wuq_rope).reshape(MLA_Q_RANK, MLA_HEADS * MLA_ROPE).astype(BF16)
        wukt = w_uk[l].reshape(MLA_KV_RANK, MLA_HEADS, MLA_NOPE).transpose(1, 2, 0).astype(BF16)
        wuvt = w_uv[l].reshape(MLA_KV_RANK, MLA_HEADS, MLA_DV).transpose(1, 2, 0).astype(BF16)

        gq, gk, gv, gg, la, qcat, ckvn, kr, za, zb = _inproj(
            x, mod, norm_mix[l][None, :], w1, wa, ba, q_norm[l][None, :], kv_norm[l][None, :],
            wuqn, wuqr, wuqrr, wukt, cq_tab, sq_tab, cos32, sin32)

        to_t = lambda s: s.transpose(0, 3, 1, 2).reshape(s.shape[0], GLA_DV, HK)
        zeros_ctx = jnp.zeros((BATCH, GLA_DV, HK), F32)
        s0f = jnp.concatenate([zeros_ctx, to_t(state_gla_fwd[:, l])], axis=0)
        s0b = jnp.concatenate([zeros_ctx, to_t(state_gla_bwd[:, l])], axis=0)
        o_f, o_r, sf, sb = _gla(gla_tab, gq, gk, gv, la, s0f, s0b, tri_f, tri_b, ones_blk)

        ckv_ctx = ckvn[:N_CTX].reshape(BATCH, SEQ, MLA_KV_RANK)
        kr_ctx = kr[:N_CTX].reshape(BATCH, SEQ, MLA_ROPE)
        ckv_lat = jnp.concatenate([cache_ckv[:, l], ckvn[N_CTX:].reshape(DEC_BATCH, DEC_SEQ, MLA_KV_RANK)], axis=1)
        kr_lat = jnp.concatenate([cache_krope[:, l], kr[N_CTX:].reshape(DEC_BATCH, DEC_SEQ, MLA_ROPE)], axis=1)

        def kv_operands(ckv_all, kr_all):
            kcat = jnp.concatenate([ckv_all, kr_all], axis=-1).astype(BF16)
            return kcat, ckv_all.astype(BF16).transpose(0, 2, 1)

        kc, vt = kv_operands(ckv_ctx, kr_ctx)
        ob_ctx = _attention(qcat, kc, vt, wuvt, tile0=0, tiles_per_seq=1, name="mla_ctx")
        kc, vt = kv_operands(ckv_lat, kr_lat)
        ob_lat = _attention(qcat, kc, vt, wuvt, tile0=CTX_TILES, tiles_per_seq=LAT_TILES_PER_SEQ, name="mla_lat")

        x1, h2x, lg = _merge(x, mod, o_f, o_r, gg, gla_norm[l][None, :], ob_ctx, ob_lat, za, zb, w_pa[l].astype(BF16), w_pb[l].astype(BF16),
                             w_o[l].astype(BF16), norm_ffn[l][None, :], wr1, wr2)
        h2x, grp, rank, counts = _route(lg.T, b_router[:, None], tri_route, h2x)

        cnt = counts[:N_GROUPS, 0].astype(jnp.int32)
        padded = (cnt + (TE - 1)) // TE * TE
        g_end = jnp.cumsum(padded)
        pos = (g_end - padded)[grp[0]] + rank[0]
        tok_of_slot = jnp.zeros((N_SLOTS,), jnp.int32).at[pos].set(
            jnp.arange(N_TOK, dtype=jnp.int32), unique_indices=True)
        tile_start = jnp.arange(N_SLOT_TILES, dtype=jnp.int32) * TE
        tile_grp = jnp.minimum(jnp.sum(tile_start[:, None] >= g_end[None, :], axis=1), N_GROUPS - 1).astype(jnp.int32)
        n_used = (g_end[N_GROUPS - 1:] // TE).astype(jnp.int32)

        xs = _dispatch(tok_of_slot, h2x)
        ys = _moe(tile_grp, n_used, xs, w_exp_gate[l].astype(BF16), w_exp_up[l].astype(BF16),
                  w_exp_down[l].astype(BF16).reshape(N_GROUPS, EXPERTS_PER_GROUP * EXPERT_FF, D_MODEL),
                  w_sh_gate[l].astype(BF16), w_sh_up[l].astype(BF16), w_sh_down[l].astype(BF16))
        x = _combine(pos, ys, x1, mod, final_norm[None, :], final=(l == DEPTH - 1))

        ckvs.append(ckv_ctx)
        krs.append(kr_ctx)
        from_t = lambda s: s[:BATCH].reshape(BATCH, GLA_DV, GLA_HEADS, GLA_DK).transpose(0, 2, 3, 1)
        sfs.append(from_t(sf))
        sbs.append(from_t(sb))

    y_prompt = x[0].reshape(BATCH, SEQ, D_MODEL)
    y_sample = x[1].reshape(DEC_BATCH, DEC_SEQ, D_MODEL)
    return (y_prompt, y_sample, jnp.stack(ckvs, axis=1), jnp.stack(krs, axis=1),
            jnp.stack(sfs, axis=1), jnp.stack(sbs, axis=1))
```

```python
import functools

import jax
import jax.numpy as jnp
from jax import lax
from jax.experimental import pallas as pl
from jax.experimental.pallas import tpu as pltpu

F32 = jnp.float32
BF16 = jnp.bfloat16

D_MODEL = 1024
BATCH = 16
SEQ = 256
DEPTH = 2
DEC_BATCH = 8
DEC_SEQ = 4096
PAST_LEN = 512
GRID_W = 64
GLA_HEADS = 4
GLA_DK = 64
GLA_DV = 128
GLA_GATE_RANK = 16
GLA_TAU = 16.0
GLA_CHUNK = 64
MLA_HEADS = 8
MLA_Q_RANK = 256
MLA_KV_RANK = 128
MLA_NOPE = 64
MLA_ROPE = 32
MLA_DV = 64
ROPE_BASE = 10000.0
N_EXPERTS = 16
N_GROUPS = 4
EXPERTS_PER_GROUP = 4
EXPERT_FF = 512
SHARED_FF = 512
N_MOD = 6
EPS = 1e-6

N_CTX = BATCH * SEQ
N_LAT = DEC_BATCH * DEC_SEQ
N_TOK = N_CTX + N_LAT
N_SEQ = BATCH + DEC_BATCH
TM = 256
N_TILES = N_TOK // TM
CTX_TILES = N_CTX // TM
LAT_TILES_PER_SEQ = DEC_SEQ // TM
TB = 512
N_TILES_B = N_TOK // TB
CTX_TILES_B = N_CTX // TB
LAT_TILES_B_PER_SEQ = DEC_SEQ // TB
HK = GLA_HEADS * GLA_DK
HV = GLA_HEADS * GLA_DV
QCAT = MLA_KV_RANK + MLA_ROPE
Q_SCALE =(MLA_NOPE + MLA_ROPE) ** -0.5 * 1.4426950408889634
TE = 512
N_SLOT_TILES = N_TOK // TE + N_GROUPS
N_SLOTS = N_SLOT_TILES * TE
XW = D_MODEL + 128
ROUTE_W = 2048
VMEM_LIMIT = 56 * 1024 * 1024

C_GQ, C_GK, C_GV, C_GG, C_CQ, C_CKV, C_SMALL, C_ZA, C_ZB, C_END = (
    0, 256, 512, 1024, 1536, 1792, 1920, 2048, 3072, 4096)


def _sigmoid(x):
    return 1.0 / (1.0 + jnp.exp(-x))


def _rms(x, w):
    return x * lax.rsqrt(jnp.mean(x * x, axis=-1, keepdims=True) + EPS) * w


def _dot(a, b):
    return jnp.dot(a, b, preferred_element_type=F32)


def _dot_nt(a, b):
    return lax.dot_general(a, b, (((1,), (1,)), ((), ())), preferred_element_type=F32)


def _dot_tn(a, b):
    return lax.dot_general(a, b, (((0,), (0,)), ((), ())), preferred_element_type=F32)


def _tile_seq(i):
    return jnp.where(i < CTX_TILES_B, 0, 1 + (i - CTX_TILES_B) // LAT_TILES_B_PER_SEQ)


def _tile_pos(i):
    return jnp.where(i < CTX_TILES_B, LAT_TILES_B_PER_SEQ, (i - CTX_TILES_B) % LAT_TILES_B_PER_SEQ)


def _x_operands(x):
    if isinstance(x, tuple):
        xa, xb, off = x[0], x[1], 0
    else:
        xa, xb, off = x, x, CTX_TILES_B
    spec_a = pl.BlockSpec((TB, D_MODEL), lambda i: (jnp.minimum(i, CTX_TILES_B - 1), 0))
    spec_b = pl.BlockSpec((TB, D_MODEL), lambda i: (jnp.maximum(i - CTX_TILES_B, 0) + off, 0))
    return xa, xb, spec_a, spec_b


def _x_tile(xa_ref, xb_ref):
    return jnp.where(pl.program_id(0) < CTX_TILES_B, xa_ref[...], xb_ref[...])


def _mod_kernel(c_ref, w_ref, b_ref, o_ref):
    c = c_ref[...]
    sc = (c * _sigmoid(c)).astype(BF16)
    o_ref[...] = _dot(sc, w_ref[...]) + b_ref[...]


def _modulation(cond, w_mod, b_mod):
    nb = 1024
    return pl.pallas_call(
        _mod_kernel,
        out_shape=jax.ShapeDtypeStruct((DEPTH, 16, N_MOD * D_MODEL), F32),
        grid=(DEPTH, N_MOD * D_MODEL // nb),
        in_specs=[
            pl.BlockSpec((16, D_MODEL), lambda l, j: (0, 0)),
            pl.BlockSpec((None, D_MODEL, nb), lambda l, j: (l, 0, j)),
            pl.BlockSpec((None, 1, nb), lambda l, j: (l, 0, j)),
        ],
        out_specs=pl.BlockSpec((None, 16, nb), lambda l, j: (l, 0, j)),
        name="modulation",
    )(cond, w_mod, b_mod)


def _inproj_kernel(xa_ref, xb_ref, mod_ref, nw_ref, w1_ref, wa_ref, ba_ref, qnw_ref, kvnw_ref,
                   wuqn_ref, wuqr_ref, wuqrr_ref, wukt_ref, cq_ref, sq_ref, ck_ref, sk_ref,
                   gq_ref, gk_ref, gv_ref, gg_ref, la_ref, q_ref, ckv_ref, kr_ref, za_ref, zb_ref):
    x = _x_tile(xa_ref, xb_ref)
    mod = mod_ref[0]
    h = (_rms(x, nw_ref[...]) * (1.0 + mod[1:2, :]) + mod[0:1, :]).astype(BF16)

    def proj(lo, hi):
        return _dot(h, w1_ref[:, lo:hi])

    gq_ref[...] = (proj(C_GQ, C_GK) * (GLA_DK ** -0.5)).astype(BF16)
    gk_ref[...] = proj(C_GK, C_GV).astype(BF16)
    gv_ref[...] = proj(C_GV, C_GG).astype(BF16)
    gg_ref[...] = proj(C_GG, C_CQ).astype(BF16)
    za_ref[...] = _sigmoid(proj(C_ZA, C_ZB)).astype(BF16)
    zb_ref[...] = _sigmoid(proj(C_ZB, C_END)).astype(BF16)

    small = proj(C_SMALL, C_ZA)
    lin = _dot(small.astype(BF16), wa_ref[...]) + ba_ref[...]
    la_ref[...] = (jnp.minimum(lin, 0.0) - jnp.log(1.0 + jnp.exp(-jnp.abs(lin)))) * (1.0 / GLA_TAU)
    kr_ref[...] = small[:, 32:64] * ck_ref[...] + small[:, 64:96] * sk_ref[...]

    ckv_ref[...] = _rms(proj(C_CKV, C_SMALL), kvnw_ref[...])

    cqn = _rms(proj(C_CQ, C_CKV), qnw_ref[...]).astype(BF16)
    qn = _dot(cqn, wuqn_ref[...]).astype(BF16)
    qr = (_dot(cqn, wuqr_ref[...]) * cq_ref[...]
          + _dot(cqn, wuqrr_ref[...]) * sq_ref[...])
    for hd in range(MLA_HEADS):
        qa = _dot(qn[:, hd * MLA_NOPE:(hd + 1) * MLA_NOPE], wukt_ref[hd]) * Q_SCALE
        q_ref[hd, :, 0:MLA_KV_RANK] = qa.astype(BF16)
        q_ref[hd, :, MLA_KV_RANK:QCAT] = qr[:, hd * MLA_ROPE:(hd + 1) * MLA_ROPE].astype(BF16)


def _inproj(x, mod, nw, w1, wa, ba, qnw, kvnw, wuqn, wuqr, wuqrr, wukt, cq, sq, ck, sk):
    full = lambda a: pl.BlockSpec(a.shape, lambda i: (0,) * a.ndim)
    tok = lambda w: pl.BlockSpec((TB, w), lambda i: (i, 0))
    pos = lambda w: pl.BlockSpec((TB, w), lambda i: (_tile_pos(i), 0))
    xa, xb, spec_a, spec_b = _x_operands(x)
    out_shape = [
        jax.ShapeDtypeStruct((N_TOK, HK), BF16),
        jax.ShapeDtypeStruct((N_TOK, HK), BF16),
        jax.ShapeDtypeStruct((N_TOK, HV), BF16),
        jax.ShapeDtypeStruct((N_TOK, HV), BF16),
        jax.ShapeDtypeStruct((N_TOK, 2 * HK), F32),
        jax.ShapeDtypeStruct((MLA_HEADS, N_TOK, QCAT), BF16),
        jax.ShapeDtypeStruct((N_TOK, MLA_KV_RANK), F32),
        jax.ShapeDtypeStruct((N_TOK, MLA_ROPE), F32),
        jax.ShapeDtypeStruct((N_TOK, D_MODEL), BF16),
        jax.ShapeDtypeStruct((N_TOK, D_MODEL), BF16),
    ]
    out_specs = [tok(HK), tok(HK), tok(HV), tok(HV), tok(2 * HK),
                 pl.BlockSpec((MLA_HEADS, TB, QCAT), lambda i: (0, i, 0)),
                 tok(MLA_KV_RANK), tok(MLA_ROPE), tok(D_MODEL), tok(D_MODEL)]
    in_specs = [spec_a, spec_b,
                pl.BlockSpec((1, N_MOD, D_MODEL), lambda i: (_tile_seq(i), 0, 0)),
                full(nw), full(w1), full(wa), full(ba), full(qnw), full(kvnw),
                full(wuqn), full(wuqr), full(wuqrr), full(wukt),
                pos(HK), pos(HK), pos(MLA_ROPE), pos(MLA_ROPE)]
    return pl.pallas_call(
        _inproj_kernel, out_shape=out_shape, grid=(N_TILES_B,),
        in_specs=in_specs, out_specs=out_specs,
        compiler_params=pltpu.CompilerParams(dimension_semantics=("arbitrary",),
                                             vmem_limit_bytes=VMEM_LIMIT),
        name="inproj",
    )(xa, xb, mod, nw, w1, wa, ba, qnw, kvnw, wuqn, wuqr, wuqrr, wukt, cq, sq, ck, sk)


def _gla_tile(q_ref, k_ref, v_ref, la_ref, st_ref, o_ref, tri, ones_blk, *, reverse):
    n_chunks = TM // GLA_CHUNK
    la = la_ref[...]
    la_hi = la.astype(BF16)
    la_lo = (la - la_hi.astype(F32)).astype(BF16)
    cum = _dot(tri, la_hi) + _dot(tri, la_lo)
    tot = _dot(ones_blk, la_hi) + _dot(ones_blk, la_lo)
    q = q_ref[...].astype(F32)
    k = k_ref[...].astype(F32)
    qd = q * jnp.exp(cum)
    kin = (k * jnp.exp(-cum)).astype(BF16)
    kout = k * jnp.exp(tot - cum)
    decay = jnp.exp(tot)
    lane = lax.broadcasted_iota(jnp.int32, (TM, HK), 1)
    head_of_lane = lane // GLA_DK
    qd_h = [jnp.where(head_of_lane == hd, qd, 0.0).astype(BF16) for hd in range(GLA_HEADS)]
    kout_h = [jnp.where(head_of_lane == hd, kout, 0.0).astype(BF16) for hd in range(GLA_HEADS)]
    row = lax.broadcasted_iota(jnp.int32, (TM, TM), 0)
    col = lax.broadcasted_iota(jnp.int32, (TM, TM), 1)
    keep = (row // GLA_CHUNK == col // GLA_CHUNK) & ((col >= row) if reverse else (col <= row))
    head_vs = [slice(hd * GLA_DV, (hd + 1) * GLA_DV) for hd in range(GLA_HEADS)]

    att_all = _dot_nt(jnp.concatenate(qd_h, axis=0), kin)
    o_intra = [_dot(jnp.where(keep, att_all[hd * TM:(hd + 1) * TM], 0.0).astype(BF16), v_ref[:, head_vs[hd]])
               for hd in range(GLA_HEADS)]

    st = st_ref[...]
    for ci in range(n_chunks):
        c = (n_chunks - 1 - ci) if reverse else ci
        rows = slice(c * GLA_CHUNK, (c + 1) * GLA_CHUNK)
        q_stack = jnp.concatenate([qd_h[hd][rows] for hd in range(GLA_HEADS)], axis=0)
        o_inter = _dot_nt(q_stack, st.astype(BF16))
        for hd in range(GLA_HEADS):
            o = o_intra[hd][rows] + o_inter[hd * GLA_CHUNK:(hd + 1) * GLA_CHUNK]
            o_ref[rows, head_vs[hd]] = o.astype(o_ref.dtype)
        v_stack = jnp.concatenate([v_ref[rows, head_vs[hd]] for hd in range(GLA_HEADS)], axis=0)
        k_stack = jnp.concatenate([kout_h[hd][rows] for hd in range(GLA_HEADS)], axis=0)
        st = st * decay[c * GLA_CHUNK:c * GLA_CHUNK + 1, :] + _dot_tn(v_stack, k_stack)
    st_ref[...] = st


def _gla_kernel(blkf_ref, blkb_ref, first_ref, seq_ref,
                qf_ref, kf_ref, vf_ref, laf_ref, qb_ref, kb_ref, vb_ref, lab_ref, s0f_ref, s0b_ref,
                trif_ref, trib_ref, ones_ref,
                of_ref, ob_ref, sff_ref, sfb_ref, stf_ref, stb_ref):
    step = pl.program_id(0)

    @pl.when(first_ref[step] == 1)
    def _():
        stf_ref[...] = s0f_ref[0]
        stb_ref[...] = s0b_ref[0]

    _gla_tile(qf_ref, kf_ref, vf_ref, laf_ref, stf_ref, of_ref, trif_ref[...], ones_ref[...], reverse=False)
    _gla_tile(qb_ref, kb_ref, vb_ref, lab_ref, stb_ref, ob_ref, trib_ref[...], ones_ref[...], reverse=True)
    sff_ref[0] = stf_ref[...]
    sfb_ref[0] = stb_ref[...]


def _gla(tables, q, k, v, la, s0f, s0b, trif, trib, ones_blk):
    blkf, blkb, first, seq = tables
    fwd = lambda w, j=0: pl.BlockSpec((TM, w), lambda s, bf, bb, f, q_: (bf[s], j))
    bwd = lambda w, j=0: pl.BlockSpec((TM, w), lambda s, bf, bb, f, q_: (bb[s], j))
    per_seq = pl.BlockSpec((1, GLA_DV, HK), lambda s, bf, bb, f, q_: (q_[s], 0, 0))
    const = pl.BlockSpec((TM, TM), lambda s, bf, bb, f, q_: (0, 0))
    return pl.pallas_call(
        _gla_kernel,
        out_shape=[jax.ShapeDtypeStruct((N_TOK, HV), BF16), jax.ShapeDtypeStruct((N_TOK, HV), BF16),
                   jax.ShapeDtypeStruct((N_SEQ, GLA_DV, HK), F32), jax.ShapeDtypeStruct((N_SEQ, GLA_DV, HK), F32)],
        grid_spec=pltpu.PrefetchScalarGridSpec(
            num_scalar_prefetch=4, grid=(N_TILES,),
            in_specs=[fwd(HK), fwd(HK), fwd(HV), fwd(HK, 0), bwd(HK), bwd(HK), bwd(HV), bwd(HK, 1),
                      per_seq, per_seq, const, const, const],
            out_specs=[fwd(HV), bwd(HV), per_seq, per_seq],
            scratch_shapes=[pltpu.VMEM((GLA_DV, HK), F32), pltpu.VMEM((GLA_DV, HK), F32)]),
        compiler_params=pltpu.CompilerParams(dimension_semantics=("arbitrary",)),
        name="gla",
    )(blkf, blkb, first, seq, q, k, v, la, q, k, v, la, s0f, s0b, trif, trib, ones_blk)


def _attn_kernel(q_ref, k_ref, vt_ref, wuvt_ref, o_ref, s_ref, p_ref, ot_ref):
    def scores(hd):
        s = _dot_nt(k_ref[0], q_ref[hd])
        s_ref[hd % 2] = s
        return jnp.max(s, axis=0, keepdims=True)

    m = scores(0)
    for hd in range(MLA_HEADS):
        m_next = scores(hd + 1) if hd + 1 < MLA_HEADS else None
        p = jnp.exp2(s_ref[hd % 2] - m)
        l = jnp.sum(p, axis=0, keepdims=True)
        p_ref[hd % 2] = p.astype(BF16)
        acc = _dot(vt_ref[0], p_ref[hd % 2])
        lat = (acc / l).astype(BF16)
        ot_ref[hd] = _dot(wuvt_ref[hd], lat)
        m = m_next
    o_ref[...] = ot_ref[...].reshape(MLA_HEADS * MLA_DV, TM).T.astype(o_ref.dtype)


def _attention(q, kcat, vt, wuvt, *, tile0, tiles_per_seq, name):
    n_seq, s_len, _ = kcat.shape
    return pl.pallas_call(
        _attn_kernel,
        out_shape=jax.ShapeDtypeStruct((n_seq * tiles_per_seq * TM, MLA_HEADS * MLA_DV), BF16),
        grid=(n_seq, tiles_per_seq),
        in_specs=[
            pl.BlockSpec((MLA_HEADS, TM, QCAT), lambda b, i: (0, tile0 + b * tiles_per_seq + i, 0)),
            pl.BlockSpec((1, s_len, QCAT), lambda b, i: (b, 0, 0)),
            pl.BlockSpec((1, MLA_KV_RANK, s_len), lambda b, i: (b, 0, 0)),
            pl.BlockSpec(wuvt.shape, lambda b, i: (0, 0, 0)),
        ],
        out_specs=pl.BlockSpec((TM, MLA_HEADS * MLA_DV), lambda b, i: (b * tiles_per_seq + i, 0)),
        scratch_shapes=[pltpu.VMEM((2, s_len, TM), F32), pltpu.VMEM((2, s_len, TM), BF16),
                        pltpu.VMEM((MLA_HEADS, MLA_DV, TM), F32)],
        compiler_params=pltpu.CompilerParams(dimension_semantics=("arbitrary", "arbitrary"),
                                             vmem_limit_bytes=VMEM_LIMIT),
        name=name,
    )(q, kcat, vt, wuvt)


def _merge_kernel(xa_ref, xb_ref, mod_ref, of_ref, ob_ref, gg_ref, gnw_ref, obc_ref, obl_ref, za_ref, zb_ref,
                  wpa_ref, wpb_ref, wo_ref, nw_ref, wr1_ref, wr2_ref, x1_ref, h2x_ref, lg_ref):
    mod = mod_ref[0]
    o_sum = of_ref[...].astype(F32) + ob_ref[...].astype(F32)
    gate = gg_ref[...].astype(F32)
    gate = gate * _sigmoid(gate)
    oa = jnp.concatenate(
        [(_rms(o_sum[:, hd * GLA_DV:(hd + 1) * GLA_DV], gnw_ref[...])
          * gate[:, hd * GLA_DV:(hd + 1) * GLA_DV]).astype(BF16) for hd in range(GLA_HEADS)], axis=1)
    ob = jnp.where(pl.program_id(0) < CTX_TILES_B, obc_ref[...], obl_ref[...])
    y = (za_ref[...].astype(F32) * _dot(oa, wpa_ref[...])
         + zb_ref[...].astype(F32) * _dot(ob, wpb_ref[...]))
    out = _dot(y.astype(BF16), wo_ref[...])
    x1 = _x_tile(xa_ref, xb_ref) + mod[2:3, :] * out
    x1_ref[...] = x1
    h2 = _rms(x1, nw_ref[...]) * (1.0 + mod[4:5, :]) + mod[3:4, :]
    h2_hi = h2.astype(BF16)
    h2_lo = (h2 - h2_hi.astype(F32)).astype(BF16)
    h2x_ref[:, 0:D_MODEL] = h2
    h2x_ref[:, D_MODEL:XW] = jnp.zeros((TB, XW - D_MODEL), F32)
    d1 = _dot(h2_hi, wr1_ref[...])
    d2 = _dot(h2_lo, wr2_ref[...])
    lg_ref[...] = (d1[:, 0:N_EXPERTS] + d1[:, N_EXPERTS:2 * N_EXPERTS]) + d2[:, 0:N_EXPERTS]


def _merge(x, mod, o_f, o_b, gg, gnw, ob_ctx, ob_lat, za, zb, wpa, wpb, wo, nw, wr1, wr2):
    full = lambda a: pl.BlockSpec(a.shape, lambda i: (0,) * a.ndim)
    tok = lambda w: pl.BlockSpec((TB, w), lambda i: (i, 0))
    hb = MLA_HEADS * MLA_DV
    ctx_spec = pl.BlockSpec((TB, hb), lambda i: (jnp.minimum(i, CTX_TILES_B - 1), 0))
    lat_spec = pl.BlockSpec((TB, hb), lambda i: (jnp.maximum(i - CTX_TILES_B, 0), 0))
    xa, xb, spec_a, spec_b = _x_operands(x)
    return pl.pallas_call(
        _merge_kernel,
        out_shape=[jax.ShapeDtypeStruct((N_TOK, D_MODEL), F32),
                   jax.ShapeDtypeStruct((N_TOK, XW), F32),
                   jax.ShapeDtypeStruct((N_TOK, N_EXPERTS), F32)],
        grid=(N_TILES_B,),
        in_specs=[spec_a, spec_b, pl.BlockSpec((1, N_MOD, D_MODEL), lambda i: (_tile_seq(i), 0, 0)),
                  tok(HV), tok(HV), tok(HV), full(gnw), ctx_spec, lat_spec, tok(D_MODEL), tok(D_MODEL),
                  full(wpa), full(wpb), full(wo), full(nw), full(wr1), full(wr2)],
        out_specs=[tok(D_MODEL), tok(XW), tok(N_EXPERTS)],
        compiler_params=pltpu.CompilerParams(dimension_semantics=("arbitrary",),
                                             vmem_limit_bytes=VMEM_LIMIT),
        name="merge",
    )(xa, xb, mod, o_f, o_b, gg, gnw, ob_ctx, ob_lat, za, zb, wpa, wpb, wo, nw, wr1, wr2)


def _route_kernel(lg_ref, b_ref, tri_ref, h2x_in_ref, h2x_ref, grp_ref, rank_ref, cnt_ref, carry_ref):
    del h2x_in_ref
    step = pl.program_id(0)

    @pl.when(step == 0)
    def _():
        carry_ref[...] = jnp.zeros_like(carry_ref)

    aff = _sigmoid(lg_ref[...])
    biased = aff + b_ref[...]
    row = lambda a, e: a[e:e + 1, :]
    best = None
    sel = None
    for g in range(N_GROUPS):
        b = [row(biased, g * EXPERTS_PER_GROUP + i) for i in range(EXPERTS_PER_GROUP)]
        score = None
        for i in range(EXPERTS_PER_GROUP):
            for j in range(i + 1, EXPERTS_PER_GROUP):
                pair = b[i] + b[j]
                score = pair if score is None else jnp.maximum(score, pair)
        if g == 0:
            best, sel = score, jnp.zeros_like(score, dtype=jnp.int32)
        else:
            better = score > best
            best = jnp.where(better, score, best)
            sel = jnp.where(better, g, sel)
    cb, ca = [], []
    for i in range(EXPERTS_PER_GROUP):
        vb = row(biased, i)
        va = row(aff, i)
        for g in range(1, N_GROUPS):
            vb = jnp.where(sel == g, row(biased, g * EXPERTS_PER_GROUP + i), vb)
            va = jnp.where(sel == g, row(aff, g * EXPERTS_PER_GROUP + i), va)
        cb.append(vb)
        ca.append(va)
    picked = []
    for i in range(EXPERTS_PER_GROUP):
        rank = jnp.zeros_like(sel)
        for j in range(EXPERTS_PER_GROUP):
            if j == i:
                continue
            ahead = (cb[j] >= cb[i]) if j < i else (cb[j] > cb[i])
            rank = rank + ahead.astype(jnp.int32)
        picked.append(rank < 2)
    denom = None
    for i in range(EXPERTS_PER_GROUP):
        term = jnp.where(picked[i], ca[i], 0.0)
        denom = term if denom is None else denom + term
    cw = [jnp.where(picked[i], ca[i] / denom, 0.0) for i in range(EXPERTS_PER_GROUP)]
    cw_t = jnp.concatenate(cw + [jnp.zeros((128 - EXPERTS_PER_GROUP, ROUTE_W), F32)], axis=0)
    h2x_ref[...] = cw_t.T
    grp_ref[...] = sel

    onehot = jnp.concatenate([(sel == g).astype(F32) for g in range(N_GROUPS)]
                             + [jnp.zeros((8 - N_GROUPS, ROUTE_W), F32)], axis=0)
    carry = carry_ref[...]
    for c in range(ROUTE_W // 256):
        lanes = slice(c * 256, (c + 1) * 256)
        oh = onehot[:, lanes]
        before = _dot(oh.astype(BF16), tri_ref[...]) + carry
        sel_c = sel[:, lanes]
        r = before[N_GROUPS - 1:N_GROUPS, :]
        for g in range(N_GROUPS - 2, -1, -1):
            r = jnp.where(sel_c == g, before[g:g + 1, :], r)
        rank_ref[:, lanes] = r.astype(jnp.int32)
        carry = carry + jnp.sum(oh, axis=1, keepdims=True)
    carry_ref[...] = carry
    cnt_ref[...] = carry[:, 0:128]


def _route(lg_t, b_router, tri, h2x):
    return pl.pallas_call(
        _route_kernel,
        out_shape=[jax.ShapeDtypeStruct((N_TOK, XW), F32),
                   jax.ShapeDtypeStruct((1, N_TOK), jnp.int32),
                   jax.ShapeDtypeStruct((1, N_TOK), jnp.int32),
                   jax.ShapeDtypeStruct((8, 128), F32)],
        grid=(N_TOK // ROUTE_W,),
        in_specs=[pl.BlockSpec((N_EXPERTS, ROUTE_W), lambda i: (0, i)),
                  pl.BlockSpec((N_EXPERTS, 1), lambda i: (0, 0)),
                  pl.BlockSpec((256, 256), lambda i: (0, 0)),
                  pl.BlockSpec(memory_space=pl.ANY)],
        out_specs=[pl.BlockSpec((ROUTE_W, XW - D_MODEL), lambda i: (i, D_MODEL // (XW - D_MODEL))),
                   pl.BlockSpec((1, ROUTE_W), lambda i: (0, i)),
                   pl.BlockSpec((1, ROUTE_W), lambda i: (0, i)),
                   pl.BlockSpec((8, 128), lambda i: (0, 0))],
        scratch_shapes=[pltpu.VMEM((8, 256), F32)],
        input_output_aliases={3: 0},
        compiler_params=pltpu.CompilerParams(dimension_semantics=("arbitrary",)),
        name="route",
    )(lg_t, b_router, tri, h2x)


def _gather_rows(idx_ref, src_ref, dst_ref, sem, n_rows):
    def row_copy(r, src_row):
        return pltpu.make_async_copy(src_ref.at[pl.ds(src_row, 1), :], dst_ref.at[pl.ds(r, 1), :], sem)

    def issue(r, c):
        row_copy(r, idx_ref[0, 0, r]).start()
        return c

    def drain(r, c):
        row_copy(r, 0).wait()
        return c

    lax.fori_loop(0, n_rows, issue, 0, unroll=8)
    lax.fori_loop(0, n_rows, drain, 0, unroll=8)


def _dispatch_kernel(lo_ref, hi_ref, pos_ref, x_ref, xs_ref, zrow_ref, sem, zsem):
    def row_copy(r, slot):
        return pltpu.make_async_copy(x_ref.at[pl.ds(r, 1), :], xs_ref.at[pl.ds(slot, 1), :], sem)

    def issue(r, c):
        row_copy(r, pos_ref[0, 0, r]).start()
        return c

    def drain(r, c):
        row_copy(r, 0).wait()
        return c

    lax.fori_loop(0, TE, issue, 0, unroll=8)
    lax.fori_loop(0, TE, drain, 0, unroll=8)

    @pl.when(pl.program_id(0) == pl.num_programs(0) - 1)
    def _():
        zrow_ref[...] = jnp.zeros_like(zrow_ref)

        def zero_copy(slot):
            return pltpu.make_async_copy(zrow_ref.at[pl.ds(0, 1), :], xs_ref.at[pl.ds(slot, 1), :], zsem)

        def z_issue(slot, c):
            zero_copy(slot).start()
            return c

        def z_drain(slot, c):
            zero_copy(slot).wait()
            return c

        for g in range(N_GROUPS + 1):
            lax.fori_loop(lo_ref[g], hi_ref[g], z_issue, 0)
        for g in range(N_GROUPS + 1):
            lax.fori_loop(lo_ref[g], hi_ref[g], z_drain, 0)


def _dispatch(pad_lo, pad_hi, pos, h2x):
    return pl.pallas_call(
        _dispatch_kernel,
        out_shape=jax.ShapeDtypeStruct((N_SLOTS, XW), F32),
        grid_spec=pltpu.PrefetchScalarGridSpec(
            num_scalar_prefetch=2, grid=(N_TOK // TE,),
            in_specs=[pl.BlockSpec((1, 1, TE), lambda t, lo, hi: (t, 0, 0), memory_space=pltpu.SMEM),
                      pl.BlockSpec((TE, XW), lambda t, lo, hi: (t, 0))],
            out_specs=pl.BlockSpec(memory_space=pl.ANY),
            scratch_shapes=[pltpu.VMEM((8, XW), F32), pltpu.SemaphoreType.DMA, pltpu.SemaphoreType.DMA]),
        compiler_params=pltpu.CompilerParams(dimension_semantics=("arbitrary",)),
        name="moe_dispatch",
    )(pad_lo, pad_hi, pos.reshape(N_TOK // TE, 1, TE), h2x)


def _combine_kernel(idx_ref, ys_ref, x1_ref, mod_ref, fw_ref, *rest, final):
    if final:
        yc_ref, yl_ref, rows_ref, sem = rest
    else:
        o_ref, rows_ref, sem = rest
    _gather_rows(idx_ref, ys_ref, rows_ref, sem, TE)
    x2 = x1_ref[...] + mod_ref[0][5:6, :] * rows_ref[...]
    if not final:
        o_ref[...] = x2
        return
    y = _rms(x2, fw_ref[...])
    is_ctx = pl.program_id(0) < N_CTX // TE

    @pl.when(is_ctx)
    def _():
        yc_ref[...] = y

    @pl.when(jnp.logical_not(is_ctx))
    def _():
        yl_ref[...] = y


def _combine(pos, ys, x1, mod, fw, *, final):
    per_seq = DEC_SEQ // TE
    ctx_tiles = N_CTX // TE
    seq_of = lambda i: jnp.where(i < ctx_tiles, 0, 1 + (i - ctx_tiles) // per_seq)
    if final:
        out_shape = [jax.ShapeDtypeStruct((N_CTX, D_MODEL), F32), jax.ShapeDtypeStruct((N_LAT, D_MODEL), F32)]
        out_specs = [pl.BlockSpec((TE, D_MODEL), lambda t: (jnp.minimum(t, ctx_tiles - 1), 0)),
                     pl.BlockSpec((TE, D_MODEL), lambda t: (jnp.maximum(t - ctx_tiles, 0), 0))]
    else:
        out_shape = jax.ShapeDtypeStruct((N_TOK, D_MODEL), F32)
        out_specs = pl.BlockSpec((TE, D_MODEL), lambda t: (t, 0))
    return pl.pallas_call(
        functools.partial(_combine_kernel, final=final),
        out_shape=out_shape,
        grid=(N_TOK // TE,),
        in_specs=[pl.BlockSpec((1, 1, TE), lambda t: (t, 0, 0), memory_space=pltpu.SMEM),
                  pl.BlockSpec(memory_space=pl.ANY),
                  pl.BlockSpec((TE, D_MODEL), lambda t: (t, 0)),
                  pl.BlockSpec((1, N_MOD, D_MODEL), lambda t: (seq_of(t), 0, 0)),
                  pl.BlockSpec((1, D_MODEL), lambda t: (0, 0))],
        out_specs=out_specs,
        scratch_shapes=[pltpu.VMEM((TE, D_MODEL), F32), pltpu.SemaphoreType.DMA],
        compiler_params=pltpu.CompilerParams(dimension_semantics=("arbitrary",)),
        name="moe_combine",
    )(pos.reshape(N_TOK // TE, 1, TE), ys, x1, mod, fw)


def _moe_kernel(tg_ref, used_ref, xs_ref, wg_ref, wu_ref, wd_ref, wsg_ref, wsu_ref, wsd_ref, o_ref):
    t = pl.program_id(0)

    @pl.when(t < used_ref[0])
    def _():
        x = xs_ref[:, 0:D_MODEL].astype(BF16)
        cw = xs_ref[:, D_MODEL:XW]
        acts = []
        for j in range(EXPERTS_PER_GROUP):
            gate = _dot(x, wg_ref[j])
            up = _dot(x, wu_ref[j])
            acts.append((gate * _sigmoid(gate) * up * cw[:, j:j + 1]).astype(BF16))
        y = _dot(jnp.concatenate(acts, axis=1), wd_ref[0])
        gate = _dot(x, wsg_ref[...])
        up = _dot(x, wsu_ref[...])
        o_ref[...] = y + _dot((gate * _sigmoid(gate) * up).astype(BF16), wsd_ref[...])

    @pl.when(t >= used_ref[0])
    def _():
        o_ref[...] = jnp.zeros_like(o_ref)


def _moe(tile_grp, n_used, xs, wg, wu, wd, wsg, wsu, wsd):
    full = lambda a: pl.BlockSpec(a.shape, lambda t, tg, nu: (0,) * a.ndim)
    grp_w = lambda a: pl.BlockSpec((EXPERTS_PER_GROUP,) + a.shape[1:], lambda t, tg, nu: (tg[t], 0, 0))
    return pl.pallas_call(
        _moe_kernel,
        out_shape=jax.ShapeDtypeStruct((N_SLOTS, D_MODEL), F32),
        grid_spec=pltpu.PrefetchScalarGridSpec(
            num_scalar_prefetch=2, grid=(N_SLOT_TILES,),
            in_specs=[pl.BlockSpec((TE, XW), lambda t, tg, nu: (jnp.minimum(t, nu[0] - 1), 0)),
                      grp_w(wg), grp_w(wu),
                      pl.BlockSpec((1,) + wd.shape[1:], lambda t, tg, nu: (tg[t], 0, 0)),
                      full(wsg), full(wsu), full(wsd)],
            out_specs=pl.BlockSpec((TE, D_MODEL), lambda t, tg, nu: (t, 0))),
        compiler_params=pltpu.CompilerParams(dimension_semantics=("arbitrary",),
                                             vmem_limit_bytes=VMEM_LIMIT),
        name="moe_experts",
    )(tile_grp, n_used, xs, wg, wu, wd, wsg, wsu, wsd)


def _rope_tables():
    rows = DEC_SEQ // GRID_W
    r = jnp.repeat(jnp.arange(rows, dtype=F32), GRID_W)
    col = jnp.tile(jnp.arange(GRID_W, dtype=F32), rows)
    n_freq = MLA_ROPE // 4
    inv = ROPE_BASE ** (-jnp.arange(n_freq, dtype=F32) / n_freq)
    ang = jnp.stack([r[:, None] * inv, col[:, None] * inv], axis=1)
    expand = lambda t: jnp.broadcast_to(t[:, :, None, :], (DEC_SEQ, 2, 2, n_freq)).reshape(DEC_SEQ, MLA_ROPE)
    cos = jnp.concatenate([expand(jnp.cos(ang)), jnp.ones((TB, MLA_ROPE), F32)], axis=0)
    sin = jnp.concatenate([expand(jnp.sin(ang)), jnp.zeros((TB, MLA_ROPE), F32)], axis=0)
    return cos, sin


def _rot_cols(w):
    shp = w.shape
    w4 = w.reshape(shp[:-1] + (2, 2, MLA_ROPE // 4))
    return jnp.stack([-w4[..., 1, :], w4[..., 0, :]], axis=-2).reshape(shp)


def _gla_tables():
    blk_f, blk_b, first, seq = [], [], [], []
    for s in range(N_SEQ):
        if s < BATCH:
            tiles = [s]
        else:
            base = CTX_TILES + (s - BATCH) * LAT_TILES_PER_SEQ
            tiles = list(range(base, base + LAT_TILES_PER_SEQ))
        for n, t in enumerate(tiles):
            blk_f.append(t)
            blk_b.append(tiles[len(tiles) - 1 - n])
            first.append(1 if n == 0 else 0)
            seq.append(s)
    as_i32 = lambda v: jnp.asarray(v, dtype=jnp.int32)
    return as_i32(blk_f), as_i32(blk_b), as_i32(first), as_i32(seq)


def kernel(x_prompt, x_sample, cache_ckv, cache_krope, state_gla_fwd, state_gla_bwd, c, c_ctx, w_mod, b_mod, norm_mix, norm_ffn, w_in, w_alpha, b_alpha, gla_norm, q_norm, kv_norm, w_uq, w_uk, w_uv, w_pa, w_pb, w_o, w_router, b_router, w_exp_gate, w_exp_up, w_exp_down, w_sh_gate, w_sh_up, w_sh_down, final_norm):
    x = (x_prompt.reshape(N_CTX, D_MODEL), x_sample.reshape(N_LAT, D_MODEL))

    cond = jnp.concatenate([c_ctx[None, :], c, jnp.zeros((16 - 1 - DEC_BATCH, D_MODEL), F32)], axis=0)
    mod_all = _modulation(cond, w_mod.astype(BF16), b_mod[:, None, :])
    mod_all = mod_all.reshape(DEPTH, 16, N_MOD, D_MODEL)

    cos32, sin32 = _rope_tables()
    cq_tab = jnp.tile(cos32, (1, MLA_HEADS)) * Q_SCALE
    sq_tab = jnp.tile(sin32, (1, MLA_HEADS)) * Q_SCALE

    idx256 = jnp.arange(TM)
    same_chunk = (idx256[:, None] // GLA_CHUNK) == (idx256[None, :] // GLA_CHUNK)
    tri_f = (same_chunk & (idx256[None, :] <= idx256[:, None])).astype(BF16)
    tri_b = (same_chunk & (idx256[None, :] >= idx256[:, None])).astype(BF16)
    ones_blk = same_chunk.astype(BF16)
    gla_tab = _gla_tables()
    tri_route = (idx256[:, None] < idx256[None, :]).astype(BF16)

    wr_hi = w_router.astype(BF16)
    wr_lo = (w_router - wr_hi.astype(F32)).astype(BF16)
    zpad = jnp.zeros((D_MODEL, 128 - 2 * N_EXPERTS), BF16)
    wr1 = jnp.concatenate([wr_hi, wr_lo, zpad], axis=1)
    wr2 = jnp.concatenate([wr_hi, jnp.zeros_like(wr_lo), zpad], axis=1)

    ckvs, krs, sfs, sbs = [], [], [], []
    for l in range(DEPTH):
        mod = mod_all[l]
        (w_gq, w_gk, w_gv, w_gg, w_gaf, w_gab, w_cq, w_ckv, w_kr, w_za, w_zb) = jnp.split(
            w_in[l], (256, 512, 1024, 1536, 1552, 1568, 1824, 1952, 1984, 3008), axis=1)
        w_small = jnp.concatenate([w_gaf, w_gab, w_kr, _rot_cols(w_kr), jnp.zeros((D_MODEL, 32), F32)], axis=1)
        w1 = jnp.concatenate([w_gq, w_gk, w_gv, w_gg, w_cq, w_ckv, w_small, w_za, w_zb], axis=1).astype(BF16)
        wa = jnp.zeros((128, 2 * HK), F32)
        wa = wa.at[0:GLA_GATE_RANK, 0:HK].set(w_alpha[l, 0])
        wa = wa.at[GLA_GATE_RANK:2 * GLA_GATE_RANK, HK:2 * HK].set(w_alpha[l, 1]).astype(BF16)
        ba = b_alpha[l].reshape(1, 2 * HK)
        wuq3 = w_uq[l].reshape(MLA_Q_RANK, MLA_HEADS, MLA_NOPE + MLA_ROPE)
        wuqn = wuq3[:, :, :MLA_NOPE].reshape(MLA_Q_RANK, MLA_HEADS * MLA_NOPE).astype(BF16)
        wuq_rope = wuq3[:, :, MLA_NOPE:]
        wuqr = wuq_rope.reshape(MLA_Q_RANK, MLA_HEADS * MLA_ROPE).astype(BF16)
        wuqrr = _rot_cols(wuq_rope).reshape(MLA_Q_RANK, MLA_HEADS * MLA_ROPE).astype(BF16)
        wukt = w_uk[l].reshape(MLA_KV_RANK, MLA_HEADS, MLA_NOPE).transpose(1, 2, 0).astype(BF16)
        wuvt = w_uv[l].reshape(MLA_KV_RANK, MLA_HEADS, MLA_DV).transpose(1, 2, 0).astype(BF16)

        gq, gk, gv, gg, la, qcat, ckvn, kr, za, zb = _inproj(
            x, mod, norm_mix[l][None, :], w1, wa, ba, q_norm[l][None, :], kv_norm[l][None, :],
            wuqn, wuqr, wuqrr, wukt, cq_tab, sq_tab, cos32, sin32)

        to_t = lambda s: s.transpose(0, 3, 1, 2).reshape(s.shape[0], GLA_DV, HK)
        zeros_ctx = jnp.zeros((BATCH, GLA_DV, HK), F32)
        s0f = jnp.concatenate([zeros_ctx, to_t(state_gla_fwd[:, l])], axis=0)
        s0b = jnp.concatenate([zeros_ctx, to_t(state_gla_bwd[:, l])], axis=0)
        o_f, o_r, sf, sb = _gla(gla_tab, gq, gk, gv, la, s0f, s0b, tri_f, tri_b, ones_blk)

        ckv_ctx = ckvn[:N_CTX].reshape(BATCH, SEQ, MLA_KV_RANK)
        kr_ctx = kr[:N_CTX].reshape(BATCH, SEQ, MLA_ROPE)
        ckv_lat = jnp.concatenate([cache_ckv[:, l], ckvn[N_CTX:].reshape(DEC_BATCH, DEC_SEQ, MLA_KV_RANK)], axis=1)
        kr_lat = jnp.concatenate([cache_krope[:, l], kr[N_CTX:].reshape(DEC_BATCH, DEC_SEQ, MLA_ROPE)], axis=1)

        def kv_operands(ckv_all, kr_all):
            kcat = jnp.concatenate([ckv_all, kr_all], axis=-1).astype(BF16)
            return kcat, ckv_all.astype(BF16).transpose(0, 2, 1)

        kc, vt = kv_operands(ckv_ctx, kr_ctx)
        ob_ctx = _attention(qcat, kc, vt, wuvt, tile0=0, tiles_per_seq=1, name="mla_ctx")
        kc, vt = kv_operands(ckv_lat, kr_lat)
        ob_lat = _attention(qcat, kc, vt, wuvt, tile0=CTX_TILES, tiles_per_seq=LAT_TILES_PER_SEQ, name="mla_lat")

        x1, h2x, lg = _merge(x, mod, o_f, o_r, gg, gla_norm[l][None, :], ob_ctx, ob_lat, za, zb, w_pa[l].astype(BF16), w_pb[l].astype(BF16),
                             w_o[l].astype(BF16), norm_ffn[l][None, :], wr1, wr2)
        h2x, grp, rank, counts = _route(lg.T, b_router[:, None], tri_route, h2x)

        cnt = counts[:N_GROUPS, 0].astype(jnp.int32)
        padded = (cnt + (TE - 1)) // TE * TE
        g_end = jnp.cumsum(padded)
        pos = (g_end - padded)[grp[0]] + rank[0]
        g_start = g_end - padded
        tile_start = jnp.arange(N_SLOT_TILES, dtype=jnp.int32) * TE
        tile_grp = jnp.minimum(jnp.sum(tile_start[:, None] >= g_end[None, :], axis=1), N_GROUPS - 1).astype(jnp.int32)
        n_used = (g_end[N_GROUPS - 1:] // TE).astype(jnp.int32)

        pad_lo = jnp.concatenate([g_start + cnt, g_end[N_GROUPS - 1:]]).astype(jnp.int32)
        pad_hi = jnp.concatenate([g_end, jnp.full((1,), N_SLOTS, jnp.int32)]).astype(jnp.int32)
        xs = _dispatch(pad_lo, pad_hi, pos, h2x)
        ys = _moe(tile_grp, n_used, xs, w_exp_gate[l].astype(BF16), w_exp_up[l].astype(BF16),
                  w_exp_down[l].astype(BF16).reshape(N_GROUPS, EXPERTS_PER_GROUP * EXPERT_FF, D_MODEL),
                  w_sh_gate[l].astype(BF16), w_sh_up[l].astype(BF16), w_sh_down[l].astype(BF16))
        x = _combine(pos, ys, x1, mod, final_norm[None, :], final=(l == DEPTH - 1))

        ckvs.append(ckv_ctx)
        krs.append(kr_ctx)
        from_t = lambda s: s[:BATCH].reshape(BATCH, GLA_DV, GLA_HEADS, GLA_DK).transpose(0, 2, 3, 1)
        sfs.append(from_t(sf))
        sbs.append(from_t(sb))

    y_prompt = x[0].reshape(BATCH, SEQ, D_MODEL)
    y_sample = x[1].reshape(DEC_BATCH, DEC_SEQ, D_MODEL)
    return (y_prompt, y_sample, jnp.stack(ckvs, axis=1), jnp.stack(krs, axis=1),
            jnp.stack(sfs, axis=1), jnp.stack(sbs, axis=1))
```

```python
import functools

import jax
import jax.numpy as jnp
from jax import lax
from jax.experimental import pallas as pl
from jax.experimental.pallas import tpu as pltpu

F32 = jnp.float32
BF16 = jnp.bfloat16

D_MODEL = 1024
BATCH = 16
SEQ = 256
DEPTH = 2
DEC_BATCH = 8
DEC_SEQ = 4096
PAST_LEN = 512
GRID_W = 64
GLA_HEADS = 4
GLA_DK = 64
GLA_DV = 128
GLA_GATE_RANK = 16
GLA_TAU = 16.0
GLA_CHUNK = 64
MLA_HEADS = 8
MLA_Q_RANK = 256
MLA_KV_RANK = 128
MLA_NOPE = 64
MLA_ROPE = 32
MLA_DV = 64
ROPE_BASE = 10000.0
N_EXPERTS = 16
N_GROUPS = 4
EXPERTS_PER_GROUP = 4
EXPERT_FF = 512
SHARED_FF = 512
N_MOD = 6
EPS = 1e-6

N_CTX = BATCH * SEQ
N_LAT = DEC_BATCH * DEC_SEQ
N_TOK = N_CTX + N_LAT
N_SEQ = BATCH + DEC_BATCH
TM = 256
N_TILES = N_TOK // TM
CTX_TILES = N_CTX // TM
LAT_TILES_PER_SEQ = DEC_SEQ // TM
TB = 512
N_TILES_B = N_TOK // TB
CTX_TILES_B = N_CTX // TB
LAT_TILES_B_PER_SEQ = DEC_SEQ // TB
HK = GLA_HEADS * GLA_DK
HV = GLA_HEADS * GLA_DV
QCAT = MLA_KV_RANK + MLA_ROPE
Q_SCALE =(MLA_NOPE + MLA_ROPE) ** -0.5 * 1.4426950408889634
TE = 512
N_SLOT_TILES = N_TOK // TE + N_GROUPS
N_SLOTS = N_SLOT_TILES * TE
XW = D_MODEL + 128
ROUTE_W = 2048
VMEM_LIMIT = 56 * 1024 * 1024

C_GQ, C_GK, C_GV, C_GG, C_CQ, C_CKV, C_SMALL, C_ZA, C_ZB, C_END = (
    0, 256, 512, 1024, 1536, 1792, 1920, 2048, 3072, 4096)


def _sigmoid(x):
    return 1.0 / (1.0 + jnp.exp(-x))


def _rms(x, w):
    return x * lax.rsqrt(jnp.mean(x * x, axis=-1, keepdims=True) + EPS) * w


def _dot(a, b):
    return jnp.dot(a, b, preferred_element_type=F32)


def _dot_nt(a, b):
    return lax.dot_general(a, b, (((1,), (1,)), ((), ())), preferred_element_type=F32)


def _dot_tn(a, b):
    return lax.dot_general(a, b, (((0,), (0,)), ((), ())), preferred_element_type=F32)


def _tile_seq(i):
    return jnp.where(i < CTX_TILES_B, 0, 1 + (i - CTX_TILES_B) // LAT_TILES_B_PER_SEQ)


def _tile_pos(i):
    return jnp.where(i < CTX_TILES_B, LAT_TILES_B_PER_SEQ, (i - CTX_TILES_B) % LAT_TILES_B_PER_SEQ)


def _x_operands(x):
    if isinstance(x, tuple):
        xa, xb, off = x[0], x[1], 0
    else:
        xa, xb, off = x, x, CTX_TILES_B
    spec_a = pl.BlockSpec((TB, D_MODEL), lambda i: (jnp.minimum(i, CTX_TILES_B - 1), 0))
    spec_b = pl.BlockSpec((TB, D_MODEL), lambda i: (jnp.maximum(i - CTX_TILES_B, 0) + off, 0))
    return xa, xb, spec_a, spec_b


def _x_tile(xa_ref, xb_ref):
    return jnp.where(pl.program_id(0) < CTX_TILES_B, xa_ref[...], xb_ref[...])


def _mod_kernel(c_ref, w_ref, b_ref, o_ref):
    c = c_ref[...]
    sc = (c * _sigmoid(c)).astype(BF16)
    o_ref[...] = _dot(sc, w_ref[...]) + b_ref[...]


def _modulation(cond, w_mod, b_mod):
    nb = 1024
    return pl.pallas_call(
        _mod_kernel,
        out_shape=jax.ShapeDtypeStruct((DEPTH, 16, N_MOD * D_MODEL), F32),
        grid=(DEPTH, N_MOD * D_MODEL // nb),
        in_specs=[
            pl.BlockSpec((16, D_MODEL), lambda l, j: (0, 0)),
            pl.BlockSpec((None, D_MODEL, nb), lambda l, j: (l, 0, j)),
            pl.BlockSpec((None, 1, nb), lambda l, j: (l, 0, j)),
        ],
        out_specs=pl.BlockSpec((None, 16, nb), lambda l, j: (l, 0, j)),
        name="modulation",
    )(cond, w_mod, b_mod)


def _inproj_kernel(xa_ref, xb_ref, mod_ref, nw_ref, w1_ref, wa_ref, ba_ref, qnw_ref, kvnw_ref,
                   wuqn_ref, wuqr_ref, wuqrr_ref, wukt_ref, cq_ref, sq_ref, ck_ref, sk_ref,
                   gq_ref, gk_ref, gv_ref, gg_ref, la_ref, q_ref, ckv_ref, kr_ref, za_ref, zb_ref):
    x = _x_tile(xa_ref, xb_ref)
    mod = mod_ref[0]
    h = (_rms(x, nw_ref[...]) * (1.0 + mod[1:2, :]) + mod[0:1, :]).astype(BF16)

    def proj(lo, hi):
        return _dot(h, w1_ref[:, lo:hi])

    gq_ref[...] = (proj(C_GQ, C_GK) * (GLA_DK ** -0.5)).astype(BF16)
    gk_ref[...] = proj(C_GK, C_GV).astype(BF16)
    gv_ref[...] = proj(C_GV, C_GG).astype(BF16)
    gg_ref[...] = proj(C_GG, C_CQ).astype(BF16)
    za_ref[...] = _sigmoid(proj(C_ZA, C_ZB)).astype(BF16)
    zb_ref[...] = _sigmoid(proj(C_ZB, C_END)).astype(BF16)

    small = proj(C_SMALL, C_ZA)
    lin = _dot(small.astype(BF16), wa_ref[...]) + ba_ref[...]
    la_ref[...] = (jnp.minimum(lin, 0.0) - jnp.log(1.0 + jnp.exp(-jnp.abs(lin)))) * (1.0 / GLA_TAU)
    kr_ref[...] = small[:, 32:64] * ck_ref[...] + small[:, 64:96] * sk_ref[...]

    ckv_ref[...] = _rms(proj(C_CKV, C_SMALL), kvnw_ref[...])

    cqn = _rms(proj(C_CQ, C_CKV), qnw_ref[...]).astype(BF16)
    qn = _dot(cqn, wuqn_ref[...]).astype(BF16)
    qr = (_dot(cqn, wuqr_ref[...]) * cq_ref[...]
          + _dot(cqn, wuqrr_ref[...]) * sq_ref[...])
    for hd in range(MLA_HEADS):
        qa = _dot(qn[:, hd * MLA_NOPE:(hd + 1) * MLA_NOPE], wukt_ref[hd]) * Q_SCALE
        q_ref[hd, :, 0:MLA_KV_RANK] = qa.astype(BF16)
        q_ref[hd, :, MLA_KV_RANK:QCAT] = qr[:, hd * MLA_ROPE:(hd + 1) * MLA_ROPE].astype(BF16)


def _inproj(x, mod, nw, w1, wa, ba, qnw, kvnw, wuqn, wuqr, wuqrr, wukt, cq, sq, ck, sk):
    full = lambda a: pl.BlockSpec(a.shape, lambda i: (0,) * a.ndim)
    tok = lambda w: pl.BlockSpec((TB, w), lambda i: (i, 0))
    pos = lambda w: pl.BlockSpec((TB, w), lambda i: (_tile_pos(i), 0))
    xa, xb, spec_a, spec_b = _x_operands(x)
    out_shape = [
        jax.ShapeDtypeStruct((N_TOK, HK), BF16),
        jax.ShapeDtypeStruct((N_TOK, HK), BF16),
        jax.ShapeDtypeStruct((N_TOK, HV), BF16),
        jax.ShapeDtypeStruct((N_TOK, HV), BF16),
        jax.ShapeDtypeStruct((N_TOK, 2 * HK), F32),
        jax.ShapeDtypeStruct((MLA_HEADS, N_TOK, QCAT), BF16),
        jax.ShapeDtypeStruct((N_TOK, MLA_KV_RANK), F32),
        jax.ShapeDtypeStruct((N_TOK, MLA_ROPE), F32),
        jax.ShapeDtypeStruct((N_TOK, D_MODEL), BF16),
        jax.ShapeDtypeStruct((N_TOK, D_MODEL), BF16),
    ]
    out_specs = [tok(HK), tok(HK), tok(HV), tok(HV), tok(2 * HK),
                 pl.BlockSpec((MLA_HEADS, TB, QCAT), lambda i: (0, i, 0)),
                 tok(MLA_KV_RANK), tok(MLA_ROPE), tok(D_MODEL), tok(D_MODEL)]
    in_specs = [spec_a, spec_b,
                pl.BlockSpec((1, N_MOD, D_MODEL), lambda i: (_tile_seq(i), 0, 0)),
                full(nw), full(w1), full(wa), full(ba), full(qnw), full(kvnw),
                full(wuqn), full(wuqr), full(wuqrr), full(wukt),
                pos(HK), pos(HK), pos(MLA_ROPE), pos(MLA_ROPE)]
    return pl.pallas_call(
        _inproj_kernel, out_shape=out_shape, grid=(N_TILES_B,),
        in_specs=in_specs, out_specs=out_specs,
        compiler_params=pltpu.CompilerParams(dimension_semantics=("arbitrary",),
                                             vmem_limit_bytes=VMEM_LIMIT),
        name="inproj",
    )(xa, xb, mod, nw, w1, wa, ba, qnw, kvnw, wuqn, wuqr, wuqrr, wukt, cq, sq, ck, sk)


def _gla_tile(q_ref, k_ref, v_ref, la_ref, st_ref, o_ref, tri, ones_blk, *, reverse):
    n_chunks = TM // GLA_CHUNK
    la = la_ref[...]
    la_hi = la.astype(BF16)
    la_lo = (la - la_hi.astype(F32)).astype(BF16)
    cum = _dot(tri, la_hi) + _dot(tri, la_lo)
    tot = _dot(ones_blk, la_hi) + _dot(ones_blk, la_lo)
    q = q_ref[...].astype(F32)
    k = k_ref[...].astype(F32)
    qd = q * jnp.exp(cum)
    kin = (k * jnp.exp(-cum)).astype(BF16)
    kout = k * jnp.exp(tot - cum)
    decay = jnp.exp(tot)
    lane = lax.broadcasted_iota(jnp.int32, (TM, HK), 1)
    head_of_lane = lane // GLA_DK
    qd_h = [jnp.where(head_of_lane == hd, qd, 0.0).astype(BF16) for hd in range(GLA_HEADS)]
    kout_h = [jnp.where(head_of_lane == hd, kout, 0.0).astype(BF16) for hd in range(GLA_HEADS)]
    row = lax.broadcasted_iota(jnp.int32, (TM, TM), 0)
    col = lax.broadcasted_iota(jnp.int32, (TM, TM), 1)
    keep = (row // GLA_CHUNK == col // GLA_CHUNK) & ((col >= row) if reverse else (col <= row))
    head_vs = [slice(hd * GLA_DV, (hd + 1) * GLA_DV) for hd in range(GLA_HEADS)]

    att_all = _dot_nt(jnp.concatenate(qd_h, axis=0), kin)
    o_intra = [_dot(jnp.where(keep, att_all[hd * TM:(hd + 1) * TM], 0.0).astype(BF16), v_ref[:, head_vs[hd]])
               for hd in range(GLA_HEADS)]

    st = st_ref[...]
    for ci in range(n_chunks):
        c = (n_chunks - 1 - ci) if reverse else ci
        rows = slice(c * GLA_CHUNK, (c + 1) * GLA_CHUNK)
        q_stack = jnp.concatenate([qd_h[hd][rows] for hd in range(GLA_HEADS)], axis=0)
        o_inter = _dot_nt(q_stack, st.astype(BF16))
        for hd in range(GLA_HEADS):
            o = o_intra[hd][rows] + o_inter[hd * GLA_CHUNK:(hd + 1) * GLA_CHUNK]
            o_ref[rows, head_vs[hd]] = o.astype(o_ref.dtype)
        v_stack = jnp.concatenate([v_ref[rows, head_vs[hd]] for hd in range(GLA_HEADS)], axis=0)
        k_stack = jnp.concatenate([kout_h[hd][rows] for hd in range(GLA_HEADS)], axis=0)
        st = st * decay[c * GLA_CHUNK:c * GLA_CHUNK + 1, :] + _dot_tn(v_stack, k_stack)
    st_ref[...] = st


def _gla_kernel(blkf_ref, blkb_ref, first_ref, seq_ref,
                qf_ref, kf_ref, vf_ref, laf_ref, qb_ref, kb_ref, vb_ref, lab_ref, s0f_ref, s0b_ref,
                trif_ref, trib_ref, ones_ref,
                of_ref, ob_ref, sff_ref, sfb_ref, stf_ref, stb_ref):
    step = pl.program_id(0)

    @pl.when(first_ref[step] == 1)
    def _():
        stf_ref[...] = s0f_ref[0]
        stb_ref[...] = s0b_ref[0]

    _gla_tile(qf_ref, kf_ref, vf_ref, laf_ref, stf_ref, of_ref, trif_ref[...], ones_ref[...], reverse=False)
    _gla_tile(qb_ref, kb_ref, vb_ref, lab_ref, stb_ref, ob_ref, trib_ref[...], ones_ref[...], reverse=True)
    sff_ref[0] = stf_ref[...]
    sfb_ref[0] = stb_ref[...]


def _gla(tables, q, k, v, la, s0f, s0b, trif, trib, ones_blk):
    blkf, blkb, first, seq = tables
    fwd = lambda w, j=0: pl.BlockSpec((TM, w), lambda s, bf, bb, f, q_: (bf[s], j))
    bwd = lambda w, j=0: pl.BlockSpec((TM, w), lambda s, bf, bb, f, q_: (bb[s], j))
    per_seq = pl.BlockSpec((1, GLA_DV, HK), lambda s, bf, bb, f, q_: (q_[s], 0, 0))
    const = pl.BlockSpec((TM, TM), lambda s, bf, bb, f, q_: (0, 0))
    return pl.pallas_call(
        _gla_kernel,
        out_shape=[jax.ShapeDtypeStruct((N_TOK, HV), BF16), jax.ShapeDtypeStruct((N_TOK, HV), BF16),
                   jax.ShapeDtypeStruct((N_SEQ, GLA_DV, HK), F32), jax.ShapeDtypeStruct((N_SEQ, GLA_DV, HK), F32)],
        grid_spec=pltpu.PrefetchScalarGridSpec(
            num_scalar_prefetch=4, grid=(N_TILES,),
            in_specs=[fwd(HK), fwd(HK), fwd(HV), fwd(HK, 0), bwd(HK), bwd(HK), bwd(HV), bwd(HK, 1),
                      per_seq, per_seq, const, const, const],
            out_specs=[fwd(HV), bwd(HV), per_seq, per_seq],
            scratch_shapes=[pltpu.VMEM((GLA_DV, HK), F32), pltpu.VMEM((GLA_DV, HK), F32)]),
        compiler_params=pltpu.CompilerParams(dimension_semantics=("arbitrary",)),
        name="gla",
    )(blkf, blkb, first, seq, q, k, v, la, q, k, v, la, s0f, s0b, trif, trib, ones_blk)


def _attn_kernel(q_ref, k_ref, vt_ref, wuvt_ref, o_ref, s_ref, p_ref, ot_ref, vh_ref):
    @pl.when(pl.program_id(1) == 0)
    def _():
        for hd in range(MLA_HEADS):
            vh_ref[hd] = _dot(wuvt_ref[hd], vt_ref[0]).astype(BF16)

    def scores(hd):
        s = _dot_nt(k_ref[0], q_ref[hd])
        s_ref[hd % 2] = s
        return jnp.max(s, axis=0, keepdims=True)

    m = scores(0)
    for hd in range(MLA_HEADS):
        m_next = scores(hd + 1) if hd + 1 < MLA_HEADS else None
        p = jnp.exp2(s_ref[hd % 2] - m)
        l = jnp.sum(p, axis=0, keepdims=True)
        p_ref[hd % 2] = p.astype(BF16)
        ot_ref[hd] = _dot(vh_ref[hd], p_ref[hd % 2]) / l
        m = m_next
    o_ref[...] = ot_ref[...].reshape(MLA_HEADS * MLA_DV, TM).T.astype(o_ref.dtype)


def _attention(q, kcat, vt, wuvt, *, tile0, tiles_per_seq, name):
    n_seq, s_len, _ = kcat.shape
    return pl.pallas_call(
        _attn_kernel,
        out_shape=jax.ShapeDtypeStruct((n_seq * tiles_per_seq * TM, MLA_HEADS * MLA_DV), BF16),
        grid=(n_seq, tiles_per_seq),
        in_specs=[
            pl.BlockSpec((MLA_HEADS, TM, QCAT), lambda b, i: (0, tile0 + b * tiles_per_seq + i, 0)),
            pl.BlockSpec((1, s_len, QCAT), lambda b, i: (b, 0, 0)),
            pl.BlockSpec((1, MLA_KV_RANK, s_len), lambda b, i: (b, 0, 0)),
            pl.BlockSpec(wuvt.shape, lambda b, i: (0, 0, 0)),
        ],
        out_specs=pl.BlockSpec((TM, MLA_HEADS * MLA_DV), lambda b, i: (b * tiles_per_seq + i, 0)),
        scratch_shapes=[pltpu.VMEM((2, s_len, TM), F32), pltpu.VMEM((2, s_len, TM), BF16),
                        pltpu.VMEM((MLA_HEADS, MLA_DV, TM), F32),
                        pltpu.VMEM((MLA_HEADS, MLA_DV, s_len), BF16)],
        compiler_params=pltpu.CompilerParams(dimension_semantics=("arbitrary", "arbitrary"),
                                             vmem_limit_bytes=VMEM_LIMIT),
        name=name,
    )(q, kcat, vt, wuvt)


def _merge_kernel(xa_ref, xb_ref, mod_ref, of_ref, ob_ref, gg_ref, gnw_ref, obc_ref, obl_ref, za_ref, zb_ref,
                  wpa_ref, wpb_ref, wo_ref, nw_ref, wr1_ref, wr2_ref, x1_ref, h2x_ref, lg_ref):
    mod = mod_ref[0]
    o_sum = of_ref[...].astype(F32) + ob_ref[...].astype(F32)
    gate = gg_ref[...].astype(F32)
    gate = gate * _sigmoid(gate)
    oa = jnp.concatenate(
        [(_rms(o_sum[:, hd * GLA_DV:(hd + 1) * GLA_DV], gnw_ref[...])
          * gate[:, hd * GLA_DV:(hd + 1) * GLA_DV]).astype(BF16) for hd in range(GLA_HEADS)], axis=1)
    ob = jnp.where(pl.program_id(0) < CTX_TILES_B, obc_ref[...], obl_ref[...])
    y = (za_ref[...].astype(F32) * _dot(oa, wpa_ref[...])
         + zb_ref[...].astype(F32) * _dot(ob, wpb_ref[...]))
    out = _dot(y.astype(BF16), wo_ref[...])
    x1 = _x_tile(xa_ref, xb_ref) + mod[2:3, :] * out
    x1_ref[...] = x1
    h2 = _rms(x1, nw_ref[...]) * (1.0 + mod[4:5, :]) + mod[3:4, :]
    h2_hi = h2.astype(BF16)
    h2_lo = (h2 - h2_hi.astype(F32)).astype(BF16)
    h2x_ref[:, 0:D_MODEL] = h2
    h2x_ref[:, D_MODEL:XW] = jnp.zeros((TB, XW - D_MODEL), F32)
    d1 = _dot(h2_hi, wr1_ref[...])
    d2 = _dot(h2_lo, wr2_ref[...])
    lg_ref[...] = (d1[:, 0:N_EXPERTS] + d1[:, N_EXPERTS:2 * N_EXPERTS]) + d2[:, 0:N_EXPERTS]


def _merge(x, mod, o_f, o_b, gg, gnw, ob_ctx, ob_lat, za, zb, wpa, wpb, wo, nw, wr1, wr2):
    full = lambda a: pl.BlockSpec(a.shape, lambda i: (0,) * a.ndim)
    tok = lambda w: pl.BlockSpec((TB, w), lambda i: (i, 0))
    hb = MLA_HEADS * MLA_DV
    ctx_spec = pl.BlockSpec((TB, hb), lambda i: (jnp.minimum(i, CTX_TILES_B - 1), 0))
    lat_spec = pl.BlockSpec((TB, hb), lambda i: (jnp.maximum(i - CTX_TILES_B, 0), 0))
    xa, xb, spec_a, spec_b = _x_operands(x)
    return pl.pallas_call(
        _merge_kernel,
        out_shape=[jax.ShapeDtypeStruct((N_TOK, D_MODEL), F32),
                   jax.ShapeDtypeStruct((N_TOK, XW), F32),
                   jax.ShapeDtypeStruct((N_TOK, N_EXPERTS), F32)],
        grid=(N_TILES_B,),
        in_specs=[spec_a, spec_b, pl.BlockSpec((1, N_MOD, D_MODEL), lambda i: (_tile_seq(i), 0, 0)),
                  tok(HV), tok(HV), tok(HV), full(gnw), ctx_spec, lat_spec, tok(D_MODEL), tok(D_MODEL),
                  full(wpa), full(wpb), full(wo), full(nw), full(wr1), full(wr2)],
        out_specs=[tok(D_MODEL), tok(XW), tok(N_EXPERTS)],
        compiler_params=pltpu.CompilerParams(dimension_semantics=("arbitrary",),
                                             vmem_limit_bytes=VMEM_LIMIT),
        name="merge",
    )(xa, xb, mod, o_f, o_b, gg, gnw, ob_ctx, ob_lat, za, zb, wpa, wpb, wo, nw, wr1, wr2)


def _route_kernel(lg_ref, b_ref, tri_ref, h2x_in_ref, h2x_ref, grp_ref, rank_ref, cnt_ref, carry_ref):
    del h2x_in_ref
    step = pl.program_id(0)

    @pl.when(step == 0)
    def _():
        carry_ref[...] = jnp.zeros_like(carry_ref)

    aff = _sigmoid(lg_ref[...])
    biased = aff + b_ref[...]
    row = lambda a, e: a[e:e + 1, :]
    best = None
    sel = None
    for g in range(N_GROUPS):
        b = [row(biased, g * EXPERTS_PER_GROUP + i) for i in range(EXPERTS_PER_GROUP)]
        score = None
        for i in range(EXPERTS_PER_GROUP):
            for j in range(i + 1, EXPERTS_PER_GROUP):
                pair = b[i] + b[j]
                score = pair if score is None else jnp.maximum(score, pair)
        if g == 0:
            best, sel = score, jnp.zeros_like(score, dtype=jnp.int32)
        else:
            better = score > best
            best = jnp.where(better, score, best)
            sel = jnp.where(better, g, sel)
    cb, ca = [], []
    for i in range(EXPERTS_PER_GROUP):
        vb = row(biased, i)
        va = row(aff, i)
        for g in range(1, N_GROUPS):
            vb = jnp.where(sel == g, row(biased, g * EXPERTS_PER_GROUP + i), vb)
            va = jnp.where(sel == g, row(aff, g * EXPERTS_PER_GROUP + i), va)
        cb.append(vb)
        ca.append(va)
    picked = []
    for i in range(EXPERTS_PER_GROUP):
        rank = jnp.zeros_like(sel)
        for j in range(EXPERTS_PER_GROUP):
            if j == i:
                continue
            ahead = (cb[j] >= cb[i]) if j < i else (cb[j] > cb[i])
            rank = rank + ahead.astype(jnp.int32)
        picked.append(rank < 2)
    denom = None
    for i in range(EXPERTS_PER_GROUP):
        term = jnp.where(picked[i], ca[i], 0.0)
        denom = term if denom is None else denom + term
    cw = [jnp.where(picked[i], ca[i] / denom, 0.0) for i in range(EXPERTS_PER_GROUP)]
    cw_t = jnp.concatenate(cw + [jnp.zeros((128 - EXPERTS_PER_GROUP, ROUTE_W), F32)], axis=0)
    h2x_ref[...] = cw_t.T
    grp_ref[...] = sel

    onehot = jnp.concatenate([(sel == g).astype(F32) for g in range(N_GROUPS)]
                             + [jnp.zeros((8 - N_GROUPS, ROUTE_W), F32)], axis=0)
    carry = carry_ref[...]
    for c in range(ROUTE_W // 256):
        lanes = slice(c * 256, (c + 1) * 256)
        oh = onehot[:, lanes]
        before = _dot(oh.astype(BF16), tri_ref[...]) + carry
        sel_c = sel[:, lanes]
        r = before[N_GROUPS - 1:N_GROUPS, :]
        for g in range(N_GROUPS - 2, -1, -1):
            r = jnp.where(sel_c == g, before[g:g + 1, :], r)
        rank_ref[:, lanes] = r.astype(jnp.int32)
        carry = carry + jnp.sum(oh, axis=1, keepdims=True)
    carry_ref[...] = carry
    cnt_ref[...] = carry[:, 0:128]


def _route(lg_t, b_router, tri, h2x):
    return pl.pallas_call(
        _route_kernel,
        out_shape=[jax.ShapeDtypeStruct((N_TOK, XW), F32),
                   jax.ShapeDtypeStruct((1, N_TOK), jnp.int32),
                   jax.ShapeDtypeStruct((1, N_TOK), jnp.int32),
                   jax.ShapeDtypeStruct((8, 128), F32)],
        grid=(N_TOK // ROUTE_W,),
        in_specs=[pl.BlockSpec((N_EXPERTS, ROUTE_W), lambda i: (0, i)),
                  pl.BlockSpec((N_EXPERTS, 1), lambda i: (0, 0)),
                  pl.BlockSpec((256, 256), lambda i: (0, 0)),
                  pl.BlockSpec(memory_space=pl.ANY)],
        out_specs=[pl.BlockSpec((ROUTE_W, XW - D_MODEL), lambda i: (i, D_MODEL // (XW - D_MODEL))),
                   pl.BlockSpec((1, ROUTE_W), lambda i: (0, i)),
                   pl.BlockSpec((1, ROUTE_W), lambda i: (0, i)),
                   pl.BlockSpec((8, 128), lambda i: (0, 0))],
        scratch_shapes=[pltpu.VMEM((8, 256), F32)],
        input_output_aliases={3: 0},
        compiler_params=pltpu.CompilerParams(dimension_semantics=("arbitrary",)),
        name="route",
    )(lg_t, b_router, tri, h2x)


def _gather_rows_start(idx_ref, src_ref, dst_ref, sem, n_rows):
    def issue(r, c):
        pltpu.make_async_copy(src_ref.at[pl.ds(idx_ref[0, 0, r], 1), :], dst_ref.at[pl.ds(r, 1), :], sem).start()
        return c

    lax.fori_loop(0, n_rows, issue, 0, unroll=8)


def _gather_rows_wait(src_ref, dst_ref, sem, n_rows):
    def drain(r, c):
        pltpu.make_async_copy(src_ref.at[pl.ds(0, 1), :], dst_ref.at[pl.ds(r, 1), :], sem).wait()
        return c

    lax.fori_loop(0, n_rows, drain, 0, unroll=8)


def _dispatch_kernel(lo_ref, hi_ref, pos_ref, x_ref, xs_ref, zrow_ref, sem, zsem):
    def row_copy(r, slot):
        return pltpu.make_async_copy(x_ref.at[pl.ds(r, 1), :], xs_ref.at[pl.ds(slot, 1), :], sem)

    def issue(r, c):
        row_copy(r, pos_ref[0, 0, r]).start()
        return c

    def drain(r, c):
        row_copy(r, 0).wait()
        return c

    lax.fori_loop(0, TE, issue, 0, unroll=8)
    lax.fori_loop(0, TE, drain, 0, unroll=8)

    @pl.when(pl.program_id(0) == pl.num_programs(0) - 1)
    def _():
        zrow_ref[...] = jnp.zeros_like(zrow_ref)

        def zero_copy(slot):
            return pltpu.make_async_copy(zrow_ref.at[pl.ds(0, 1), :], xs_ref.at[pl.ds(slot, 1), :], zsem)

        def z_issue(slot, c):
            zero_copy(slot).start()
            return c

        def z_drain(slot, c):
            zero_copy(slot).wait()
            return c

        for g in range(N_GROUPS + 1):
            lax.fori_loop(lo_ref[g], hi_ref[g], z_issue, 0)
        for g in range(N_GROUPS + 1):
            lax.fori_loop(lo_ref[g], hi_ref[g], z_drain, 0)


def _dispatch(pad_lo, pad_hi, pos, h2x):
    return pl.pallas_call(
        _dispatch_kernel,
        out_shape=jax.ShapeDtypeStruct((N_SLOTS, XW), F32),
        grid_spec=pltpu.PrefetchScalarGridSpec(
            num_scalar_prefetch=2, grid=(N_TOK // TE,),
            in_specs=[pl.BlockSpec((1, 1, TE), lambda t, lo, hi: (t, 0, 0), memory_space=pltpu.SMEM),
                      pl.BlockSpec((TE, XW), lambda t, lo, hi: (t, 0))],
            out_specs=pl.BlockSpec(memory_space=pl.ANY),
            scratch_shapes=[pltpu.VMEM((8, XW), F32), pltpu.SemaphoreType.DMA, pltpu.SemaphoreType.DMA]),
        compiler_params=pltpu.CompilerParams(dimension_semantics=("arbitrary",)),
        name="moe_dispatch",
    )(pad_lo, pad_hi, pos.reshape(N_TOK // TE, 1, TE), h2x)


def _combine_kernel(idx_ref, idx_next_ref, ys_ref, x1_ref, mod_ref, fw_ref, *rest, final):
    if final:
        yc_ref, yl_ref, rows_ref, sems = rest
    else:
        o_ref, rows_ref, sems = rest
    t = pl.program_id(0)
    slot = t % 2

    @pl.when(t == 0)
    def _():
        _gather_rows_start(idx_ref, ys_ref, rows_ref.at[0], sems.at[0], TE)

    @pl.when(t + 1 < pl.num_programs(0))
    def _():
        _gather_rows_start(idx_next_ref, ys_ref, rows_ref.at[1 - slot], sems.at[1 - slot], TE)

    _gather_rows_wait(ys_ref, rows_ref.at[slot], sems.at[slot], TE)
    x2 = x1_ref[...] + mod_ref[0][5:6, :] * rows_ref[slot]
    if not final:
        o_ref[...] = x2
        return
    y = _rms(x2, fw_ref[...])
    is_ctx = pl.program_id(0) < N_CTX // TE

    @pl.when(is_ctx)
    def _():
        yc_ref[...] = y

    @pl.when(jnp.logical_not(is_ctx))
    def _():
        yl_ref[...] = y


def _combine(pos, ys, x1, mod, fw, *, final):
    per_seq = DEC_SEQ // TE
    ctx_tiles = N_CTX // TE
    seq_of = lambda i: jnp.where(i < ctx_tiles, 0, 1 + (i - ctx_tiles) // per_seq)
    n_tiles = N_TOK // TE
    pos_tiles = pos.reshape(n_tiles, 1, TE)
    if final:
        out_shape = [jax.ShapeDtypeStruct((N_CTX, D_MODEL), F32), jax.ShapeDtypeStruct((N_LAT, D_MODEL), F32)]
        out_specs = [pl.BlockSpec((TE, D_MODEL), lambda t: (jnp.minimum(t, ctx_tiles - 1), 0)),
                     pl.BlockSpec((TE, D_MODEL), lambda t: (jnp.maximum(t - ctx_tiles, 0), 0))]
    else:
        out_shape = jax.ShapeDtypeStruct((N_TOK, D_MODEL), F32)
        out_specs = pl.BlockSpec((TE, D_MODEL), lambda t: (t, 0))
    return pl.pallas_call(
        functools.partial(_combine_kernel, final=final),
        out_shape=out_shape,
        grid=(n_tiles,),
        in_specs=[pl.BlockSpec((1, 1, TE), lambda t: (t, 0, 0), memory_space=pltpu.SMEM),
                  pl.BlockSpec((1, 1, TE), lambda t: (jnp.minimum(t + 1, n_tiles - 1), 0, 0),
                               memory_space=pltpu.SMEM),
                  pl.BlockSpec(memory_space=pl.ANY),
                  pl.BlockSpec((TE, D_MODEL), lambda t: (t, 0)),
                  pl.BlockSpec((1, N_MOD, D_MODEL), lambda t: (seq_of(t), 0, 0)),
                  pl.BlockSpec((1, D_MODEL), lambda t: (0, 0))],
        out_specs=out_specs,
        scratch_shapes=[pltpu.VMEM((2, TE, D_MODEL), F32), pltpu.SemaphoreType.DMA((2,))],
        compiler_params=pltpu.CompilerParams(dimension_semantics=("arbitrary",)),
        name="moe_combine",
    )(pos_tiles, pos_tiles, ys, x1, mod, fw)


def _moe_kernel(tg_ref, used_ref, xs_ref, wg_ref, wu_ref, wd_ref, wsg_ref, wsu_ref, wsd_ref, o_ref):
    t = pl.program_id(0)

    @pl.when(t < used_ref[0])
    def _():
        x = xs_ref[:, 0:D_MODEL].astype(BF16)
        cw = xs_ref[:, D_MODEL:XW]
        acts = []
        for j in range(EXPERTS_PER_GROUP):
            gate = _dot(x, wg_ref[j])
            up = _dot(x, wu_ref[j])
            acts.append((gate * _sigmoid(gate) * up * cw[:, j:j + 1]).astype(BF16))
        y = _dot(jnp.concatenate(acts, axis=1), wd_ref[0])
        gate = _dot(x, wsg_ref[...])
        up = _dot(x, wsu_ref[...])
        o_ref[...] = y + _dot((gate * _sigmoid(gate) * up).astype(BF16), wsd_ref[...])

    @pl.when(t >= used_ref[0])
    def _():
        o_ref[...] = jnp.zeros_like(o_ref)


def _moe(tile_grp, n_used, xs, wg, wu, wd, wsg, wsu, wsd):
    full = lambda a: pl.BlockSpec(a.shape, lambda t, tg, nu: (0,) * a.ndim)
    grp_w = lambda a: pl.BlockSpec((EXPERTS_PER_GROUP,) + a.shape[1:], lambda t, tg, nu: (tg[t], 0, 0))
    return pl.pallas_call(
        _moe_kernel,
        out_shape=jax.ShapeDtypeStruct((N_SLOTS, D_MODEL), F32),
        grid_spec=pltpu.PrefetchScalarGridSpec(
            num_scalar_prefetch=2, grid=(N_SLOT_TILES,),
            in_specs=[pl.BlockSpec((TE, XW), lambda t, tg, nu: (jnp.minimum(t, nu[0] - 1), 0)),
                      grp_w(wg), grp_w(wu),
                      pl.BlockSpec((1,) + wd.shape[1:], lambda t, tg, nu: (tg[t], 0, 0)),
                      full(wsg), full(wsu), full(wsd)],
            out_specs=pl.BlockSpec((TE, D_MODEL), lambda t, tg, nu: (t, 0))),
        compiler_params=pltpu.CompilerParams(dimension_semantics=("arbitrary",),
                                             vmem_limit_bytes=VMEM_LIMIT),
        name="moe_experts",
    )(tile_grp, n_used, xs, wg, wu, wd, wsg, wsu, wsd)


def _rope_tables():
    rows = DEC_SEQ // GRID_W
    r = jnp.repeat(jnp.arange(rows, dtype=F32), GRID_W)
    col = jnp.tile(jnp.arange(GRID_W, dtype=F32), rows)
    n_freq = MLA_ROPE // 4
    inv = ROPE_BASE ** (-jnp.arange(n_freq, dtype=F32) / n_freq)
    ang = jnp.stack([r[:, None] * inv, col[:, None] * inv], axis=1)
    expand = lambda t: jnp.broadcast_to(t[:, :, None, :], (DEC_SEQ, 2, 2, n_freq)).reshape(DEC_SEQ, MLA_ROPE)
    cos = jnp.concatenate([expand(jnp.cos(ang)), jnp.ones((TB, MLA_ROPE), F32)], axis=0)
    sin = jnp.concatenate([expand(jnp.sin(ang)), jnp.zeros((TB, MLA_ROPE), F32)], axis=0)
    return cos, sin


def _rot_cols(w):
    shp = w.shape
    w4 = w.reshape(shp[:-1] + (2, 2, MLA_ROPE // 4))
    return jnp.stack([-w4[..., 1, :], w4[..., 0, :]], axis=-2).reshape(shp)


def _gla_tables():
    blk_f, blk_b, first, seq = [], [], [], []
    for s in range(N_SEQ):
        if s < BATCH:
            tiles = [s]
        else:
            base = CTX_TILES + (s - BATCH) * LAT_TILES_PER_SEQ
            tiles = list(range(base, base + LAT_TILES_PER_SEQ))
        for n, t in enumerate(tiles):
            blk_f.append(t)
            blk_b.append(tiles[len(tiles) - 1 - n])
            first.append(1 if n == 0 else 0)
            seq.append(s)
    as_i32 = lambda v: jnp.asarray(v, dtype=jnp.int32)
    return as_i32(blk_f), as_i32(blk_b), as_i32(first), as_i32(seq)


def kernel(x_prompt, x_sample, cache_ckv, cache_krope, state_gla_fwd, state_gla_bwd, c, c_ctx, w_mod, b_mod, norm_mix, norm_ffn, w_in, w_alpha, b_alpha, gla_norm, q_norm, kv_norm, w_uq, w_uk, w_uv, w_pa, w_pb, w_o, w_router, b_router, w_exp_gate, w_exp_up, w_exp_down, w_sh_gate, w_sh_up, w_sh_down, final_norm):
    x = (x_prompt.reshape(N_CTX, D_MODEL), x_sample.reshape(N_LAT, D_MODEL))

    cond = jnp.concatenate([c_ctx[None, :], c, jnp.zeros((16 - 1 - DEC_BATCH, D_MODEL), F32)], axis=0)
    mod_all = _modulation(cond, w_mod.astype(BF16), b_mod[:, None, :])
    mod_all = mod_all.reshape(DEPTH, 16, N_MOD, D_MODEL)

    cos32, sin32 = _rope_tables()
    cq_tab = jnp.tile(cos32, (1, MLA_HEADS)) * Q_SCALE
    sq_tab = jnp.tile(sin32, (1, MLA_HEADS)) * Q_SCALE

    idx256 = jnp.arange(TM)
    same_chunk = (idx256[:, None] // GLA_CHUNK) == (idx256[None, :] // GLA_CHUNK)
    tri_f = (same_chunk & (idx256[None, :] <= idx256[:, None])).astype(BF16)
    tri_b = (same_chunk & (idx256[None, :] >= idx256[:, None])).astype(BF16)
    ones_blk = same_chunk.astype(BF16)
    gla_tab = _gla_tables()
    tri_route = (idx256[:, None] < idx256[None, :]).astype(BF16)

    wr_hi = w_router.astype(BF16)
    wr_lo = (w_router - wr_hi.astype(F32)).astype(BF16)
    zpad = jnp.zeros((D_MODEL, 128 - 2 * N_EXPERTS), BF16)
    wr1 = jnp.concatenate([wr_hi, wr_lo, zpad], axis=1)
    wr2 = jnp.concatenate([wr_hi, jnp.zeros_like(wr_lo), zpad], axis=1)

    ckvs, krs, sfs, sbs = [], [], [], []
    for l in range(DEPTH):
        mod = mod_all[l]
        (w_gq, w_gk, w_gv, w_gg, w_gaf, w_gab, w_cq, w_ckv, w_kr, w_za, w_zb) = jnp.split(
            w_in[l], (256, 512, 1024, 1536, 1552, 1568, 1824, 1952, 1984, 3008), axis=1)
        w_small = jnp.concatenate([w_gaf, w_gab, w_kr, _rot_cols(w_kr), jnp.zeros((D_MODEL, 32), F32)], axis=1)
        w1 = jnp.concatenate([w_gq, w_gk, w_gv, w_gg, w_cq, w_ckv, w_small, w_za, w_zb], axis=1).astype(BF16)
        wa = jnp.zeros((128, 2 * HK), F32)
        wa = wa.at[0:GLA_GATE_RANK, 0:HK].set(w_alpha[l, 0])
        wa = wa.at[GLA_GATE_RANK:2 * GLA_GATE_RANK, HK:2 * HK].set(w_alpha[l, 1]).astype(BF16)
        ba = b_alpha[l].reshape(1, 2 * HK)
        wuq3 = w_uq[l].reshape(MLA_Q_RANK, MLA_HEADS, MLA_NOPE + MLA_ROPE)
        wuqn = wuq3[:, :, :MLA_NOPE].reshape(MLA_Q_RANK, MLA_HEADS * MLA_NOPE).astype(BF16)
        wuq_rope = wuq3[:, :, MLA_NOPE:]
        wuqr = wuq_rope.reshape(MLA_Q_RANK, MLA_HEADS * MLA_ROPE).astype(BF16)
        wuqrr = _rot_cols(wuq_rope).reshape(MLA_Q_RANK, MLA_HEADS * MLA_ROPE).astype(BF16)
        wukt = w_uk[l].reshape(MLA_KV_RANK, MLA_HEADS, MLA_NOPE).transpose(1, 2, 0).astype(BF16)
        wuvt = w_uv[l].reshape(MLA_KV_RANK, MLA_HEADS, MLA_DV).transpose(1, 2, 0).astype(BF16)

        gq, gk, gv, gg, la, qcat, ckvn, kr, za, zb = _inproj(
            x, mod, norm_mix[l][None, :], w1, wa, ba, q_norm[l][None, :], kv_norm[l][None, :],
            wuqn, wuqr, wuqrr, wukt, cq_tab, sq_tab, cos32, sin32)

        to_t = lambda s: s.transpose(0, 3, 1, 2).reshape(s.shape[0], GLA_DV, HK)
        zeros_ctx = jnp.zeros((BATCH, GLA_DV, HK), F32)
        s0f = jnp.concatenate([zeros_ctx, to_t(state_gla_fwd[:, l])], axis=0)
        s0b = jnp.concatenate([zeros_ctx, to_t(state_gla_bwd[:, l])], axis=0)
        o_f, o_r, sf, sb = _gla(gla_tab, gq, gk, gv, la, s0f, s0b, tri_f, tri_b, ones_blk)

        ckv_ctx = ckvn[:N_CTX].reshape(BATCH, SEQ, MLA_KV_RANK)
        kr_ctx = kr[:N_CTX].reshape(BATCH, SEQ, MLA_ROPE)
        ckv_lat = jnp.concatenate([cache_ckv[:, l], ckvn[N_CTX:].reshape(DEC_BATCH, DEC_SEQ, MLA_KV_RANK)], axis=1)
        kr_lat = jnp.concatenate([cache_krope[:, l], kr[N_CTX:].reshape(DEC_BATCH, DEC_SEQ, MLA_ROPE)], axis=1)

        def kv_operands(ckv_all, kr_all):
            kcat = jnp.concatenate([ckv_all, kr_all], axis=-1).astype(BF16)
            return kcat, ckv_all.astype(BF16).transpose(0, 2, 1)

        kc, vt = kv_operands(ckv_ctx, kr_ctx)
        ob_ctx = _attention(qcat, kc, vt, wuvt, tile0=0, tiles_per_seq=1, name="mla_ctx")
        kc, vt = kv_operands(ckv_lat, kr_lat)
        ob_lat = _attention(qcat, kc, vt, wuvt, tile0=CTX_TILES, tiles_per_seq=LAT_TILES_PER_SEQ, name="mla_lat")

        x1, h2x, lg = _merge(x, mod, o_f, o_r, gg, gla_norm[l][None, :], ob_ctx, ob_lat, za, zb, w_pa[l].astype(BF16), w_pb[l].astype(BF16),
                             w_o[l].astype(BF16), norm_ffn[l][None, :], wr1, wr2)
        h2x, grp, rank, counts = _route(lg.T, b_router[:, None], tri_route, h2x)

        cnt = counts[:N_GROUPS, 0].astype(jnp.int32)
        padded = (cnt + (TE - 1)) // TE * TE
        g_end = jnp.cumsum(padded)
        pos = (g_end - padded)[grp[0]] + rank[0]
        g_start = g_end - padded
        tile_start = jnp.arange(N_SLOT_TILES, dtype=jnp.int32) * TE
        tile_grp = jnp.minimum(jnp.sum(tile_start[:, None] >= g_end[None, :], axis=1), N_GROUPS - 1).astype(jnp.int32)
        n_used = (g_end[N_GROUPS - 1:] // TE).astype(jnp.int32)

        pad_lo = jnp.concatenate([g_start + cnt, g_end[N_GROUPS - 1:]]).astype(jnp.int32)
        pad_hi = jnp.concatenate([g_end, jnp.full((1,), N_SLOTS, jnp.int32)]).astype(jnp.int32)
        xs = _dispatch(pad_lo, pad_hi, pos, h2x)
        ys = _moe(tile_grp, n_used, xs, w_exp_gate[l].astype(BF16), w_exp_up[l].astype(BF16),
                  w_exp_down[l].astype(BF16).reshape(N_GROUPS, EXPERTS_PER_GROUP * EXPERT_FF, D_MODEL),
                  w_sh_gate[l].astype(BF16), w_sh_up[l].astype(BF16), w_sh_down[l].astype(BF16))
        x = _combine(pos, ys, x1, mod, final_norm[None, :], final=(l == DEPTH - 1))

        ckvs.append(ckv_ctx)
        krs.append(kr_ctx)
        from_t = lambda s: s[:BATCH].reshape(BATCH, GLA_DV, GLA_HEADS, GLA_DK).transpose(0, 2, 3, 1)
        sfs.append(from_t(sf))
        sbs.append(from_t(sb))

    y_prompt = x[0].reshape(BATCH, SEQ, D_MODEL)
    y_sample = x[1].reshape(DEC_BATCH, DEC_SEQ, D_MODEL)
    return (y_prompt, y_sample, jnp.stack(ckvs, axis=1), jnp.stack(krs, axis=1),
            jnp.stack(sfs, axis=1), jnp.stack(sbs, axis=1))
```

```python
import functools

import jax
import jax.numpy as jnp
from jax import lax
from jax.experimental import pallas as pl
from jax.experimental.pallas import tpu as pltpu

F32 = jnp.float32
BF16 = jnp.bfloat16

D_MODEL = 1024
BATCH = 16
SEQ = 256
DEPTH = 2
DEC_BATCH = 8
DEC_SEQ = 4096
PAST_LEN = 512
GRID_W = 64
GLA_HEADS = 4
GLA_DK = 64
GLA_DV = 128
GLA_GATE_RANK = 16
GLA_TAU = 16.0
GLA_CHUNK = 64
MLA_HEADS = 8
MLA_Q_RANK = 256
MLA_KV_RANK = 128
MLA_NOPE = 64
MLA_ROPE = 32
MLA_DV = 64
ROPE_BASE = 10000.0
N_EXPERTS = 16
N_GROUPS = 4
EXPERTS_PER_GROUP = 4
EXPERT_FF = 512
SHARED_FF = 512
N_MOD = 6
EPS = 1e-6

N_CTX = BATCH * SEQ
N_LAT = DEC_BATCH * DEC_SEQ
N_TOK = N_CTX + N_LAT
N_SEQ = BATCH + DEC_BATCH
TM = 256
N_TILES = N_TOK // TM
CTX_TILES = N_CTX // TM
LAT_TILES_PER_SEQ = DEC_SEQ // TM
TB = 512
N_TILES_B = N_TOK // TB
CTX_TILES_B = N_CTX // TB
LAT_TILES_B_PER_SEQ = DEC_SEQ // TB
HK = GLA_HEADS * GLA_DK
HV = GLA_HEADS * GLA_DV
QCAT = MLA_KV_RANK + MLA_ROPE
Q_SCALE =(MLA_NOPE + MLA_ROPE) ** -0.5 * 1.4426950408889634
TE = 512
RUN_ALIGN = 8
XP_ROWS = TE + N_GROUPS * RUN_ALIGN
N_SLOT_TILES = N_TOK // TE + N_GROUPS + -(-(N_TOK // TE) * N_GROUPS * (RUN_ALIGN - 1) // TE)
N_SLOTS = N_SLOT_TILES * TE
XW = D_MODEL + 128
ROUTE_W = 2048
VMEM_LIMIT = 56 * 1024 * 1024

C_GQ, C_GK, C_GV, C_GG, C_CQ, C_CKV, C_SMALL, C_ZA, C_ZB, C_END = (
    0, 256, 512, 1024, 1536, 1792, 1920, 2048, 3072, 4096)


def _sigmoid(x):
    return 1.0 / (1.0 + jnp.exp(-x))


def _rms(x, w):
    return x * lax.rsqrt(jnp.mean(x * x, axis=-1, keepdims=True) + EPS) * w


def _dot(a, b):
    return jnp.dot(a, b, preferred_element_type=F32)


def _dot_nt(a, b):
    return lax.dot_general(a, b, (((1,), (1,)), ((), ())), preferred_element_type=F32)


def _dot_tn(a, b):
    return lax.dot_general(a, b, (((0,), (0,)), ((), ())), preferred_element_type=F32)


def _tile_seq(i):
    return jnp.where(i < CTX_TILES_B, 0, 1 + (i - CTX_TILES_B) // LAT_TILES_B_PER_SEQ)


def _tile_pos(i):
    return jnp.where(i < CTX_TILES_B, LAT_TILES_B_PER_SEQ, (i - CTX_TILES_B) % LAT_TILES_B_PER_SEQ)


def _x_operands(x):
    if isinstance(x, tuple):
        xa, xb, off = x[0], x[1], 0
    else:
        xa, xb, off = x, x, CTX_TILES_B
    spec_a = pl.BlockSpec((TB, D_MODEL), lambda i: (jnp.minimum(i, CTX_TILES_B - 1), 0))
    spec_b = pl.BlockSpec((TB, D_MODEL), lambda i: (jnp.maximum(i - CTX_TILES_B, 0) + off, 0))
    return xa, xb, spec_a, spec_b


def _x_tile(xa_ref, xb_ref):
    return jnp.where(pl.program_id(0) < CTX_TILES_B, xa_ref[...], xb_ref[...])


def _mod_kernel(c_ref, w_ref, b_ref, o_ref):
    c = c_ref[...]
    sc = (c * _sigmoid(c)).astype(BF16)
    o_ref[...] = _dot(sc, w_ref[...]) + b_ref[...]


def _modulation(cond, w_mod, b_mod):
    nb = 1024
    return pl.pallas_call(
        _mod_kernel,
        out_shape=jax.ShapeDtypeStruct((DEPTH, 16, N_MOD * D_MODEL), F32),
        grid=(DEPTH, N_MOD * D_MODEL // nb),
        in_specs=[
            pl.BlockSpec((16, D_MODEL), lambda l, j: (0, 0)),
            pl.BlockSpec((None, D_MODEL, nb), lambda l, j: (l, 0, j)),
            pl.BlockSpec((None, 1, nb), lambda l, j: (l, 0, j)),
        ],
        out_specs=pl.BlockSpec((None, 16, nb), lambda l, j: (l, 0, j)),
        name="modulation",
    )(cond, w_mod, b_mod)


def _inproj_kernel(xa_ref, xb_ref, mod_ref, nw_ref, w1_ref, wa_ref, ba_ref, qnw_ref, kvnw_ref,
                   wuqn_ref, wuqr_ref, wuqrr_ref, wukt_ref, cq_ref, sq_ref, ck_ref, sk_ref,
                   gq_ref, gk_ref, gv_ref, gg_ref, la_ref, q_ref, ckv_ref, kr_ref, za_ref, zb_ref):
    x = _x_tile(xa_ref, xb_ref)
    mod = mod_ref[0]
    h = (_rms(x, nw_ref[...]) * (1.0 + mod[1:2, :]) + mod[0:1, :]).astype(BF16)

    def proj(lo, hi):
        return _dot(h, w1_ref[:, lo:hi])

    gq_ref[...] = (proj(C_GQ, C_GK) * (GLA_DK ** -0.5)).astype(BF16)
    gk_ref[...] = proj(C_GK, C_GV).astype(BF16)
    gv_ref[...] = proj(C_GV, C_GG).astype(BF16)
    gg_ref[...] = proj(C_GG, C_CQ).astype(BF16)
    za_ref[...] = _sigmoid(proj(C_ZA, C_ZB)).astype(BF16)
    zb_ref[...] = _sigmoid(proj(C_ZB, C_END)).astype(BF16)

    small = proj(C_SMALL, C_ZA)
    lin = _dot(small.astype(BF16), wa_ref[...]) + ba_ref[...]
    la_ref[...] = (jnp.minimum(lin, 0.0) - jnp.log(1.0 + jnp.exp(-jnp.abs(lin)))) * (1.0 / GLA_TAU)
    kr_ref[...] = small[:, 32:64] * ck_ref[...] + small[:, 64:96] * sk_ref[...]

    ckv_ref[...] = _rms(proj(C_CKV, C_SMALL), kvnw_ref[...])

    cqn = _rms(proj(C_CQ, C_CKV), qnw_ref[...]).astype(BF16)
    qn = _dot(cqn, wuqn_ref[...]).astype(BF16)
    qr = (_dot(cqn, wuqr_ref[...]) * cq_ref[...]
          + _dot(cqn, wuqrr_ref[...]) * sq_ref[...])
    for hd in range(MLA_HEADS):
        qa = _dot(qn[:, hd * MLA_NOPE:(hd + 1) * MLA_NOPE], wukt_ref[hd]) * Q_SCALE
        q_ref[hd, :, 0:MLA_KV_RANK] = qa.astype(BF16)
        q_ref[hd, :, MLA_KV_RANK:QCAT] = qr[:, hd * MLA_ROPE:(hd + 1) * MLA_ROPE].astype(BF16)


def _inproj(x, mod, nw, w1, wa, ba, qnw, kvnw, wuqn, wuqr, wuqrr, wukt, cq, sq, ck, sk):
    full = lambda a: pl.BlockSpec(a.shape, lambda i: (0,) * a.ndim)
    tok = lambda w: pl.BlockSpec((TB, w), lambda i: (i, 0))
    pos = lambda w: pl.BlockSpec((TB, w), lambda i: (_tile_pos(i), 0))
    xa, xb, spec_a, spec_b = _x_operands(x)
    out_shape = [
        jax.ShapeDtypeStruct((N_TOK, HK), BF16),
        jax.ShapeDtypeStruct((N_TOK, HK), BF16),
        jax.ShapeDtypeStruct((N_TOK, HV), BF16),
        jax.ShapeDtypeStruct((N_TOK, HV), BF16),
        jax.ShapeDtypeStruct((N_TOK, 2 * HK), F32),
        jax.ShapeDtypeStruct((MLA_HEADS, N_TOK, QCAT), BF16),
        jax.ShapeDtypeStruct((N_TOK, MLA_KV_RANK), F32),
        jax.ShapeDtypeStruct((N_TOK, MLA_ROPE), F32),
        jax.ShapeDtypeStruct((N_TOK, D_MODEL), BF16),
        jax.ShapeDtypeStruct((N_TOK, D_MODEL), BF16),
    ]
    out_specs = [tok(HK), tok(HK), tok(HV), tok(HV), tok(2 * HK),
                 pl.BlockSpec((MLA_HEADS, TB, QCAT), lambda i: (0, i, 0)),
                 tok(MLA_KV_RANK), tok(MLA_ROPE), tok(D_MODEL), tok(D_MODEL)]
    in_specs = [spec_a, spec_b,
                pl.BlockSpec((1, N_MOD, D_MODEL), lambda i: (_tile_seq(i), 0, 0)),
                full(nw), full(w1), full(wa), full(ba), full(qnw), full(kvnw),
                full(wuqn), full(wuqr), full(wuqrr), full(wukt),
                pos(HK), pos(HK), pos(MLA_ROPE), pos(MLA_ROPE)]
    return pl.pallas_call(
        _inproj_kernel, out_shape=out_shape, grid=(N_TILES_B,),
        in_specs=in_specs, out_specs=out_specs,
        compiler_params=pltpu.CompilerParams(dimension_semantics=("arbitrary",),
                                             vmem_limit_bytes=VMEM_LIMIT),
        name="inproj",
    )(xa, xb, mod, nw, w1, wa, ba, qnw, kvnw, wuqn, wuqr, wuqrr, wukt, cq, sq, ck, sk)


def _gla_tile(q_ref, k_ref, v_ref, la_ref, st_ref, o_ref, tri, ones_blk, *, reverse):
    n_chunks = TM // GLA_CHUNK
    la = la_ref[...]
    la_hi = la.astype(BF16)
    la_lo = (la - la_hi.astype(F32)).astype(BF16)
    cum = _dot(tri, la_hi) + _dot(tri, la_lo)
    tot = _dot(ones_blk, la_hi) + _dot(ones_blk, la_lo)
    q = q_ref[...].astype(F32)
    k = k_ref[...].astype(F32)
    qd = q * jnp.exp(cum)
    kin = (k * jnp.exp(-cum)).astype(BF16)
    kout = k * jnp.exp(tot - cum)
    decay = jnp.exp(tot)
    lane = lax.broadcasted_iota(jnp.int32, (TM, HK), 1)
    head_of_lane = lane // GLA_DK
    qd_h = [jnp.where(head_of_lane == hd, qd, 0.0).astype(BF16) for hd in range(GLA_HEADS)]
    kout_h = [jnp.where(head_of_lane == hd, kout, 0.0).astype(BF16) for hd in range(GLA_HEADS)]
    row = lax.broadcasted_iota(jnp.int32, (TM, TM), 0)
    col = lax.broadcasted_iota(jnp.int32, (TM, TM), 1)
    keep = (row // GLA_CHUNK == col // GLA_CHUNK) & ((col >= row) if reverse else (col <= row))
    head_vs = [slice(hd * GLA_DV, (hd + 1) * GLA_DV) for hd in range(GLA_HEADS)]

    att_all = _dot_nt(jnp.concatenate(qd_h, axis=0), kin)
    o_intra = [_dot(jnp.where(keep, att_all[hd * TM:(hd + 1) * TM], 0.0).astype(BF16), v_ref[:, head_vs[hd]])
               for hd in range(GLA_HEADS)]

    st = st_ref[...]
    for ci in range(n_chunks):
        c = (n_chunks - 1 - ci) if reverse else ci
        rows = slice(c * GLA_CHUNK, (c + 1) * GLA_CHUNK)
        q_stack = jnp.concatenate([qd_h[hd][rows] for hd in range(GLA_HEADS)], axis=0)
        o_inter = _dot_nt(q_stack, st.astype(BF16))
        for hd in range(GLA_HEADS):
            o = o_intra[hd][rows] + o_inter[hd * GLA_CHUNK:(hd + 1) * GLA_CHUNK]
            o_ref[rows, head_vs[hd]] = o.astype(o_ref.dtype)
        v_stack = jnp.concatenate([v_ref[rows, head_vs[hd]] for hd in range(GLA_HEADS)], axis=0)
        k_stack = jnp.concatenate([kout_h[hd][rows] for hd in range(GLA_HEADS)], axis=0)
        st = st * decay[c * GLA_CHUNK:c * GLA_CHUNK + 1, :] + _dot_tn(v_stack, k_stack)
    st_ref[...] = st


def _gla_kernel(blkf_ref, blkb_ref, first_ref, seq_ref,
                qf_ref, kf_ref, vf_ref, laf_ref, qb_ref, kb_ref, vb_ref, lab_ref, s0f_ref, s0b_ref,
                trif_ref, trib_ref, ones_ref,
                of_ref, ob_ref, sff_ref, sfb_ref, stf_ref, stb_ref):
    step = pl.program_id(0)

    @pl.when(first_ref[step] == 1)
    def _():
        stf_ref[...] = s0f_ref[0]
        stb_ref[...] = s0b_ref[0]

    _gla_tile(qf_ref, kf_ref, vf_ref, laf_ref, stf_ref, of_ref, trif_ref[...], ones_ref[...], reverse=False)
    _gla_tile(qb_ref, kb_ref, vb_ref, lab_ref, stb_ref, ob_ref, trib_ref[...], ones_ref[...], reverse=True)
    sff_ref[0] = stf_ref[...]
    sfb_ref[0] = stb_ref[...]


def _gla(tables, q, k, v, la, s0f, s0b, trif, trib, ones_blk):
    blkf, blkb, first, seq = tables
    fwd = lambda w, j=0: pl.BlockSpec((TM, w), lambda s, bf, bb, f, q_: (bf[s], j))
    bwd = lambda w, j=0: pl.BlockSpec((TM, w), lambda s, bf, bb, f, q_: (bb[s], j))
    per_seq = pl.BlockSpec((1, GLA_DV, HK), lambda s, bf, bb, f, q_: (q_[s], 0, 0))
    const = pl.BlockSpec((TM, TM), lambda s, bf, bb, f, q_: (0, 0))
    return pl.pallas_call(
        _gla_kernel,
        out_shape=[jax.ShapeDtypeStruct((N_TOK, HV), BF16), jax.ShapeDtypeStruct((N_TOK, HV), BF16),
                   jax.ShapeDtypeStruct((N_SEQ, GLA_DV, HK), F32), jax.ShapeDtypeStruct((N_SEQ, GLA_DV, HK), F32)],
        grid_spec=pltpu.PrefetchScalarGridSpec(
            num_scalar_prefetch=4, grid=(N_TILES,),
            in_specs=[fwd(HK), fwd(HK), fwd(HV), fwd(HK, 0), bwd(HK), bwd(HK), bwd(HV), bwd(HK, 1),
                      per_seq, per_seq, const, const, const],
            out_specs=[fwd(HV), bwd(HV), per_seq, per_seq],
            scratch_shapes=[pltpu.VMEM((GLA_DV, HK), F32), pltpu.VMEM((GLA_DV, HK), F32)]),
        compiler_params=pltpu.CompilerParams(dimension_semantics=("arbitrary",)),
        name="gla",
    )(blkf, blkb, first, seq, q, k, v, la, q, k, v, la, s0f, s0b, trif, trib, ones_blk)


def _attn_kernel(q_ref, k_ref, vt_ref, wuvt_ref, o_ref, s_ref, p_ref, ot_ref, vh_ref):
    @pl.when(pl.program_id(1) == 0)
    def _():
        for hd in range(MLA_HEADS):
            vh_ref[hd] = _dot(wuvt_ref[hd], vt_ref[0]).astype(BF16)

    def scores(hd):
        s = _dot_nt(k_ref[0], q_ref[hd])
        s_ref[hd % 2] = s
        return jnp.max(s, axis=0, keepdims=True)

    m = scores(0)
    for hd in range(MLA_HEADS):
        m_next = scores(hd + 1) if hd + 1 < MLA_HEADS else None
        p = jnp.exp2(s_ref[hd % 2] - m)
        l = jnp.sum(p, axis=0, keepdims=True)
        p_ref[hd % 2] = p.astype(BF16)
        ot_ref[hd] = _dot(vh_ref[hd], p_ref[hd % 2]) / l
        m = m_next
    o_ref[...] = ot_ref[...].reshape(MLA_HEADS * MLA_DV, TM).T.astype(o_ref.dtype)


def _attention(q, kcat, vt, wuvt, *, tile0, tiles_per_seq, name):
    n_seq, s_len, _ = kcat.shape
    return pl.pallas_call(
        _attn_kernel,
        out_shape=jax.ShapeDtypeStruct((n_seq * tiles_per_seq * TM, MLA_HEADS * MLA_DV), BF16),
        grid=(n_seq, tiles_per_seq),
        in_specs=[
            pl.BlockSpec((MLA_HEADS, TM, QCAT), lambda b, i: (0, tile0 + b * tiles_per_seq + i, 0)),
            pl.BlockSpec((1, s_len, QCAT), lambda b, i: (b, 0, 0)),
            pl.BlockSpec((1, MLA_KV_RANK, s_len), lambda b, i: (b, 0, 0)),
            pl.BlockSpec(wuvt.shape, lambda b, i: (0, 0, 0)),
        ],
        out_specs=pl.BlockSpec((TM, MLA_HEADS * MLA_DV), lambda b, i: (b * tiles_per_seq + i, 0)),
        scratch_shapes=[pltpu.VMEM((2, s_len, TM), F32), pltpu.VMEM((2, s_len, TM), BF16),
                        pltpu.VMEM((MLA_HEADS, MLA_DV, TM), F32),
                        pltpu.VMEM((MLA_HEADS, MLA_DV, s_len), BF16)],
        compiler_params=pltpu.CompilerParams(dimension_semantics=("arbitrary", "arbitrary"),
                                             vmem_limit_bytes=VMEM_LIMIT),
        name=name,
    )(q, kcat, vt, wuvt)


def _merge_kernel(xa_ref, xb_ref, mod_ref, of_ref, ob_ref, gg_ref, gnw_ref, obc_ref, obl_ref, za_ref, zb_ref,
                  wpa_ref, wpb_ref, wo_ref, nw_ref, wr1_ref, wr2_ref, x1_ref, h2x_ref, lg_ref):
    mod = mod_ref[0]
    o_sum = of_ref[...].astype(F32) + ob_ref[...].astype(F32)
    gate = gg_ref[...].astype(F32)
    gate = gate * _sigmoid(gate)
    oa = jnp.concatenate(
        [(_rms(o_sum[:, hd * GLA_DV:(hd + 1) * GLA_DV], gnw_ref[...])
          * gate[:, hd * GLA_DV:(hd + 1) * GLA_DV]).astype(BF16) for hd in range(GLA_HEADS)], axis=1)
    ob = jnp.where(pl.program_id(0) < CTX_TILES_B, obc_ref[...], obl_ref[...])
    y = (za_ref[...].astype(F32) * _dot(oa, wpa_ref[...])
         + zb_ref[...].astype(F32) * _dot(ob, wpb_ref[...]))
    out = _dot(y.astype(BF16), wo_ref[...])
    x1 = _x_tile(xa_ref, xb_ref) + mod[2:3, :] * out
    x1_ref[...] = x1
    h2 = _rms(x1, nw_ref[...]) * (1.0 + mod[4:5, :]) + mod[3:4, :]
    h2_hi = h2.astype(BF16)
    h2_lo = (h2 - h2_hi.astype(F32)).astype(BF16)
    h2x_ref[:, 0:D_MODEL] = h2
    h2x_ref[:, D_MODEL:XW] = jnp.zeros((TB, XW - D_MODEL), F32)
    d1 = _dot(h2_hi, wr1_ref[...])
    d2 = _dot(h2_lo, wr2_ref[...])
    lg_ref[...] = (d1[:, 0:N_EXPERTS] + d1[:, N_EXPERTS:2 * N_EXPERTS]) + d2[:, 0:N_EXPERTS]


def _merge(x, mod, o_f, o_b, gg, gnw, ob_ctx, ob_lat, za, zb, wpa, wpb, wo, nw, wr1, wr2):
    full = lambda a: pl.BlockSpec(a.shape, lambda i: (0,) * a.ndim)
    tok = lambda w: pl.BlockSpec((TB, w), lambda i: (i, 0))
    hb = MLA_HEADS * MLA_DV
    ctx_spec = pl.BlockSpec((TB, hb), lambda i: (jnp.minimum(i, CTX_TILES_B - 1), 0))
    lat_spec = pl.BlockSpec((TB, hb), lambda i: (jnp.maximum(i - CTX_TILES_B, 0), 0))
    xa, xb, spec_a, spec_b = _x_operands(x)
    return pl.pallas_call(
        _merge_kernel,
        out_shape=[jax.ShapeDtypeStruct((N_TOK, D_MODEL), F32),
                   jax.ShapeDtypeStruct((N_TOK, XW), F32),
                   jax.ShapeDtypeStruct((N_TOK, N_EXPERTS), F32)],
        grid=(N_TILES_B,),
        in_specs=[spec_a, spec_b, pl.BlockSpec((1, N_MOD, D_MODEL), lambda i: (_tile_seq(i), 0, 0)),
                  tok(HV), tok(HV), tok(HV), full(gnw), ctx_spec, lat_spec, tok(D_MODEL), tok(D_MODEL),
                  full(wpa), full(wpb), full(wo), full(nw), full(wr1), full(wr2)],
        out_specs=[tok(D_MODEL), tok(XW), tok(N_EXPERTS)],
        compiler_params=pltpu.CompilerParams(dimension_semantics=("arbitrary",),
                                             vmem_limit_bytes=VMEM_LIMIT),
        name="merge",
    )(xa, xb, mod, o_f, o_b, gg, gnw, ob_ctx, ob_lat, za, zb, wpa, wpb, wo, nw, wr1, wr2)


def _route_kernel(lg_ref, b_ref, tri_ref, h2x_in_ref, h2x_ref, grp_ref, rank_ref, cnt_ref, carry_ref):
    del h2x_in_ref
    step = pl.program_id(0)

    @pl.when(step == 0)
    def _():
        carry_ref[...] = jnp.zeros_like(carry_ref)

    aff = _sigmoid(lg_ref[...])
    biased = aff + b_ref[...]
    row = lambda a, e: a[e:e + 1, :]
    best = None
    sel = None
    for g in range(N_GROUPS):
        b = [row(biased, g * EXPERTS_PER_GROUP + i) for i in range(EXPERTS_PER_GROUP)]
        score = None
        for i in range(EXPERTS_PER_GROUP):
            for j in range(i + 1, EXPERTS_PER_GROUP):
                pair = b[i] + b[j]
                score = pair if score is None else jnp.maximum(score, pair)
        if g == 0:
            best, sel = score, jnp.zeros_like(score, dtype=jnp.int32)
        else:
            better = score > best
            best = jnp.where(better, score, best)
            sel = jnp.where(better, g, sel)
    cb, ca = [], []
    for i in range(EXPERTS_PER_GROUP):
        vb = row(biased, i)
        va = row(aff, i)
        for g in range(1, N_GROUPS):
            vb = jnp.where(sel == g, row(biased, g * EXPERTS_PER_GROUP + i), vb)
            va = jnp.where(sel == g, row(aff, g * EXPERTS_PER_GROUP + i), va)
        cb.append(vb)
        ca.append(va)
    picked = []
    for i in range(EXPERTS_PER_GROUP):
        rank = jnp.zeros_like(sel)
        for j in range(EXPERTS_PER_GROUP):
            if j == i:
                continue
            ahead = (cb[j] >= cb[i]) if j < i else (cb[j] > cb[i])
            rank = rank + ahead.astype(jnp.int32)
        picked.append(rank < 2)
    denom = None
    for i in range(EXPERTS_PER_GROUP):
        term = jnp.where(picked[i], ca[i], 0.0)
        denom = term if denom is None else denom + term
    cw = [jnp.where(picked[i], ca[i] / denom, 0.0) for i in range(EXPERTS_PER_GROUP)]
    cw_t = jnp.concatenate(cw + [jnp.zeros((128 - EXPERTS_PER_GROUP, ROUTE_W), F32)], axis=0)
    h2x_ref[...] = cw_t.T
    grp_ref[...] = sel

    onehot = jnp.concatenate([(sel == g).astype(F32) for g in range(N_GROUPS)]
                             + [jnp.zeros((8 - N_GROUPS, ROUTE_W), F32)], axis=0)
    carry = carry_ref[...]
    for c in range(ROUTE_W // 256):
        lanes = slice(c * 256, (c + 1) * 256)
        oh = onehot[:, lanes]
        before = _dot(oh.astype(BF16), tri_ref[...]) + carry
        sel_c = sel[:, lanes]
        r = before[N_GROUPS - 1:N_GROUPS, :]
        for g in range(N_GROUPS - 2, -1, -1):
            r = jnp.where(sel_c == g, before[g:g + 1, :], r)
        rank_ref[:, lanes] = r.astype(jnp.int32)
        carry = carry + jnp.sum(oh, axis=1, keepdims=True)
    carry_ref[...] = carry
    cnt_ref[...] = carry[:, 0:128]


def _route(lg_t, b_router, tri, h2x):
    return pl.pallas_call(
        _route_kernel,
        out_shape=[jax.ShapeDtypeStruct((N_TOK, XW), F32),
                   jax.ShapeDtypeStruct((1, N_TOK), jnp.int32),
                   jax.ShapeDtypeStruct((1, N_TOK), jnp.int32),
                   jax.ShapeDtypeStruct((8, 128), F32)],
        grid=(N_TOK // ROUTE_W,),
        in_specs=[pl.BlockSpec((N_EXPERTS, ROUTE_W), lambda i: (0, i)),
                  pl.BlockSpec((N_EXPERTS, 1), lambda i: (0, 0)),
                  pl.BlockSpec((256, 256), lambda i: (0, 0)),
                  pl.BlockSpec(memory_space=pl.ANY)],
        out_specs=[pl.BlockSpec((ROUTE_W, XW - D_MODEL), lambda i: (i, D_MODEL // (XW - D_MODEL))),
                   pl.BlockSpec((1, ROUTE_W), lambda i: (0, i)),
                   pl.BlockSpec((1, ROUTE_W), lambda i: (0, i)),
                   pl.BlockSpec((8, 128), lambda i: (0, 0))],
        scratch_shapes=[pltpu.VMEM((8, 256), F32)],
        input_output_aliases={3: 0},
        compiler_params=pltpu.CompilerParams(dimension_semantics=("arbitrary",)),
        name="route",
    )(lg_t, b_router, tri, h2x)


def _gather_rows_start(idx_ref, src_ref, dst_ref, sem, n_rows):
    def issue(r, c):
        pltpu.make_async_copy(src_ref.at[pl.ds(idx_ref[0, 0, r], 1), :], dst_ref.at[pl.ds(r, 1), :], sem).start()
        return c

    lax.fori_loop(0, n_rows, issue, 0, unroll=8)


def _gather_rows_wait(src_ref, dst_ref, sem, n_rows):
    def drain(r, c):
        pltpu.make_async_copy(src_ref.at[pl.ds(0, 1), :], dst_ref.at[pl.ds(r, 1), :], sem).wait()
        return c

    lax.fori_loop(0, n_rows, drain, 0, unroll=8)


def _dispatch_kernel(lo_ref, hi_ref, cnt_ref, src_ref, dst_ref, lrow_ref, x_ref, xs_ref, xp_ref, zrow_ref, sem, zsem):
    t = pl.program_id(0)
    row = lax.broadcasted_iota(jnp.int32, (XP_ROWS, TE), 0)
    perm = jnp.where(row == lrow_ref[0], 1.0, 0.0).astype(BF16)
    x = x_ref[...]
    xp_ref[:, 0:D_MODEL] = _dot(perm, x[:, 0:D_MODEL].astype(BF16))
    cw = x[:, D_MODEL:XW]
    cw_hi = cw.astype(BF16)
    cw_lo = (cw - cw_hi.astype(F32)).astype(BF16)
    xp_ref[:, D_MODEL:XW] = _dot(perm, cw_hi) + _dot(perm, cw_lo)

    def pieces(action):
        for g in range(N_GROUPS):
            n = cnt_ref[t * N_GROUPS + g]
            src0 = src_ref[t * N_GROUPS + g]
            dst0 = dst_ref[t * N_GROUPS + g]
            off = 0
            size = TE
            while size >= RUN_ALIGN:
                take = (n & size) != 0

                @pl.when(take)
                def _(off=off, size=size):
                    src = pl.multiple_of(src0 + off, RUN_ALIGN)
                    dst = pl.multiple_of(dst0 + off, RUN_ALIGN)
                    action(pltpu.make_async_copy(xp_ref.at[pl.ds(src, size), :],
                                                 xs_ref.at[pl.ds(dst, size), :], sem))

                off = off + jnp.where(take, size, 0)
                size //= 2

    pieces(lambda cp: cp.start())
    pieces(lambda cp: cp.wait())

    @pl.when(t == pl.num_programs(0) - 1)
    def _():
        zrow_ref[...] = jnp.zeros_like(zrow_ref)

        def zero_copy(slot):
            return pltpu.make_async_copy(zrow_ref.at[pl.ds(0, 1), :], xs_ref.at[pl.ds(slot, 1), :], zsem)

        def z_issue(slot, c):
            zero_copy(slot).start()
            return c

        def z_drain(slot, c):
            zero_copy(slot).wait()
            return c

        for g in range(N_GROUPS + 1):
            lax.fori_loop(lo_ref[g], hi_ref[g], z_issue, 0)
        for g in range(N_GROUPS + 1):
            lax.fori_loop(lo_ref[g], hi_ref[g], z_drain, 0)


def _dispatch(pad_lo, pad_hi, run_len, run_src, run_dst, lrow, h2x):
    n_tiles = N_TOK // TE
    return pl.pallas_call(
        _dispatch_kernel,
        out_shape=jax.ShapeDtypeStruct((N_SLOTS, XW), F32),
        grid_spec=pltpu.PrefetchScalarGridSpec(
            num_scalar_prefetch=5, grid=(n_tiles,),
            in_specs=[pl.BlockSpec((1, 1, TE), lambda t, *_: (t, 0, 0)),
                      pl.BlockSpec((TE, XW), lambda t, *_: (t, 0))],
            out_specs=pl.BlockSpec(memory_space=pl.ANY),
            scratch_shapes=[pltpu.VMEM((XP_ROWS, XW), F32), pltpu.VMEM((8, XW), F32),
                            pltpu.SemaphoreType.DMA, pltpu.SemaphoreType.DMA]),
        compiler_params=pltpu.CompilerParams(dimension_semantics=("arbitrary",)),
        name="moe_dispatch",
    )(pad_lo, pad_hi, run_len, run_src, run_dst, lrow.reshape(n_tiles, 1, TE), h2x)


def _combine_kernel(idx_ref, idx_next_ref, ys_ref, x1_ref, mod_ref, fw_ref, *rest, final):
    if final:
        yc_ref, yl_ref, rows_ref, sems = rest
    else:
        o_ref, rows_ref, sems = rest
    t = pl.program_id(0)
    slot = t % 2

    @pl.when(t == 0)
    def _():
        _gather_rows_start(idx_ref, ys_ref, rows_ref.at[0], sems.at[0], TE)

    @pl.when(t + 1 < pl.num_programs(0))
    def _():
        _gather_rows_start(idx_next_ref, ys_ref, rows_ref.at[1 - slot], sems.at[1 - slot], TE)

    _gather_rows_wait(ys_ref, rows_ref.at[slot], sems.at[slot], TE)
    x2 = x1_ref[...] + mod_ref[0][5:6, :] * rows_ref[slot]
    if not final:
        o_ref[...] = x2
        return
    y = _rms(x2, fw_ref[...])
    is_ctx = pl.program_id(0) < N_CTX // TE

    @pl.when(is_ctx)
    def _():
        yc_ref[...] = y

    @pl.when(jnp.logical_not(is_ctx))
    def _():
        yl_ref[...] = y


def _combine(pos, ys, x1, mod, fw, *, final):
    per_seq = DEC_SEQ // TE
    ctx_tiles = N_CTX // TE
    seq_of = lambda i: jnp.where(i < ctx_tiles, 0, 1 + (i - ctx_tiles) // per_seq)
    n_tiles = N_TOK // TE
    pos_tiles = pos.reshape(n_tiles, 1, TE)
    if final:
        out_shape = [jax.ShapeDtypeStruct((N_CTX, D_MODEL), F32), jax.ShapeDtypeStruct((N_LAT, D_MODEL), F32)]
        out_specs = [pl.BlockSpec((TE, D_MODEL), lambda t: (jnp.minimum(t, ctx_tiles - 1), 0)),
                     pl.BlockSpec((TE, D_MODEL), lambda t: (jnp.maximum(t - ctx_tiles, 0), 0))]
    else:
        out_shape = jax.ShapeDtypeStruct((N_TOK, D_MODEL), F32)
        out_specs = pl.BlockSpec((TE, D_MODEL), lambda t: (t, 0))
    return pl.pallas_call(
        functools.partial(_combine_kernel, final=final),
        out_shape=out_shape,
        grid=(n_tiles,),
        in_specs=[pl.BlockSpec((1, 1, TE), lambda t: (t, 0, 0), memory_space=pltpu.SMEM),
                  pl.BlockSpec((1, 1, TE), lambda t: (jnp.minimum(t + 1, n_tiles - 1), 0, 0),
                               memory_space=pltpu.SMEM),
                  pl.BlockSpec(memory_space=pl.ANY),
                  pl.BlockSpec((TE, D_MODEL), lambda t: (t, 0)),
                  pl.BlockSpec((1, N_MOD, D_MODEL), lambda t: (seq_of(t), 0, 0)),
                  pl.BlockSpec((1, D_MODEL), lambda t: (0, 0))],
        out_specs=out_specs,
        scratch_shapes=[pltpu.VMEM((2, TE, D_MODEL), F32), pltpu.SemaphoreType.DMA((2,))],
        compiler_params=pltpu.CompilerParams(dimension_semantics=("arbitrary",)),
        name="moe_combine",
    )(pos_tiles, pos_tiles, ys, x1, mod, fw)


def _moe_kernel(tg_ref, used_ref, xs_ref, wg_ref, wu_ref, wd_ref, wsg_ref, wsu_ref, wsd_ref, o_ref):
    t = pl.program_id(0)

    @pl.when(t < used_ref[0])
    def _():
        x = xs_ref[:, 0:D_MODEL].astype(BF16)
        cw = xs_ref[:, D_MODEL:XW]
        acts = []
        for j in range(EXPERTS_PER_GROUP):
            gate = _dot(x, wg_ref[j])
            up = _dot(x, wu_ref[j])
            acts.append((gate * _sigmoid(gate) * up * cw[:, j:j + 1]).astype(BF16))
        y = _dot(jnp.concatenate(acts, axis=1), wd_ref[0])
        gate = _dot(x, wsg_ref[...])
        up = _dot(x, wsu_ref[...])
        o_ref[...] = y + _dot((gate * _sigmoid(gate) * up).astype(BF16), wsd_ref[...])

    @pl.when(t >= used_ref[0])
    def _():
        o_ref[...] = jnp.zeros_like(o_ref)


def _moe(tile_grp, n_used, xs, wg, wu, wd, wsg, wsu, wsd):
    full = lambda a: pl.BlockSpec(a.shape, lambda t, tg, nu: (0,) * a.ndim)
    grp_w = lambda a: pl.BlockSpec((EXPERTS_PER_GROUP,) + a.shape[1:], lambda t, tg, nu: (tg[t], 0, 0))
    return pl.pallas_call(
        _moe_kernel,
        out_shape=jax.ShapeDtypeStruct((N_SLOTS, D_MODEL), F32),
        grid_spec=pltpu.PrefetchScalarGridSpec(
            num_scalar_prefetch=2, grid=(N_SLOT_TILES,),
            in_specs=[pl.BlockSpec((TE, XW), lambda t, tg, nu: (jnp.minimum(t, nu[0] - 1), 0)),
                      grp_w(wg), grp_w(wu),
                      pl.BlockSpec((1,) + wd.shape[1:], lambda t, tg, nu: (tg[t], 0, 0)),
                      full(wsg), full(wsu), full(wsd)],
            out_specs=pl.BlockSpec((TE, D_MODEL), lambda t, tg, nu: (t, 0))),
        compiler_params=pltpu.CompilerParams(dimension_semantics=("arbitrary",),
                                             vmem_limit_bytes=VMEM_LIMIT),
        name="moe_experts",
    )(tile_grp, n_used, xs, wg, wu, wd, wsg, wsu, wsd)


def _rope_tables():
    rows = DEC_SEQ // GRID_W
    r = jnp.repeat(jnp.arange(rows, dtype=F32), GRID_W)
    col = jnp.tile(jnp.arange(GRID_W, dtype=F32), rows)
    n_freq = MLA_ROPE // 4
    inv = ROPE_BASE ** (-jnp.arange(n_freq, dtype=F32) / n_freq)
    ang = jnp.stack([r[:, None] * inv, col[:, None] * inv], axis=1)
    expand = lambda t: jnp.broadcast_to(t[:, :, None, :], (DEC_SEQ, 2, 2, n_freq)).reshape(DEC_SEQ, MLA_ROPE)
    cos = jnp.concatenate([expand(jnp.cos(ang)), jnp.ones((TB, MLA_ROPE), F32)], axis=0)
    sin = jnp.concatenate([expand(jnp.sin(ang)), jnp.zeros((TB, MLA_ROPE), F32)], axis=0)
    return cos, sin


def _rot_cols(w):
    shp = w.shape
    w4 = w.reshape(shp[:-1] + (2, 2, MLA_ROPE // 4))
    return jnp.stack([-w4[..., 1, :], w4[..., 0, :]], axis=-2).reshape(shp)


def _gla_tables():
    blk_f, blk_b, first, seq = [], [], [], []
    for s in range(N_SEQ):
        if s < BATCH:
            tiles = [s]
        else:
            base = CTX_TILES + (s - BATCH) * LAT_TILES_PER_SEQ
            tiles = list(range(base, base + LAT_TILES_PER_SEQ))
        for n, t in enumerate(tiles):
            blk_f.append(t)
            blk_b.append(tiles[len(tiles) - 1 - n])
            first.append(1 if n == 0 else 0)
            seq.append(s)
    as_i32 = lambda v: jnp.asarray(v, dtype=jnp.int32)
    return as_i32(blk_f), as_i32(blk_b), as_i32(first), as_i32(seq)


def kernel(x_prompt, x_sample, cache_ckv, cache_krope, state_gla_fwd, state_gla_bwd, c, c_ctx, w_mod, b_mod, norm_mix, norm_ffn, w_in, w_alpha, b_alpha, gla_norm, q_norm, kv_norm, w_uq, w_uk, w_uv, w_pa, w_pb, w_o, w_router, b_router, w_exp_gate, w_exp_up, w_exp_down, w_sh_gate, w_sh_up, w_sh_down, final_norm):
    x = (x_prompt.reshape(N_CTX, D_MODEL), x_sample.reshape(N_LAT, D_MODEL))

    cond = jnp.concatenate([c_ctx[None, :], c, jnp.zeros((16 - 1 - DEC_BATCH, D_MODEL), F32)], axis=0)
    mod_all = _modulation(cond, w_mod.astype(BF16), b_mod[:, None, :])
    mod_all = mod_all.reshape(DEPTH, 16, N_MOD, D_MODEL)

    cos32, sin32 = _rope_tables()
    cq_tab = jnp.tile(cos32, (1, MLA_HEADS)) * Q_SCALE
    sq_tab = jnp.tile(sin32, (1, MLA_HEADS)) * Q_SCALE

    idx256 = jnp.arange(TM)
    same_chunk = (idx256[:, None] // GLA_CHUNK) == (idx256[None, :] // GLA_CHUNK)
    tri_f = (same_chunk & (idx256[None, :] <= idx256[:, None])).astype(BF16)
    tri_b = (same_chunk & (idx256[None, :] >= idx256[:, None])).astype(BF16)
    ones_blk = same_chunk.astype(BF16)
    gla_tab = _gla_tables()
    tri_route = (idx256[:, None] < idx256[None, :]).astype(BF16)

    wr_hi = w_router.astype(BF16)
    wr_lo = (w_router - wr_hi.astype(F32)).astype(BF16)
    zpad = jnp.zeros((D_MODEL, 128 - 2 * N_EXPERTS), BF16)
    wr1 = jnp.concatenate([wr_hi, wr_lo, zpad], axis=1)
    wr2 = jnp.concatenate([wr_hi, jnp.zeros_like(wr_lo), zpad], axis=1)

    ckvs, krs, sfs, sbs = [], [], [], []
    for l in range(DEPTH):
        mod = mod_all[l]
        (w_gq, w_gk, w_gv, w_gg, w_gaf, w_gab, w_cq, w_ckv, w_kr, w_za, w_zb) = jnp.split(
            w_in[l], (256, 512, 1024, 1536, 1552, 1568, 1824, 1952, 1984, 3008), axis=1)
        w_small = jnp.concatenate([w_gaf, w_gab, w_kr, _rot_cols(w_kr), jnp.zeros((D_MODEL, 32), F32)], axis=1)
        w1 = jnp.concatenate([w_gq, w_gk, w_gv, w_gg, w_cq, w_ckv, w_small, w_za, w_zb], axis=1).astype(BF16)
        wa = jnp.zeros((128, 2 * HK), F32)
        wa = wa.at[0:GLA_GATE_RANK, 0:HK].set(w_alpha[l, 0])
        wa = wa.at[GLA_GATE_RANK:2 * GLA_GATE_RANK, HK:2 * HK].set(w_alpha[l, 1]).astype(BF16)
        ba = b_alpha[l].reshape(1, 2 * HK)
        wuq3 = w_uq[l].reshape(MLA_Q_RANK, MLA_HEADS, MLA_NOPE + MLA_ROPE)
        wuqn = wuq3[:, :, :MLA_NOPE].reshape(MLA_Q_RANK, MLA_HEADS * MLA_NOPE).astype(BF16)
        wuq_rope = wuq3[:, :, MLA_NOPE:]
        wuqr = wuq_rope.reshape(MLA_Q_RANK, MLA_HEADS * MLA_ROPE).astype(BF16)
        wuqrr = _rot_cols(wuq_rope).reshape(MLA_Q_RANK, MLA_HEADS * MLA_ROPE).astype(BF16)
        wukt = w_uk[l].reshape(MLA_KV_RANK, MLA_HEADS, MLA_NOPE).transpose(1, 2, 0).astype(BF16)
        wuvt = w_uv[l].reshape(MLA_KV_RANK, MLA_HEADS, MLA_DV).transpose(1, 2, 0).astype(BF16)

        gq, gk, gv, gg, la, qcat, ckvn, kr, za, zb = _inproj(
            x, mod, norm_mix[l][None, :], w1, wa, ba, q_norm[l][None, :], kv_norm[l][None, :],
            wuqn, wuqr, wuqrr, wukt, cq_tab, sq_tab, cos32, sin32)

        to_t = lambda s: s.transpose(0, 3, 1, 2).reshape(s.shape[0], GLA_DV, HK)
        zeros_ctx = jnp.zeros((BATCH, GLA_DV, HK), F32)
        s0f = jnp.concatenate([zeros_ctx, to_t(state_gla_fwd[:, l])], axis=0)
        s0b = jnp.concatenate([zeros_ctx, to_t(state_gla_bwd[:, l])], axis=0)
        o_f, o_r, sf, sb = _gla(gla_tab, gq, gk, gv, la, s0f, s0b, tri_f, tri_b, ones_blk)

        ckv_ctx = ckvn[:N_CTX].reshape(BATCH, SEQ, MLA_KV_RANK)
        kr_ctx = kr[:N_CTX].reshape(BATCH, SEQ, MLA_ROPE)
        ckv_lat = jnp.concatenate([cache_ckv[:, l], ckvn[N_CTX:].reshape(DEC_BATCH, DEC_SEQ, MLA_KV_RANK)], axis=1)
        kr_lat = jnp.concatenate([cache_krope[:, l], kr[N_CTX:].reshape(DEC_BATCH, DEC_SEQ, MLA_ROPE)], axis=1)

        def kv_operands(ckv_all, kr_all):
            kcat = jnp.concatenate([ckv_all, kr_all], axis=-1).astype(BF16)
            return kcat, ckv_all.astype(BF16).transpose(0, 2, 1)

        kc, vt = kv_operands(ckv_ctx, kr_ctx)
        ob_ctx = _attention(qcat, kc, vt, wuvt, tile0=0, tiles_per_seq=1, name="mla_ctx")
        kc, vt = kv_operands(ckv_lat, kr_lat)
        ob_lat = _attention(qcat, kc, vt, wuvt, tile0=CTX_TILES, tiles_per_seq=LAT_TILES_PER_SEQ, name="mla_lat")

        x1, h2x, lg = _merge(x, mod, o_f, o_r, gg, gla_norm[l][None, :], ob_ctx, ob_lat, za, zb, w_pa[l].astype(BF16), w_pb[l].astype(BF16),
                             w_o[l].astype(BF16), norm_ffn[l][None, :], wr1, wr2)
        h2x, grp, rank, _ = _route(lg.T, b_router[:, None], tri_route, h2x)

        n_tt = N_TOK // TE
        grp_t = grp[0].reshape(n_tt, TE)
        rank_t = rank[0].reshape(n_tt, TE)
        run_len = jnp.sum(grp_t[:, :, None] == jnp.arange(N_GROUPS, dtype=jnp.int32)[None, None, :],
                          axis=1, dtype=jnp.int32)
        run_len_al = (run_len + (RUN_ALIGN - 1)) // RUN_ALIGN * RUN_ALIGN
        ranks_before = jnp.cumsum(run_len, axis=0) - run_len
        cnt = jnp.sum(run_len_al, axis=0)
        padded = (cnt + (TE - 1)) // TE * TE
        g_end = jnp.cumsum(padded)
        g_start = g_end - padded
        run_src = jnp.cumsum(run_len_al, axis=1) - run_len_al
        run_dst = g_start[None, :] + jnp.cumsum(run_len_al, axis=0) - run_len_al
        in_run = rank_t - jnp.take_along_axis(ranks_before, grp_t, axis=1)
        lrow = (jnp.take_along_axis(run_src, grp_t, axis=1) + in_run).astype(jnp.int32)
        pos = (jnp.take_along_axis(run_dst, grp_t, axis=1) + in_run).astype(jnp.int32).reshape(N_TOK)
        tile_start = jnp.arange(N_SLOT_TILES, dtype=jnp.int32) * TE
        tile_grp = jnp.minimum(jnp.sum(tile_start[:, None] >= g_end[None, :], axis=1), N_GROUPS - 1).astype(jnp.int32)
        n_used = (g_end[N_GROUPS - 1:] // TE).astype(jnp.int32)
        pad_lo = jnp.concatenate([g_start + cnt, g_end[N_GROUPS - 1:]]).astype(jnp.int32)
        pad_hi = jnp.concatenate([g_end, jnp.full((1,), N_SLOTS, jnp.int32)]).astype(jnp.int32)

        xs = _dispatch(pad_lo, pad_hi, run_len_al.reshape(-1).astype(jnp.int32), run_src.reshape(-1).astype(jnp.int32),
                       run_dst.reshape(-1).astype(jnp.int32), lrow, h2x)
        ys = _moe(tile_grp, n_used, xs, w_exp_gate[l].astype(BF16), w_exp_up[l].astype(BF16),
                  w_exp_down[l].astype(BF16).reshape(N_GROUPS, EXPERTS_PER_GROUP * EXPERT_FF, D_MODEL),
                  w_sh_gate[l].astype(BF16), w_sh_up[l].astype(BF16), w_sh_down[l].astype(BF16))
        x = _combine(pos, ys, x1, mod, final_norm[None, :], final=(l == DEPTH - 1))

        ckvs.append(ckv_ctx)
        krs.append(kr_ctx)
        from_t = lambda s: s[:BATCH].reshape(BATCH, GLA_DV, GLA_HEADS, GLA_DK).transpose(0, 2, 3, 1)
        sfs.append(from_t(sf))
        sbs.append(from_t(sb))

    y_prompt = x[0].reshape(BATCH, SEQ, D_MODEL)
    y_sample = x[1].reshape(DEC_BATCH, DEC_SEQ, D_MODEL)
    return (y_prompt, y_sample, jnp.stack(ckvs, axis=1), jnp.stack(krs, axis=1),
            jnp.stack(sfs, axis=1), jnp.stack(sbs, axis=1))
```

```python
import functools

import jax
import jax.numpy as jnp
from jax import lax
from jax.experimental import pallas as pl
from jax.experimental.pallas import tpu as pltpu

F32 = jnp.float32
BF16 = jnp.bfloat16

D_MODEL = 1024
BATCH = 16
SEQ = 256
DEPTH = 2
DEC_BATCH = 8
DEC_SEQ = 4096
PAST_LEN = 512
GRID_W = 64
GLA_HEADS = 4
GLA_DK = 64
GLA_DV = 128
GLA_GATE_RANK = 16
GLA_TAU = 16.0
GLA_CHUNK = 64
MLA_HEADS = 8
MLA_Q_RANK = 256
MLA_KV_RANK = 128
MLA_NOPE = 64
MLA_ROPE = 32
MLA_DV = 64
ROPE_BASE = 10000.0
N_EXPERTS = 16
N_GROUPS = 4
EXPERTS_PER_GROUP = 4
EXPERT_FF = 512
SHARED_FF = 512
N_MOD = 6
EPS = 1e-6

N_CTX = BATCH * SEQ
N_LAT = DEC_BATCH * DEC_SEQ
N_TOK = N_CTX + N_LAT
N_SEQ = BATCH + DEC_BATCH
TM = 256
N_TILES = N_TOK // TM
CTX_TILES = N_CTX // TM
LAT_TILES_PER_SEQ = DEC_SEQ // TM
TB = 512
N_TILES_B = N_TOK // TB
CTX_TILES_B = N_CTX // TB
LAT_TILES_B_PER_SEQ = DEC_SEQ // TB
HK = GLA_HEADS * GLA_DK
HV = GLA_HEADS * GLA_DV
QCAT = MLA_KV_RANK + MLA_ROPE
Q_SCALE =(MLA_NOPE + MLA_ROPE) ** -0.5 * 1.4426950408889634
TE = 512
RUN_ALIGN = 8
XP_ROWS = TE + N_GROUPS * RUN_ALIGN
N_SLOT_TILES = N_TOK // TE + N_GROUPS + -(-(N_TOK // TE) * N_GROUPS * (RUN_ALIGN - 1) // TE)
N_SLOTS = N_SLOT_TILES * TE
XW = D_MODEL + 128
ROUTE_W = 2048
VMEM_LIMIT = 56 * 1024 * 1024

C_GQ, C_GK, C_GV, C_GG, C_CQ, C_CKV, C_SMALL, C_ZA, C_ZB, C_END = (
    0, 256, 512, 1024, 1536, 1792, 1920, 2048, 3072, 4096)


def _sigmoid(x):
    return 1.0 / (1.0 + jnp.exp(-x))


def _rms(x, w):
    return x * lax.rsqrt(jnp.mean(x * x, axis=-1, keepdims=True) + EPS) * w


def _dot(a, b):
    return jnp.dot(a, b, preferred_element_type=F32)


def _dot_nt(a, b):
    return lax.dot_general(a, b, (((1,), (1,)), ((), ())), preferred_element_type=F32)


def _dot_tn(a, b):
    return lax.dot_general(a, b, (((0,), (0,)), ((), ())), preferred_element_type=F32)


def _tile_seq(i):
    return jnp.where(i < CTX_TILES_B, 0, 1 + (i - CTX_TILES_B) // LAT_TILES_B_PER_SEQ)


def _tile_pos(i):
    return jnp.where(i < CTX_TILES_B, LAT_TILES_B_PER_SEQ, (i - CTX_TILES_B) % LAT_TILES_B_PER_SEQ)


def _x_operands(x):
    if isinstance(x, tuple):
        xa, xb, off = x[0], x[1], 0
    else:
        xa, xb, off = x, x, CTX_TILES_B
    spec_a = pl.BlockSpec((TB, D_MODEL), lambda i: (jnp.minimum(i, CTX_TILES_B - 1), 0))
    spec_b = pl.BlockSpec((TB, D_MODEL), lambda i: (jnp.maximum(i - CTX_TILES_B, 0) + off, 0))
    return xa, xb, spec_a, spec_b


def _x_tile(xa_ref, xb_ref):
    return jnp.where(pl.program_id(0) < CTX_TILES_B, xa_ref[...], xb_ref[...])


def _mod_kernel(c_ref, w_ref, b_ref, o_ref):
    c = c_ref[...]
    sc = (c * _sigmoid(c)).astype(BF16)
    o_ref[...] = _dot(sc, w_ref[...]) + b_ref[...]


def _modulation(cond, w_mod, b_mod):
    nb = 1024
    return pl.pallas_call(
        _mod_kernel,
        out_shape=jax.ShapeDtypeStruct((DEPTH, 16, N_MOD * D_MODEL), F32),
        grid=(DEPTH, N_MOD * D_MODEL // nb),
        in_specs=[
            pl.BlockSpec((16, D_MODEL), lambda l, j: (0, 0)),
            pl.BlockSpec((None, D_MODEL, nb), lambda l, j: (l, 0, j)),
            pl.BlockSpec((None, 1, nb), lambda l, j: (l, 0, j)),
        ],
        out_specs=pl.BlockSpec((None, 16, nb), lambda l, j: (l, 0, j)),
        name="modulation",
    )(cond, w_mod, b_mod)


def _inproj_kernel(xa_ref, xb_ref, mod_ref, nw_ref, w1_ref, wa_ref, ba_ref, qnw_ref, kvnw_ref,
                   wuqn_ref, wuqr_ref, wuqrr_ref, wukt_ref, cq_ref, sq_ref, ck_ref, sk_ref,
                   gq_ref, gk_ref, gv_ref, gg_ref, la_ref, q_ref, ckv_ref, kr_ref, za_ref, zb_ref):
    x = _x_tile(xa_ref, xb_ref)
    mod = mod_ref[0]
    h = (_rms(x, nw_ref[...]) * (1.0 + mod[1:2, :]) + mod[0:1, :]).astype(BF16)

    def proj(lo, hi):
        return _dot(h, w1_ref[:, lo:hi])

    gq_ref[...] = (proj(C_GQ, C_GK) * (GLA_DK ** -0.5)).astype(BF16)
    gk_ref[...] = proj(C_GK, C_GV).astype(BF16)
    gv_ref[...] = proj(C_GV, C_GG).astype(BF16)
    gg_ref[...] = proj(C_GG, C_CQ).astype(BF16)
    za_ref[...] = _sigmoid(proj(C_ZA, C_ZB)).astype(BF16)
    zb_ref[...] = _sigmoid(proj(C_ZB, C_END)).astype(BF16)

    small = proj(C_SMALL, C_ZA)
    lin = _dot(small.astype(BF16), wa_ref[...]) + ba_ref[...]
    la_ref[...] = (jnp.minimum(lin, 0.0) - jnp.log(1.0 + jnp.exp(-jnp.abs(lin)))) * (1.0 / GLA_TAU)
    kr_ref[...] = small[:, 32:64] * ck_ref[...] + small[:, 64:96] * sk_ref[...]

    ckv_ref[...] = _rms(proj(C_CKV, C_SMALL), kvnw_ref[...])

    cqn = _rms(proj(C_CQ, C_CKV), qnw_ref[...]).astype(BF16)
    qn = _dot(cqn, wuqn_ref[...]).astype(BF16)
    qr = (_dot(cqn, wuqr_ref[...]) * cq_ref[...]
          + _dot(cqn, wuqrr_ref[...]) * sq_ref[...])
    for hd in range(MLA_HEADS):
        qa = _dot(qn[:, hd * MLA_NOPE:(hd + 1) * MLA_NOPE], wukt_ref[hd]) * Q_SCALE
        q_ref[hd, :, 0:MLA_KV_RANK] = qa.astype(BF16)
        q_ref[hd, :, MLA_KV_RANK:QCAT] = qr[:, hd * MLA_ROPE:(hd + 1) * MLA_ROPE].astype(BF16)


def _inproj(x, mod, nw, w1, wa, ba, qnw, kvnw, wuqn, wuqr, wuqrr, wukt, cq, sq, ck, sk):
    full = lambda a: pl.BlockSpec(a.shape, lambda i: (0,) * a.ndim)
    tok = lambda w: pl.BlockSpec((TB, w), lambda i: (i, 0))
    pos = lambda w: pl.BlockSpec((TB, w), lambda i: (_tile_pos(i), 0))
    xa, xb, spec_a, spec_b = _x_operands(x)
    out_shape = [
        jax.ShapeDtypeStruct((N_TOK, HK), BF16),
        jax.ShapeDtypeStruct((N_TOK, HK), BF16),
        jax.ShapeDtypeStruct((N_TOK, HV), BF16),
        jax.ShapeDtypeStruct((N_TOK, HV), BF16),
        jax.ShapeDtypeStruct((N_TOK, 2 * HK), F32),
        jax.ShapeDtypeStruct((MLA_HEADS, N_TOK, QCAT), BF16),
        jax.ShapeDtypeStruct((N_TOK, MLA_KV_RANK), F32),
        jax.ShapeDtypeStruct((N_TOK, MLA_ROPE), F32),
        jax.ShapeDtypeStruct((N_TOK, D_MODEL), BF16),
        jax.ShapeDtypeStruct((N_TOK, D_MODEL), BF16),
    ]
    out_specs = [tok(HK), tok(HK), tok(HV), tok(HV), tok(2 * HK),
                 pl.BlockSpec((MLA_HEADS, TB, QCAT), lambda i: (0, i, 0)),
                 tok(MLA_KV_RANK), tok(MLA_ROPE), tok(D_MODEL), tok(D_MODEL)]
    in_specs = [spec_a, spec_b,
                pl.BlockSpec((1, N_MOD, D_MODEL), lambda i: (_tile_seq(i), 0, 0)),
                full(nw), full(w1), full(wa), full(ba), full(qnw), full(kvnw),
                full(wuqn), full(wuqr), full(wuqrr), full(wukt),
                pos(HK), pos(HK), pos(MLA_ROPE), pos(MLA_ROPE)]
    return pl.pallas_call(
        _inproj_kernel, out_shape=out_shape, grid=(N_TILES_B,),
        in_specs=in_specs, out_specs=out_specs,
        compiler_params=pltpu.CompilerParams(dimension_semantics=("arbitrary",),
                                             vmem_limit_bytes=VMEM_LIMIT),
        name="inproj",
    )(xa, xb, mod, nw, w1, wa, ba, qnw, kvnw, wuqn, wuqr, wuqrr, wukt, cq, sq, ck, sk)


def _gla_tile(q_ref, k_ref, v_ref, la_ref, st_ref, o_ref, tri, ones_blk, *, reverse):
    n_chunks = TM // GLA_CHUNK
    la = la_ref[...]
    la_hi = la.astype(BF16)
    la_lo = (la - la_hi.astype(F32)).astype(BF16)
    cum = _dot(tri, la_hi) + _dot(tri, la_lo)
    tot = _dot(ones_blk, la_hi) + _dot(ones_blk, la_lo)
    q = q_ref[...].astype(F32)
    k = k_ref[...].astype(F32)
    qd = q * jnp.exp(cum)
    kin = (k * jnp.exp(-cum)).astype(BF16)
    kout = k * jnp.exp(tot - cum)
    decay = jnp.exp(tot)
    lane = lax.broadcasted_iota(jnp.int32, (TM, HK), 1)
    head_of_lane = lane // GLA_DK
    qd_h = [jnp.where(head_of_lane == hd, qd, 0.0).astype(BF16) for hd in range(GLA_HEADS)]
    kout_h = [jnp.where(head_of_lane == hd, kout, 0.0).astype(BF16) for hd in range(GLA_HEADS)]
    row = lax.broadcasted_iota(jnp.int32, (TM, TM), 0)
    col = lax.broadcasted_iota(jnp.int32, (TM, TM), 1)
    keep = (row // GLA_CHUNK == col // GLA_CHUNK) & ((col >= row) if reverse else (col <= row))
    head_vs = [slice(hd * GLA_DV, (hd + 1) * GLA_DV) for hd in range(GLA_HEADS)]

    att_all = _dot_nt(jnp.concatenate(qd_h, axis=0), kin)
    o_intra = [_dot(jnp.where(keep, att_all[hd * TM:(hd + 1) * TM], 0.0).astype(BF16), v_ref[:, head_vs[hd]])
               for hd in range(GLA_HEADS)]

    st = st_ref[...]
    for ci in range(n_chunks):
        c = (n_chunks - 1 - ci) if reverse else ci
        rows = slice(c * GLA_CHUNK, (c + 1) * GLA_CHUNK)
        q_stack = jnp.concatenate([qd_h[hd][rows] for hd in range(GLA_HEADS)], axis=0)
        o_inter = _dot_nt(q_stack, st.astype(BF16))
        for hd in range(GLA_HEADS):
            o = o_intra[hd][rows] + o_inter[hd * GLA_CHUNK:(hd + 1) * GLA_CHUNK]
            o_ref[rows, head_vs[hd]] = o.astype(o_ref.dtype)
        v_stack = jnp.concatenate([v_ref[rows, head_vs[hd]] for hd in range(GLA_HEADS)], axis=0)
        k_stack = jnp.concatenate([kout_h[hd][rows] for hd in range(GLA_HEADS)], axis=0)
        st = st * decay[c * GLA_CHUNK:c * GLA_CHUNK + 1, :] + _dot_tn(v_stack, k_stack)
    st_ref[...] = st


def _gla_kernel(blkf_ref, blkb_ref, first_ref, seq_ref,
                qf_ref, kf_ref, vf_ref, laf_ref, qb_ref, kb_ref, vb_ref, lab_ref, s0f_ref, s0b_ref,
                trif_ref, trib_ref, ones_ref,
                of_ref, ob_ref, sff_ref, sfb_ref, stf_ref, stb_ref):
    step = pl.program_id(0)

    @pl.when(first_ref[step] == 1)
    def _():
        stf_ref[...] = s0f_ref[0]
        stb_ref[...] = s0b_ref[0]

    _gla_tile(qf_ref, kf_ref, vf_ref, laf_ref, stf_ref, of_ref, trif_ref[...], ones_ref[...], reverse=False)
    _gla_tile(qb_ref, kb_ref, vb_ref, lab_ref, stb_ref, ob_ref, trib_ref[...], ones_ref[...], reverse=True)
    sff_ref[0] = stf_ref[...]
    sfb_ref[0] = stb_ref[...]


def _gla(tables, q, k, v, la, s0f, s0b, trif, trib, ones_blk):
    blkf, blkb, first, seq = tables
    fwd = lambda w, j=0: pl.BlockSpec((TM, w), lambda s, bf, bb, f, q_: (bf[s], j))
    bwd = lambda w, j=0: pl.BlockSpec((TM, w), lambda s, bf, bb, f, q_: (bb[s], j))
    per_seq = pl.BlockSpec((1, GLA_DV, HK), lambda s, bf, bb, f, q_: (q_[s], 0, 0))
    const = pl.BlockSpec((TM, TM), lambda s, bf, bb, f, q_: (0, 0))
    return pl.pallas_call(
        _gla_kernel,
        out_shape=[jax.ShapeDtypeStruct((N_TOK, HV), BF16), jax.ShapeDtypeStruct((N_TOK, HV), BF16),
                   jax.ShapeDtypeStruct((N_SEQ, GLA_DV, HK), F32), jax.ShapeDtypeStruct((N_SEQ, GLA_DV, HK), F32)],
        grid_spec=pltpu.PrefetchScalarGridSpec(
            num_scalar_prefetch=4, grid=(N_TILES,),
            in_specs=[fwd(HK), fwd(HK), fwd(HV), fwd(HK, 0), bwd(HK), bwd(HK), bwd(HV), bwd(HK, 1),
                      per_seq, per_seq, const, const, const],
            out_specs=[fwd(HV), bwd(HV), per_seq, per_seq],
            scratch_shapes=[pltpu.VMEM((GLA_DV, HK), F32), pltpu.VMEM((GLA_DV, HK), F32)]),
        compiler_params=pltpu.CompilerParams(dimension_semantics=("arbitrary",)),
        name="gla",
    )(blkf, blkb, first, seq, q, k, v, la, q, k, v, la, s0f, s0b, trif, trib, ones_blk)


def _attn_kernel(q_ref, k_ref, vt_ref, wuvt_ref, o_ref, s_ref, p_ref, ot_ref, vh_ref):
    @pl.when(pl.program_id(1) == 0)
    def _():
        for hd in range(MLA_HEADS):
            vh_ref[hd] = _dot(wuvt_ref[hd], vt_ref[0]).astype(BF16)

    def scores(hd):
        s = _dot_nt(k_ref[0], q_ref[hd])
        s_ref[hd % 2] = s
        return jnp.max(s, axis=0, keepdims=True)

    m = scores(0)
    for hd in range(MLA_HEADS):
        m_next = scores(hd + 1) if hd + 1 < MLA_HEADS else None
        p = jnp.exp2(s_ref[hd % 2] - m)
        l = jnp.sum(p, axis=0, keepdims=True)
        p_ref[hd % 2] = p.astype(BF16)
        ot_ref[hd] = _dot(vh_ref[hd], p_ref[hd % 2]) / l
        m = m_next
    o_ref[...] = ot_ref[...].reshape(MLA_HEADS * MLA_DV, TM).T.astype(o_ref.dtype)


def _attention(q, kcat, vt, wuvt, *, tile0, tiles_per_seq, name):
    n_seq, s_len, _ = kcat.shape
    return pl.pallas_call(
        _attn_kernel,
        out_shape=jax.ShapeDtypeStruct((n_seq * tiles_per_seq * TM, MLA_HEADS * MLA_DV), BF16),
        grid=(n_seq, tiles_per_seq),
        in_specs=[
            pl.BlockSpec((MLA_HEADS, TM, QCAT), lambda b, i: (0, tile0 + b * tiles_per_seq + i, 0)),
            pl.BlockSpec((1, s_len, QCAT), lambda b, i: (b, 0, 0)),
            pl.BlockSpec((1, MLA_KV_RANK, s_len), lambda b, i: (b, 0, 0)),
            pl.BlockSpec(wuvt.shape, lambda b, i: (0, 0, 0)),
        ],
        out_specs=pl.BlockSpec((TM, MLA_HEADS * MLA_DV), lambda b, i: (b * tiles_per_seq + i, 0)),
        scratch_shapes=[pltpu.VMEM((2, s_len, TM), F32), pltpu.VMEM((2, s_len, TM), BF16),
                        pltpu.VMEM((MLA_HEADS, MLA_DV, TM), F32),
                        pltpu.VMEM((MLA_HEADS, MLA_DV, s_len), BF16)],
        compiler_params=pltpu.CompilerParams(dimension_semantics=("arbitrary", "arbitrary"),
                                             vmem_limit_bytes=VMEM_LIMIT),
        name=name,
    )(q, kcat, vt, wuvt)


def _merge_kernel(xa_ref, xb_ref, mod_ref, of_ref, ob_ref, gg_ref, gnw_ref, obc_ref, obl_ref, za_ref, zb_ref,
                  wpa_ref, wpb_ref, wo_ref, nw_ref, wr1_ref, wr2_ref, x1_ref, h2x_ref, lg_ref):
    mod = mod_ref[0]
    o_sum = of_ref[...].astype(F32) + ob_ref[...].astype(F32)
    gate = gg_ref[...].astype(F32)
    gate = gate * _sigmoid(gate)
    oa = jnp.concatenate(
        [(_rms(o_sum[:, hd * GLA_DV:(hd + 1) * GLA_DV], gnw_ref[...])
          * gate[:, hd * GLA_DV:(hd + 1) * GLA_DV]).astype(BF16) for hd in range(GLA_HEADS)], axis=1)
    ob = jnp.where(pl.program_id(0) < CTX_TILES_B, obc_ref[...], obl_ref[...])
    y = (za_ref[...].astype(F32) * _dot(oa, wpa_ref[...])
         + zb_ref[...].astype(F32) * _dot(ob, wpb_ref[...]))
    out = _dot(y.astype(BF16), wo_ref[...])
    x1 = _x_tile(xa_ref, xb_ref) + mod[2:3, :] * out
    x1_ref[...] = x1
    h2 = _rms(x1, nw_ref[...]) * (1.0 + mod[4:5, :]) + mod[3:4, :]
    h2_hi = h2.astype(BF16)
    h2_lo = (h2 - h2_hi.astype(F32)).astype(BF16)
    h2x_ref[:, 0:D_MODEL] = h2
    h2x_ref[:, D_MODEL:XW] = jnp.zeros((TB, XW - D_MODEL), F32)
    d1 = _dot(h2_hi, wr1_ref[...])
    d2 = _dot(h2_lo, wr2_ref[...])
    lg_ref[...] = (d1[:, 0:N_EXPERTS] + d1[:, N_EXPERTS:2 * N_EXPERTS]) + d2[:, 0:N_EXPERTS]


def _merge(x, mod, o_f, o_b, gg, gnw, ob_ctx, ob_lat, za, zb, wpa, wpb, wo, nw, wr1, wr2):
    full = lambda a: pl.BlockSpec(a.shape, lambda i: (0,) * a.ndim)
    tok = lambda w: pl.BlockSpec((TB, w), lambda i: (i, 0))
    hb = MLA_HEADS * MLA_DV
    ctx_spec = pl.BlockSpec((TB, hb), lambda i: (jnp.minimum(i, CTX_TILES_B - 1), 0))
    lat_spec = pl.BlockSpec((TB, hb), lambda i: (jnp.maximum(i - CTX_TILES_B, 0), 0))
    xa, xb, spec_a, spec_b = _x_operands(x)
    return pl.pallas_call(
        _merge_kernel,
        out_shape=[jax.ShapeDtypeStruct((N_TOK, D_MODEL), F32),
                   jax.ShapeDtypeStruct((N_TOK, XW), F32),
                   jax.ShapeDtypeStruct((N_TOK, N_EXPERTS), F32)],
        grid=(N_TILES_B,),
        in_specs=[spec_a, spec_b, pl.BlockSpec((1, N_MOD, D_MODEL), lambda i: (_tile_seq(i), 0, 0)),
                  tok(HV), tok(HV), tok(HV), full(gnw), ctx_spec, lat_spec, tok(D_MODEL), tok(D_MODEL),
                  full(wpa), full(wpb), full(wo), full(nw), full(wr1), full(wr2)],
        out_specs=[tok(D_MODEL), tok(XW), tok(N_EXPERTS)],
        compiler_params=pltpu.CompilerParams(dimension_semantics=("arbitrary",),
                                             vmem_limit_bytes=VMEM_LIMIT),
        name="merge",
    )(xa, xb, mod, o_f, o_b, gg, gnw, ob_ctx, ob_lat, za, zb, wpa, wpb, wo, nw, wr1, wr2)


def _route_kernel(lg_ref, b_ref, tri_ref, h2x_in_ref, h2x_ref, grp_ref, rank_ref, cnt_ref, carry_ref):
    del h2x_in_ref
    step = pl.program_id(0)

    @pl.when(step == 0)
    def _():
        carry_ref[...] = jnp.zeros_like(carry_ref)

    aff = _sigmoid(lg_ref[...])
    biased = aff + b_ref[...]
    row = lambda a, e: a[e:e + 1, :]
    best = None
    sel = None
    for g in range(N_GROUPS):
        b = [row(biased, g * EXPERTS_PER_GROUP + i) for i in range(EXPERTS_PER_GROUP)]
        score = None
        for i in range(EXPERTS_PER_GROUP):
            for j in range(i + 1, EXPERTS_PER_GROUP):
                pair = b[i] + b[j]
                score = pair if score is None else jnp.maximum(score, pair)
        if g == 0:
            best, sel = score, jnp.zeros_like(score, dtype=jnp.int32)
        else:
            better = score > best
            best = jnp.where(better, score, best)
            sel = jnp.where(better, g, sel)
    cb, ca = [], []
    for i in range(EXPERTS_PER_GROUP):
        vb = row(biased, i)
        va = row(aff, i)
        for g in range(1, N_GROUPS):
            vb = jnp.where(sel == g, row(biased, g * EXPERTS_PER_GROUP + i), vb)
            va = jnp.where(sel == g, row(aff, g * EXPERTS_PER_GROUP + i), va)
        cb.append(vb)
        ca.append(va)
    picked = []
    for i in range(EXPERTS_PER_GROUP):
        rank = jnp.zeros_like(sel)
        for j in range(EXPERTS_PER_GROUP):
            if j == i:
                continue
            ahead = (cb[j] >= cb[i]) if j < i else (cb[j] > cb[i])
            rank = rank + ahead.astype(jnp.int32)
        picked.append(rank < 2)
    denom = None
    for i in range(EXPERTS_PER_GROUP):
        term = jnp.where(picked[i], ca[i], 0.0)
        denom = term if denom is None else denom + term
    cw = [jnp.where(picked[i], ca[i] / denom, 0.0) for i in range(EXPERTS_PER_GROUP)]
    cw_t = jnp.concatenate(cw + [jnp.zeros((128 - EXPERTS_PER_GROUP, ROUTE_W), F32)], axis=0)
    h2x_ref[...] = cw_t.T
    grp_ref[...] = sel

    onehot = jnp.concatenate([(sel == g).astype(F32) for g in range(N_GROUPS)]
                             + [jnp.zeros((8 - N_GROUPS, ROUTE_W), F32)], axis=0)
    carry = carry_ref[...]
    for c in range(ROUTE_W // 256):
        lanes = slice(c * 256, (c + 1) * 256)
        oh = onehot[:, lanes]
        before = _dot(oh.astype(BF16), tri_ref[...]) + carry
        sel_c = sel[:, lanes]
        r = before[N_GROUPS - 1:N_GROUPS, :]
        for g in range(N_GROUPS - 2, -1, -1):
            r = jnp.where(sel_c == g, before[g:g + 1, :], r)
        rank_ref[:, lanes] = r.astype(jnp.int32)
        carry = carry + jnp.sum(oh, axis=1, keepdims=True)
    carry_ref[...] = carry
    cnt_ref[...] = carry[:, 0:128]


def _route(lg_t, b_router, tri, h2x):
    return pl.pallas_call(
        _route_kernel,
        out_shape=[jax.ShapeDtypeStruct((N_TOK, XW), F32),
                   jax.ShapeDtypeStruct((1, N_TOK), jnp.int32),
                   jax.ShapeDtypeStruct((1, N_TOK), jnp.int32),
                   jax.ShapeDtypeStruct((8, 128), F32)],
        grid=(N_TOK // ROUTE_W,),
        in_specs=[pl.BlockSpec((N_EXPERTS, ROUTE_W), lambda i: (0, i)),
                  pl.BlockSpec((N_EXPERTS, 1), lambda i: (0, 0)),
                  pl.BlockSpec((256, 256), lambda i: (0, 0)),
                  pl.BlockSpec(memory_space=pl.ANY)],
        out_specs=[pl.BlockSpec((ROUTE_W, XW - D_MODEL), lambda i: (i, D_MODEL // (XW - D_MODEL))),
                   pl.BlockSpec((1, ROUTE_W), lambda i: (0, i)),
                   pl.BlockSpec((1, ROUTE_W), lambda i: (0, i)),
                   pl.BlockSpec((8, 128), lambda i: (0, 0))],
        scratch_shapes=[pltpu.VMEM((8, 256), F32)],
        input_output_aliases={3: 0},
        compiler_params=pltpu.CompilerParams(dimension_semantics=("arbitrary",)),
        name="route",
    )(lg_t, b_router, tri, h2x)


def _dispatch_kernel(lo_ref, hi_ref, cnt_ref, src_ref, dst_ref, lrow_ref, x_ref, xs_ref, xp_ref, zrow_ref, sem, zsem):
    t = pl.program_id(0)
    row = lax.broadcasted_iota(jnp.int32, (XP_ROWS, TE), 0)
    perm = jnp.where(row == lrow_ref[0], 1.0, 0.0).astype(BF16)
    x = x_ref[...]
    xp_ref[:, 0:D_MODEL] = _dot(perm, x[:, 0:D_MODEL].astype(BF16))
    cw = x[:, D_MODEL:XW]
    cw_hi = cw.astype(BF16)
    cw_lo = (cw - cw_hi.astype(F32)).astype(BF16)
    xp_ref[:, D_MODEL:XW] = _dot(perm, cw_hi) + _dot(perm, cw_lo)

    def make_copy(row, slot_row, size):
        return pltpu.make_async_copy(xp_ref.at[pl.ds(row, size), :], xs_ref.at[pl.ds(slot_row, size), :], sem)

    _run_pieces(cnt_ref, src_ref, dst_ref, t, make_copy, lambda cp: cp.start())
    _run_pieces(cnt_ref, src_ref, dst_ref, t, make_copy, lambda cp: cp.wait())

    @pl.when(t == pl.num_programs(0) - 1)
    def _():
        zrow_ref[...] = jnp.zeros_like(zrow_ref)

        def zero_copy(slot):
            return pltpu.make_async_copy(zrow_ref.at[pl.ds(0, 1), :], xs_ref.at[pl.ds(slot, 1), :], zsem)

        def z_issue(slot, c):
            zero_copy(slot).start()
            return c

        def z_drain(slot, c):
            zero_copy(slot).wait()
            return c

        for g in range(N_GROUPS + 1):
            lax.fori_loop(lo_ref[g], hi_ref[g], z_issue, 0)
        for g in range(N_GROUPS + 1):
            lax.fori_loop(lo_ref[g], hi_ref[g], z_drain, 0)


def _dispatch(pad_lo, pad_hi, run_len, run_src, run_dst, lrow, h2x):
    n_tiles = N_TOK // TE
    return pl.pallas_call(
        _dispatch_kernel,
        out_shape=jax.ShapeDtypeStruct((N_SLOTS, XW), F32),
        grid_spec=pltpu.PrefetchScalarGridSpec(
            num_scalar_prefetch=5, grid=(n_tiles,),
            in_specs=[pl.BlockSpec((1, 1, TE), lambda t, *_: (t, 0, 0)),
                      pl.BlockSpec((TE, XW), lambda t, *_: (t, 0))],
            out_specs=pl.BlockSpec(memory_space=pl.ANY),
            scratch_shapes=[pltpu.VMEM((XP_ROWS, XW), F32), pltpu.VMEM((8, XW), F32),
                            pltpu.SemaphoreType.DMA, pltpu.SemaphoreType.DMA]),
        compiler_params=pltpu.CompilerParams(dimension_semantics=("arbitrary",)),
        name="moe_dispatch",
    )(pad_lo, pad_hi, run_len, run_src, run_dst, lrow.reshape(n_tiles, 1, TE), h2x)


def _run_pieces(len_ref, a_ref, b_ref, tile, make_copy, action):
    for g in range(N_GROUPS):
        n = len_ref[tile * N_GROUPS + g]
        a0 = a_ref[tile * N_GROUPS + g]
        b0 = b_ref[tile * N_GROUPS + g]
        off = 0
        size = TE
        while size >= RUN_ALIGN:
            take = (n & size) != 0

            @pl.when(take)
            def _(off=off, size=size):
                action(make_copy(pl.multiple_of(a0 + off, RUN_ALIGN), pl.multiple_of(b0 + off, RUN_ALIGN), size))

            off = off + jnp.where(take, size, 0)
            size //= 2


def _combine_kernel(len_ref, src_ref, dst_ref, lrow_ref, ys_ref, x1_ref, mod_ref, fw_ref, *rest, final):
    if final:
        yc_ref, yl_ref, ybuf_ref, sems = rest
    else:
        o_ref, ybuf_ref, sems = rest
    t = pl.program_id(0)
    slot = t % 2

    def fetch(tile, buf, action):
        def make_copy(row, slot_row, size):
            return pltpu.make_async_copy(ys_ref.at[pl.ds(slot_row, size), :],
                                         ybuf_ref.at[buf, pl.ds(row, size), :], sems.at[buf])
        _run_pieces(len_ref, src_ref, dst_ref, tile, make_copy, action)

    @pl.when(t == 0)
    def _():
        ybuf_ref[...] = jnp.zeros_like(ybuf_ref)
        fetch(t, 0, lambda cp: cp.start())

    @pl.when(t + 1 < pl.num_programs(0))
    def _():
        fetch(t + 1, 1 - slot, lambda cp: cp.start())

    fetch(t, slot, lambda cp: cp.wait())
    row = lax.broadcasted_iota(jnp.int32, (XP_ROWS, TE), 0)
    perm = jnp.where(row == lrow_ref[0], 1.0, 0.0).astype(BF16)
    yb = ybuf_ref[slot]
    y_hi = yb.astype(BF16)
    y_lo = (yb - y_hi.astype(F32)).astype(BF16)
    rows = _dot_tn(perm, y_hi) + _dot_tn(perm, y_lo)
    x2 = x1_ref[...] + mod_ref[0][5:6, :] * rows
    if not final:
        o_ref[...] = x2
        return
    y = _rms(x2, fw_ref[...])
    is_ctx = pl.program_id(0) < N_CTX // TE

    @pl.when(is_ctx)
    def _():
        yc_ref[...] = y

    @pl.when(jnp.logical_not(is_ctx))
    def _():
        yl_ref[...] = y


def _combine(run_len, run_src, run_dst, lrow, ys, x1, mod, fw, *, final):
    per_seq = DEC_SEQ // TE
    ctx_tiles = N_CTX // TE
    seq_of = lambda i: jnp.where(i < ctx_tiles, 0, 1 + (i - ctx_tiles) // per_seq)
    n_tiles = N_TOK // TE
    if final:
        out_shape = [jax.ShapeDtypeStruct((N_CTX, D_MODEL), F32), jax.ShapeDtypeStruct((N_LAT, D_MODEL), F32)]
        out_specs = [pl.BlockSpec((TE, D_MODEL), lambda t, *_: (jnp.minimum(t, ctx_tiles - 1), 0)),
                     pl.BlockSpec((TE, D_MODEL), lambda t, *_: (jnp.maximum(t - ctx_tiles, 0), 0))]
    else:
        out_shape = jax.ShapeDtypeStruct((N_TOK, D_MODEL), F32)
        out_specs = pl.BlockSpec((TE, D_MODEL), lambda t, *_: (t, 0))
    return pl.pallas_call(
        functools.partial(_combine_kernel, final=final),
        out_shape=out_shape,
        grid_spec=pltpu.PrefetchScalarGridSpec(
            num_scalar_prefetch=3, grid=(n_tiles,),
            in_specs=[pl.BlockSpec((1, 1, TE), lambda t, *_: (t, 0, 0)),
                      pl.BlockSpec(memory_space=pl.ANY),
                      pl.BlockSpec((TE, D_MODEL), lambda t, *_: (t, 0)),
                      pl.BlockSpec((1, N_MOD, D_MODEL), lambda t, *_: (seq_of(t), 0, 0)),
                      pl.BlockSpec((1, D_MODEL), lambda t, *_: (0, 0))],
            out_specs=out_specs,
            scratch_shapes=[pltpu.VMEM((2, XP_ROWS, D_MODEL), F32), pltpu.SemaphoreType.DMA((2,))]),
        compiler_params=pltpu.CompilerParams(dimension_semantics=("arbitrary",),
                                             vmem_limit_bytes=VMEM_LIMIT),
        name="moe_combine",
    )(run_len, run_src, run_dst, lrow.reshape(n_tiles, 1, TE), ys, x1, mod, fw)


def _moe_kernel(tg_ref, used_ref, xs_ref, wg_ref, wu_ref, wd_ref, wsg_ref, wsu_ref, wsd_ref, o_ref):
    t = pl.program_id(0)

    @pl.when(t < used_ref[0])
    def _():
        x = xs_ref[:, 0:D_MODEL].astype(BF16)
        cw = xs_ref[:, D_MODEL:XW]
        acts = []
        for j in range(EXPERTS_PER_GROUP):
            gate = _dot(x, wg_ref[j])
            up = _dot(x, wu_ref[j])
            acts.append((gate * _sigmoid(gate) * up * cw[:, j:j + 1]).astype(BF16))
        y = _dot(jnp.concatenate(acts, axis=1), wd_ref[0])
        gate = _dot(x, wsg_ref[...])
        up = _dot(x, wsu_ref[...])
        o_ref[...] = y + _dot((gate * _sigmoid(gate) * up).astype(BF16), wsd_ref[...])

    @pl.when(t >= used_ref[0])
    def _():
        o_ref[...] = jnp.zeros_like(o_ref)


def _moe(tile_grp, n_used, xs, wg, wu, wd, wsg, wsu, wsd):
    full = lambda a: pl.BlockSpec(a.shape, lambda t, tg, nu: (0,) * a.ndim)
    grp_w = lambda a: pl.BlockSpec((EXPERTS_PER_GROUP,) + a.shape[1:], lambda t, tg, nu: (tg[t], 0, 0))
    return pl.pallas_call(
        _moe_kernel,
        out_shape=jax.ShapeDtypeStruct((N_SLOTS, D_MODEL), F32),
        grid_spec=pltpu.PrefetchScalarGridSpec(
            num_scalar_prefetch=2, grid=(N_SLOT_TILES,),
            in_specs=[pl.BlockSpec((TE, XW), lambda t, tg, nu: (jnp.minimum(t, nu[0] - 1), 0)),
                      grp_w(wg), grp_w(wu),
                      pl.BlockSpec((1,) + wd.shape[1:], lambda t, tg, nu: (tg[t], 0, 0)),
                      full(wsg), full(wsu), full(wsd)],
            out_specs=pl.BlockSpec((TE, D_MODEL), lambda t, tg, nu: (t, 0))),
        compiler_params=pltpu.CompilerParams(dimension_semantics=("arbitrary",),
                                             vmem_limit_bytes=VMEM_LIMIT),
        name="moe_experts",
    )(tile_grp, n_used, xs, wg, wu, wd, wsg, wsu, wsd)


def _rope_tables():
    rows = DEC_SEQ // GRID_W
    r = jnp.repeat(jnp.arange(rows, dtype=F32), GRID_W)
    col = jnp.tile(jnp.arange(GRID_W, dtype=F32), rows)
    n_freq = MLA_ROPE // 4
    inv = ROPE_BASE ** (-jnp.arange(n_freq, dtype=F32) / n_freq)
    ang = jnp.stack([r[:, None] * inv, col[:, None] * inv], axis=1)
    expand = lambda t: jnp.broadcast_to(t[:, :, None, :], (DEC_SEQ, 2, 2, n_freq)).reshape(DEC_SEQ, MLA_ROPE)
    cos = jnp.concatenate([expand(jnp.cos(ang)), jnp.ones((TB, MLA_ROPE), F32)], axis=0)
    sin = jnp.concatenate([expand(jnp.sin(ang)), jnp.zeros((TB, MLA_ROPE), F32)], axis=0)
    return cos, sin


def _rot_cols(w):
    shp = w.shape
    w4 = w.reshape(shp[:-1] + (2, 2, MLA_ROPE // 4))
    return jnp.stack([-w4[..., 1, :], w4[..., 0, :]], axis=-2).reshape(shp)


def _gla_tables():
    blk_f, blk_b, first, seq = [], [], [], []
    for s in range(N_SEQ):
        if s < BATCH:
            tiles = [s]
        else:
            base = CTX_TILES + (s - BATCH) * LAT_TILES_PER_SEQ
            tiles = list(range(base, base + LAT_TILES_PER_SEQ))
        for n, t in enumerate(tiles):
            blk_f.append(t)
            blk_b.append(tiles[len(tiles) - 1 - n])
            first.append(1 if n == 0 else 0)
            seq.append(s)
    as_i32 = lambda v: jnp.asarray(v, dtype=jnp.int32)
    return as_i32(blk_f), as_i32(blk_b), as_i32(first), as_i32(seq)


def kernel(x_prompt, x_sample, cache_ckv, cache_krope, state_gla_fwd, state_gla_bwd, c, c_ctx, w_mod, b_mod, norm_mix, norm_ffn, w_in, w_alpha, b_alpha, gla_norm, q_norm, kv_norm, w_uq, w_uk, w_uv, w_pa, w_pb, w_o, w_router, b_router, w_exp_gate, w_exp_up, w_exp_down, w_sh_gate, w_sh_up, w_sh_down, final_norm):
    x = (x_prompt.reshape(N_CTX, D_MODEL), x_sample.reshape(N_LAT, D_MODEL))

    cond = jnp.concatenate([c_ctx[None, :], c, jnp.zeros((16 - 1 - DEC_BATCH, D_MODEL), F32)], axis=0)
    mod_all = _modulation(cond, w_mod.astype(BF16), b_mod[:, None, :])
    mod_all = mod_all.reshape(DEPTH, 16, N_MOD, D_MODEL)

    cos32, sin32 = _rope_tables()
    cq_tab = jnp.tile(cos32, (1, MLA_HEADS)) * Q_SCALE
    sq_tab = jnp.tile(sin32, (1, MLA_HEADS)) * Q_SCALE

    idx256 = jnp.arange(TM)
    same_chunk = (idx256[:, None] // GLA_CHUNK) == (idx256[None, :] // GLA_CHUNK)
    tri_f = (same_chunk & (idx256[None, :] <= idx256[:, None])).astype(BF16)
    tri_b = (same_chunk & (idx256[None, :] >= idx256[:, None])).astype(BF16)
    ones_blk = same_chunk.astype(BF16)
    gla_tab = _gla_tables()
    tri_route = (idx256[:, None] < idx256[None, :]).astype(BF16)

    wr_hi = w_router.astype(BF16)
    wr_lo = (w_router - wr_hi.astype(F32)).astype(BF16)
    zpad = jnp.zeros((D_MODEL, 128 - 2 * N_EXPERTS), BF16)
    wr1 = jnp.concatenate([wr_hi, wr_lo, zpad], axis=1)
    wr2 = jnp.concatenate([wr_hi, jnp.zeros_like(wr_lo), zpad], axis=1)

    ckvs, krs, sfs, sbs = [], [], [], []
    for l in range(DEPTH):
        mod = mod_all[l]
        (w_gq, w_gk, w_gv, w_gg, w_gaf, w_gab, w_cq, w_ckv, w_kr, w_za, w_zb) = jnp.split(
            w_in[l], (256, 512, 1024, 1536, 1552, 1568, 1824, 1952, 1984, 3008), axis=1)
        w_small = jnp.concatenate([w_gaf, w_gab, w_kr, _rot_cols(w_kr), jnp.zeros((D_MODEL, 32), F32)], axis=1)
        w1 = jnp.concatenate([w_gq, w_gk, w_gv, w_gg, w_cq, w_ckv, w_small, w_za, w_zb], axis=1).astype(BF16)
        wa = jnp.zeros((128, 2 * HK), F32)
        wa = wa.at[0:GLA_GATE_RANK, 0:HK].set(w_alpha[l, 0])
        wa = wa.at[GLA_GATE_RANK:2 * GLA_GATE_RANK, HK:2 * HK].set(w_alpha[l, 1]).astype(BF16)
        ba = b_alpha[l].reshape(1, 2 * HK)
        wuq3 = w_uq[l].reshape(MLA_Q_RANK, MLA_HEADS, MLA_NOPE + MLA_ROPE)
        wuqn = wuq3[:, :, :MLA_NOPE].reshape(MLA_Q_RANK, MLA_HEADS * MLA_NOPE).astype(BF16)
        wuq_rope = wuq3[:, :, MLA_NOPE:]
        wuqr = wuq_rope.reshape(MLA_Q_RANK, MLA_HEADS * MLA_ROPE).astype(BF16)
        wuqrr = _rot_cols(wuq_rope).reshape(MLA_Q_RANK, MLA_HEADS * MLA_ROPE).astype(BF16)
        wukt = w_uk[l].reshape(MLA_KV_RANK, MLA_HEADS, MLA_NOPE).transpose(1, 2, 0).astype(BF16)
        wuvt = w_uv[l].reshape(MLA_KV_RANK, MLA_HEADS, MLA_DV).transpose(1, 2, 0).astype(BF16)

        gq, gk, gv, gg, la, qcat, ckvn, kr, za, zb = _inproj(
            x, mod, norm_mix[l][None, :], w1, wa, ba, q_norm[l][None, :], kv_norm[l][None, :],
            wuqn, wuqr, wuqrr, wukt, cq_tab, sq_tab, cos32, sin32)

        to_t = lambda s: s.transpose(0, 3, 1, 2).reshape(s.shape[0], GLA_DV, HK)
        zeros_ctx = jnp.zeros((BATCH, GLA_DV, HK), F32)
        s0f = jnp.concatenate([zeros_ctx, to_t(state_gla_fwd[:, l])], axis=0)
        s0b = jnp.concatenate([zeros_ctx, to_t(state_gla_bwd[:, l])], axis=0)
        o_f, o_r, sf, sb = _gla(gla_tab, gq, gk, gv, la, s0f, s0b, tri_f, tri_b, ones_blk)

        ckv_ctx = ckvn[:N_CTX].reshape(BATCH, SEQ, MLA_KV_RANK)
        kr_ctx = kr[:N_CTX].reshape(BATCH, SEQ, MLA_ROPE)
        ckv_lat = jnp.concatenate([cache_ckv[:, l], ckvn[N_CTX:].reshape(DEC_BATCH, DEC_SEQ, MLA_KV_RANK)], axis=1)
        kr_lat = jnp.concatenate([cache_krope[:, l], kr[N_CTX:].reshape(DEC_BATCH, DEC_SEQ, MLA_ROPE)], axis=1)

        def kv_operands(ckv_all, kr_all):
            kcat = jnp.concatenate([ckv_all, kr_all], axis=-1).astype(BF16)
            return kcat, ckv_all.astype(BF16).transpose(0, 2, 1)

        kc, vt = kv_operands(ckv_ctx, kr_ctx)
        ob_ctx = _attention(qcat, kc, vt, wuvt, tile0=0, tiles_per_seq=1, name="mla_ctx")
        kc, vt = kv_operands(ckv_lat, kr_lat)
        ob_lat = _attention(qcat, kc, vt, wuvt, tile0=CTX_TILES, tiles_per_seq=LAT_TILES_PER_SEQ, name="mla_lat")

        x1, h2x, lg = _merge(x, mod, o_f, o_r, gg, gla_norm[l][None, :], ob_ctx, ob_lat, za, zb, w_pa[l].astype(BF16), w_pb[l].astype(BF16),
                             w_o[l].astype(BF16), norm_ffn[l][None, :], wr1, wr2)
        h2x, grp, rank, _ = _route(lg.T, b_router[:, None], tri_route, h2x)

        n_tt = N_TOK // TE
        grp_t = grp[0].reshape(n_tt, TE)
        rank_t = rank[0].reshape(n_tt, TE)
        run_len = jnp.sum(grp_t[:, :, None] == jnp.arange(N_GROUPS, dtype=jnp.int32)[None, None, :],
                          axis=1, dtype=jnp.int32)
        run_len_al = (run_len + (RUN_ALIGN - 1)) // RUN_ALIGN * RUN_ALIGN
        ranks_before = jnp.cumsum(run_len, axis=0) - run_len
        cnt = jnp.sum(run_len_al, axis=0)
        padded = (cnt + (TE - 1)) // TE * TE
        g_end = jnp.cumsum(padded)
        g_start = g_end - padded
        run_src = jnp.cumsum(run_len_al, axis=1) - run_len_al
        run_dst = g_start[None, :] + jnp.cumsum(run_len_al, axis=0) - run_len_al
        of_group = lambda tbl: sum(jnp.where(grp_t == g, tbl[:, g:g + 1], 0) for g in range(N_GROUPS))
        lrow = (of_group(run_src) + rank_t - of_group(ranks_before)).astype(jnp.int32)
        tile_start = jnp.arange(N_SLOT_TILES, dtype=jnp.int32) * TE
        tile_grp = jnp.minimum(jnp.sum(tile_start[:, None] >= g_end[None, :], axis=1), N_GROUPS - 1).astype(jnp.int32)
        n_used = (g_end[N_GROUPS - 1:] // TE).astype(jnp.int32)
        pad_lo = jnp.concatenate([g_start + cnt, g_end[N_GROUPS - 1:]]).astype(jnp.int32)
        pad_hi = jnp.concatenate([g_end, jnp.full((1,), N_SLOTS, jnp.int32)]).astype(jnp.int32)

        run_len_al, run_src, run_dst = (a.reshape(-1).astype(jnp.int32) for a in (run_len_al, run_src, run_dst))
        xs = _dispatch(pad_lo, pad_hi, run_len_al, run_src, run_dst, lrow, h2x)
        ys = _moe(tile_grp, n_used, xs, w_exp_gate[l].astype(BF16), w_exp_up[l].astype(BF16),
                  w_exp_down[l].astype(BF16).reshape(N_GROUPS, EXPERTS_PER_GROUP * EXPERT_FF, D_MODEL),
                  w_sh_gate[l].astype(BF16), w_sh_up[l].astype(BF16), w_sh_down[l].astype(BF16))
        x = _combine(run_len_al, run_src, run_dst, lrow, ys, x1, mod, final_norm[None, :], final=(l == DEPTH - 1))

        ckvs.append(ckv_ctx)
        krs.append(kr_ctx)
        from_t = lambda s: s[:BATCH].reshape(BATCH, GLA_DV, GLA_HEADS, GLA_DK).transpose(0, 2, 3, 1)
        sfs.append(from_t(sf))
        sbs.append(from_t(sb))

    y_prompt = x[0].reshape(BATCH, SEQ, D_MODEL)
    y_sample = x[1].reshape(DEC_BATCH, DEC_SEQ, D_MODEL)
    return (y_prompt, y_sample, jnp.stack(ckvs, axis=1), jnp.stack(krs, axis=1),
            jnp.stack(sfs, axis=1), jnp.stack(sbs, axis=1))
```

```python
import functools
import itertools

import jax
import jax.numpy as jnp
from jax import lax
from jax.experimental import pallas as pl
from jax.experimental.pallas import tpu as pltpu

F32 = jnp.float32
BF16 = jnp.bfloat16

D_MODEL = 1024
BATCH = 16
SEQ = 256
DEPTH = 2
DEC_BATCH = 8
DEC_SEQ = 4096
PAST_LEN = 512
GRID_W = 64
GLA_HEADS = 4
GLA_DK = 64
GLA_DV = 128
GLA_GATE_RANK = 16
GLA_TAU = 16.0
GLA_CHUNK = 64
MLA_HEADS = 8
MLA_Q_RANK = 256
MLA_KV_RANK = 128
MLA_NOPE = 64
MLA_ROPE = 32
MLA_DV = 64
ROPE_BASE = 10000.0
N_EXPERTS = 16
N_GROUPS = 4
EXPERTS_PER_GROUP = 4
EXPERT_FF = 512
SHARED_FF = 512
N_MOD = 6
EPS = 1e-6

N_CTX = BATCH * SEQ
N_LAT = DEC_BATCH * DEC_SEQ
N_TOK = N_CTX + N_LAT
N_SEQ = BATCH + DEC_BATCH
TM = 256
N_TILES = N_TOK // TM
CTX_TILES = N_CTX // TM
LAT_TILES_PER_SEQ = DEC_SEQ // TM
TB = 512
N_TILES_B = N_TOK // TB
CTX_TILES_B = N_CTX // TB
LAT_TILES_B_PER_SEQ = DEC_SEQ // TB
HK = GLA_HEADS * GLA_DK
HV = GLA_HEADS * GLA_DV
QCAT = MLA_KV_RANK + MLA_ROPE
ATT_KEY_BLOCK = 1536
Q_SCALE =(MLA_NOPE + MLA_ROPE) ** -0.5 * 1.4426950408889634
TE = 512
RUN_ALIGN = 8
XP_ROWS = TE + N_GROUPS * RUN_ALIGN
N_SLOT_TILES = N_TOK // TE + N_GROUPS + -(-(N_TOK // TE) * N_GROUPS * (RUN_ALIGN - 1) // TE)
N_SLOTS = N_SLOT_TILES * TE
XW = D_MODEL + 128
ROUTE_W = 2048
VMEM_LIMIT = 56 * 1024 * 1024

C_GQ, C_GK, C_GV, C_GG, C_CQ, C_CKV, C_SMALL, C_ZA, C_ZB, C_END = (
    0, 256, 512, 1024, 1536, 1792, 1920, 2048, 3072, 4096)


def _sigmoid(x):
    return 1.0 / (1.0 + jnp.exp(-x))


def _rms(x, w):
    return x * lax.rsqrt(jnp.mean(x * x, axis=-1, keepdims=True) + EPS) * w


def _dot(a, b):
    return jnp.dot(a, b, preferred_element_type=F32)


def _dot_nt(a, b):
    return lax.dot_general(a, b, (((1,), (1,)), ((), ())), preferred_element_type=F32)


def _dot_tn(a, b):
    return lax.dot_general(a, b, (((0,), (0,)), ((), ())), preferred_element_type=F32)


def _tile_seq(i):
    return jnp.where(i < CTX_TILES_B, 0, 1 + (i - CTX_TILES_B) // LAT_TILES_B_PER_SEQ)


def _tile_pos(i):
    return jnp.where(i < CTX_TILES_B, LAT_TILES_B_PER_SEQ, (i - CTX_TILES_B) % LAT_TILES_B_PER_SEQ)


def _x_operands(x):
    if isinstance(x, tuple):
        xa, xb, off = x[0], x[1], 0
    else:
        xa, xb, off = x, x, CTX_TILES_B
    spec_a = pl.BlockSpec((TB, D_MODEL), lambda i: (jnp.minimum(i, CTX_TILES_B - 1), 0))
    spec_b = pl.BlockSpec((TB, D_MODEL), lambda i: (jnp.maximum(i - CTX_TILES_B, 0) + off, 0))
    return xa, xb, spec_a, spec_b


def _x_tile(xa_ref, xb_ref):
    return jnp.where(pl.program_id(0) < CTX_TILES_B, xa_ref[...], xb_ref[...])


def _mod_kernel(c_ref, w_ref, b_ref, o_ref):
    c = c_ref[...]
    sc = (c * _sigmoid(c)).astype(BF16)
    o_ref[...] = _dot(sc, w_ref[...]) + b_ref[...]


def _modulation(cond, w_mod, b_mod):
    nb = 1024
    return pl.pallas_call(
        _mod_kernel,
        out_shape=jax.ShapeDtypeStruct((DEPTH, 16, N_MOD * D_MODEL), F32),
        grid=(DEPTH, N_MOD * D_MODEL // nb),
        in_specs=[
            pl.BlockSpec((16, D_MODEL), lambda l, j: (0, 0)),
            pl.BlockSpec((None, D_MODEL, nb), lambda l, j: (l, 0, j)),
            pl.BlockSpec((None, 1, nb), lambda l, j: (l, 0, j)),
        ],
        out_specs=pl.BlockSpec((None, 16, nb), lambda l, j: (l, 0, j)),
        name="modulation",
    )(cond, w_mod, b_mod)


def _inproj_kernel(xa_ref, xb_ref, mod_ref, nw_ref, w1_ref, wa_ref, ba_ref, qnw_ref, kvnw_ref,
                   wuqn_ref, wuqr_ref, wuqrr_ref, wukt_ref, cq_ref, sq_ref, ck_ref, sk_ref,
                   gq_ref, gk_ref, gv_ref, gg_ref, la_ref, q_ref, ckv_ref, kr_ref, za_ref, zb_ref):
    x = _x_tile(xa_ref, xb_ref)
    mod = mod_ref[0]
    h = (_rms(x, nw_ref[...]) * (1.0 + mod[1:2, :]) + mod[0:1, :]).astype(BF16)

    def proj(lo, hi):
        return _dot(h, w1_ref[:, lo:hi])

    gq_ref[...] = (proj(C_GQ, C_GK) * (GLA_DK ** -0.5)).astype(BF16)
    gk_ref[...] = proj(C_GK, C_GV).astype(BF16)
    gv_ref[...] = proj(C_GV, C_GG).astype(BF16)
    gg_ref[...] = proj(C_GG, C_CQ).astype(BF16)
    za_ref[...] = _sigmoid(proj(C_ZA, C_ZB)).astype(BF16)
    zb_ref[...] = _sigmoid(proj(C_ZB, C_END)).astype(BF16)

    small = proj(C_SMALL, C_ZA)
    lin = _dot(small.astype(BF16), wa_ref[...]) + ba_ref[...]
    la_ref[...] = (jnp.minimum(lin, 0.0) - jnp.log(1.0 + jnp.exp(-jnp.abs(lin)))) * (1.0 / GLA_TAU)
    kr_ref[...] = small[:, 32:64] * ck_ref[...] + small[:, 64:96] * sk_ref[...]

    ckv_ref[...] = _rms(proj(C_CKV, C_SMALL), kvnw_ref[...])

    cqn = _rms(proj(C_CQ, C_CKV), qnw_ref[...]).astype(BF16)
    qn = _dot(cqn, wuqn_ref[...]).astype(BF16)
    qr = (_dot(cqn, wuqr_ref[...]) * cq_ref[...]
          + _dot(cqn, wuqrr_ref[...]) * sq_ref[...])
    for hd in range(MLA_HEADS):
        qa = _dot(qn[:, hd * MLA_NOPE:(hd + 1) * MLA_NOPE], wukt_ref[hd]) * Q_SCALE
        q_ref[hd, :, 0:MLA_KV_RANK] = qa.astype(BF16)
        q_ref[hd, :, MLA_KV_RANK:QCAT] = qr[:, hd * MLA_ROPE:(hd + 1) * MLA_ROPE].astype(BF16)


def _inproj(x, mod, nw, w1, wa, ba, qnw, kvnw, wuqn, wuqr, wuqrr, wukt, cq, sq, ck, sk):
    full = lambda a: pl.BlockSpec(a.shape, lambda i: (0,) * a.ndim)
    tok = lambda w: pl.BlockSpec((TB, w), lambda i: (i, 0))
    pos = lambda w: pl.BlockSpec((TB, w), lambda i: (_tile_pos(i), 0))
    xa, xb, spec_a, spec_b = _x_operands(x)
    out_shape = [
        jax.ShapeDtypeStruct((N_TOK, HK), BF16),
        jax.ShapeDtypeStruct((N_TOK, HK), BF16),
        jax.ShapeDtypeStruct((N_TOK, HV), BF16),
        jax.ShapeDtypeStruct((N_TOK, HV), BF16),
        jax.ShapeDtypeStruct((N_TOK, 2 * HK), F32),
        jax.ShapeDtypeStruct((MLA_HEADS, N_TOK, QCAT), BF16),
        jax.ShapeDtypeStruct((N_TOK, MLA_KV_RANK), F32),
        jax.ShapeDtypeStruct((N_TOK, MLA_ROPE), F32),
        jax.ShapeDtypeStruct((N_TOK, D_MODEL), BF16),
        jax.ShapeDtypeStruct((N_TOK, D_MODEL), BF16),
    ]
    out_specs = [tok(HK), tok(HK), tok(HV), tok(HV), tok(2 * HK),
                 pl.BlockSpec((MLA_HEADS, TB, QCAT), lambda i: (0, i, 0)),
                 tok(MLA_KV_RANK), tok(MLA_ROPE), tok(D_MODEL), tok(D_MODEL)]
    in_specs = [spec_a, spec_b,
                pl.BlockSpec((1, N_MOD, D_MODEL), lambda i: (_tile_seq(i), 0, 0)),
                full(nw), full(w1), full(wa), full(ba), full(qnw), full(kvnw),
                full(wuqn), full(wuqr), full(wuqrr), full(wukt),
                pos(HK), pos(HK), pos(MLA_ROPE), pos(MLA_ROPE)]
    return pl.pallas_call(
        _inproj_kernel, out_shape=out_shape, grid=(N_TILES_B,),
        in_specs=in_specs, out_specs=out_specs,
        compiler_params=pltpu.CompilerParams(dimension_semantics=("arbitrary",),
                                             vmem_limit_bytes=VMEM_LIMIT),
        name="inproj",
    )(xa, xb, mod, nw, w1, wa, ba, qnw, kvnw, wuqn, wuqr, wuqrr, wukt, cq, sq, ck, sk)


def _gla_tile(q_ref, k_ref, v_ref, la_ref, st_ref, o_ref, tri, ones_blk, *, reverse):
    n_chunks = TM // GLA_CHUNK
    la = la_ref[...]
    la_hi = la.astype(BF16)
    la_lo = (la - la_hi.astype(F32)).astype(BF16)
    cum = _dot(tri, la_hi) + _dot(tri, la_lo)
    tot = _dot(ones_blk, la_hi) + _dot(ones_blk, la_lo)
    yield
    q = q_ref[...].astype(F32)
    k = k_ref[...].astype(F32)
    qd = q * jnp.exp(cum)
    kin = (k * jnp.exp(-cum)).astype(BF16)
    kout = k * jnp.exp(tot - cum)
    decay = jnp.exp(tot)
    lane = lax.broadcasted_iota(jnp.int32, (TM, HK), 1)
    head_of_lane = lane // GLA_DK
    qd_h = [jnp.where(head_of_lane == hd, qd, 0.0).astype(BF16) for hd in range(GLA_HEADS)]
    kout_h = [jnp.where(head_of_lane == hd, kout, 0.0).astype(BF16) for hd in range(GLA_HEADS)]
    row = lax.broadcasted_iota(jnp.int32, (TM, TM), 0)
    col = lax.broadcasted_iota(jnp.int32, (TM, TM), 1)
    keep = (row // GLA_CHUNK == col // GLA_CHUNK) & ((col >= row) if reverse else (col <= row))
    head_vs = [slice(hd * GLA_DV, (hd + 1) * GLA_DV) for hd in range(GLA_HEADS)]
    yield

    att_all = _dot_nt(jnp.concatenate(qd_h, axis=0), kin)
    yield
    o_intra = [_dot(jnp.where(keep, att_all[hd * TM:(hd + 1) * TM], 0.0).astype(BF16), v_ref[:, head_vs[hd]])
               for hd in range(GLA_HEADS)]
    yield

    st = st_ref[...]
    for ci in range(n_chunks):
        c = (n_chunks - 1 - ci) if reverse else ci
        rows = slice(c * GLA_CHUNK, (c + 1) * GLA_CHUNK)
        q_stack = jnp.concatenate([qd_h[hd][rows] for hd in range(GLA_HEADS)], axis=0)
        o_inter = _dot_nt(q_stack, st.astype(BF16))
        for hd in range(GLA_HEADS):
            o = o_intra[hd][rows] + o_inter[hd * GLA_CHUNK:(hd + 1) * GLA_CHUNK]
            o_ref[rows, head_vs[hd]] = o.astype(o_ref.dtype)
        v_stack = jnp.concatenate([v_ref[rows, head_vs[hd]] for hd in range(GLA_HEADS)], axis=0)
        k_stack = jnp.concatenate([kout_h[hd][rows] for hd in range(GLA_HEADS)], axis=0)
        st = st * decay[c * GLA_CHUNK:c * GLA_CHUNK + 1, :] + _dot_tn(v_stack, k_stack)
        yield
    st_ref[...] = st


def _gla_kernel(blkf_ref, blkb_ref, first_ref, seq_ref,
                qf_ref, kf_ref, vf_ref, laf_ref, qb_ref, kb_ref, vb_ref, lab_ref, s0f_ref, s0b_ref,
                trif_ref, trib_ref, ones_ref,
                of_ref, ob_ref, sff_ref, sfb_ref, stf_ref, stb_ref):
    step = pl.program_id(0)

    @pl.when(first_ref[step] == 1)
    def _():
        stf_ref[...] = s0f_ref[0]
        stb_ref[...] = s0b_ref[0]

    fwd = _gla_tile(qf_ref, kf_ref, vf_ref, laf_ref, stf_ref, of_ref, trif_ref[...], ones_ref[...], reverse=False)
    bwd = _gla_tile(qb_ref, kb_ref, vb_ref, lab_ref, stb_ref, ob_ref, trib_ref[...], ones_ref[...], reverse=True)
    for _ in itertools.zip_longest(fwd, bwd):
        pass
    sff_ref[0] = stf_ref[...]
    sfb_ref[0] = stb_ref[...]


def _gla(tables, q, k, v, la, s0f, s0b, trif, trib, ones_blk):
    blkf, blkb, first, seq = tables
    fwd = lambda w, j=0: pl.BlockSpec((TM, w), lambda s, bf, bb, f, q_: (bf[s], j))
    bwd = lambda w, j=0: pl.BlockSpec((TM, w), lambda s, bf, bb, f, q_: (bb[s], j))
    per_seq = pl.BlockSpec((1, GLA_DV, HK), lambda s, bf, bb, f, q_: (q_[s], 0, 0))
    const = pl.BlockSpec((TM, TM), lambda s, bf, bb, f, q_: (0, 0))
    return pl.pallas_call(
        _gla_kernel,
        out_shape=[jax.ShapeDtypeStruct((N_TOK, HV), BF16), jax.ShapeDtypeStruct((N_TOK, HV), BF16),
                   jax.ShapeDtypeStruct((N_SEQ, GLA_DV, HK), F32), jax.ShapeDtypeStruct((N_SEQ, GLA_DV, HK), F32)],
        grid_spec=pltpu.PrefetchScalarGridSpec(
            num_scalar_prefetch=4, grid=(N_TILES,),
            in_specs=[fwd(HK), fwd(HK), fwd(HV), fwd(HK, 0), bwd(HK), bwd(HK), bwd(HV), bwd(HK, 1),
                      per_seq, per_seq, const, const, const],
            out_specs=[fwd(HV), bwd(HV), per_seq, per_seq],
            scratch_shapes=[pltpu.VMEM((GLA_DV, HK), F32), pltpu.VMEM((GLA_DV, HK), F32)]),
        compiler_params=pltpu.CompilerParams(dimension_semantics=("arbitrary",)),
        name="gla",
    )(blkf, blkb, first, seq, q, k, v, la, q, k, v, la, s0f, s0b, trif, trib, ones_blk)


def _attn_kernel(q_ref, k_ref, vt_ref, wuvt_ref, o_ref, s_ref, ot_ref, vh_ref):
    @pl.when(pl.program_id(1) == 0)
    def _():
        for hd in range(MLA_HEADS):
            vh_ref[hd] = _dot(wuvt_ref[hd], vt_ref[0]).astype(BF16)

    s_len = k_ref.shape[1]
    kb = min(s_len, ATT_KEY_BLOCK)
    n_kb = s_len // kb

    def scores(hd, j, m):
        keys = slice(j * kb, (j + 1) * kb)
        s = _dot_nt(k_ref[0, keys, :], q_ref[hd])
        s_ref[hd % 2, keys, :] = s
        m_blk = jnp.max(s, axis=0, keepdims=True)
        return m_blk if m is None else jnp.maximum(m, m_blk)

    m = None
    for j in range(n_kb):
        m = scores(0, j, m)
    for hd in range(MLA_HEADS):
        m_next, acc, l = None, None, None
        for j in range(n_kb):
            keys = slice(j * kb, (j + 1) * kb)
            if hd + 1 < MLA_HEADS:
                m_next = scores(hd + 1, j, m_next)
            p = jnp.exp2(s_ref[hd % 2, keys, :] - m)
            l_blk = jnp.sum(p, axis=0, keepdims=True)
            l = l_blk if l is None else l + l_blk
            part = _dot(vh_ref[hd, :, keys], p.astype(BF16))
            acc = part if acc is None else acc + part
        ot_ref[hd] = acc / l
        m = m_next
    o_ref[...] = ot_ref[...].reshape(MLA_HEADS * MLA_DV, TM).T.astype(o_ref.dtype)


def _attention(q, kcat, vt, wuvt, *, tile0, tiles_per_seq, name):
    n_seq, s_len, _ = kcat.shape
    return pl.pallas_call(
        _attn_kernel,
        out_shape=jax.ShapeDtypeStruct((n_seq * tiles_per_seq * TM, MLA_HEADS * MLA_DV), BF16),
        grid=(n_seq, tiles_per_seq),
        in_specs=[
            pl.BlockSpec((MLA_HEADS, TM, QCAT), lambda b, i: (0, tile0 + b * tiles_per_seq + i, 0)),
            pl.BlockSpec((1, s_len, QCAT), lambda b, i: (b, 0, 0)),
            pl.BlockSpec((1, MLA_KV_RANK, s_len), lambda b, i: (b, 0, 0)),
            pl.BlockSpec(wuvt.shape, lambda b, i: (0, 0, 0)),
        ],
        out_specs=pl.BlockSpec((TM, MLA_HEADS * MLA_DV), lambda b, i: (b * tiles_per_seq + i, 0)),
        scratch_shapes=[pltpu.VMEM((2, s_len, TM), F32),
                        pltpu.VMEM((MLA_HEADS, MLA_DV, TM), F32),
                        pltpu.VMEM((MLA_HEADS, MLA_DV, s_len), BF16)],
        compiler_params=pltpu.CompilerParams(dimension_semantics=("arbitrary", "arbitrary"),
                                             vmem_limit_bytes=VMEM_LIMIT),
        name=name,
    )(q, kcat, vt, wuvt)


def _merge_kernel(xa_ref, xb_ref, mod_ref, of_ref, ob_ref, gg_ref, gnw_ref, obc_ref, obl_ref, za_ref, zb_ref,
                  wpa_ref, wpb_ref, wo_ref, nw_ref, wr1_ref, wr2_ref, x1_ref, h2x_ref, lg_ref):
    mod = mod_ref[0]
    o_sum = of_ref[...].astype(F32) + ob_ref[...].astype(F32)
    gate = gg_ref[...].astype(F32)
    gate = gate * _sigmoid(gate)
    oa = jnp.concatenate(
        [(_rms(o_sum[:, hd * GLA_DV:(hd + 1) * GLA_DV], gnw_ref[...])
          * gate[:, hd * GLA_DV:(hd + 1) * GLA_DV]).astype(BF16) for hd in range(GLA_HEADS)], axis=1)
    ob = jnp.where(pl.program_id(0) < CTX_TILES_B, obc_ref[...], obl_ref[...])
    y = (za_ref[...].astype(F32) * _dot(oa, wpa_ref[...])
         + zb_ref[...].astype(F32) * _dot(ob, wpb_ref[...]))
    out = _dot(y.astype(BF16), wo_ref[...])
    x1 = _x_tile(xa_ref, xb_ref) + mod[2:3, :] * out
    x1_ref[...] = x1
    h2 = _rms(x1, nw_ref[...]) * (1.0 + mod[4:5, :]) + mod[3:4, :]
    h2_hi = h2.astype(BF16)
    h2_lo = (h2 - h2_hi.astype(F32)).astype(BF16)
    h2x_ref[:, 0:D_MODEL] = h2
    h2x_ref[:, D_MODEL:XW] = jnp.zeros((TB, XW - D_MODEL), F32)
    d1 = _dot(h2_hi, wr1_ref[...])
    d2 = _dot(h2_lo, wr2_ref[...])
    lg_ref[...] = (d1[:, 0:N_EXPERTS] + d1[:, N_EXPERTS:2 * N_EXPERTS]) + d2[:, 0:N_EXPERTS]


def _merge(x, mod, o_f, o_b, gg, gnw, ob_ctx, ob_lat, za, zb, wpa, wpb, wo, nw, wr1, wr2):
    full = lambda a: pl.BlockSpec(a.shape, lambda i: (0,) * a.ndim)
    tok = lambda w: pl.BlockSpec((TB, w), lambda i: (i, 0))
    hb = MLA_HEADS * MLA_DV
    ctx_spec = pl.BlockSpec((TB, hb), lambda i: (jnp.minimum(i, CTX_TILES_B - 1), 0))
    lat_spec = pl.BlockSpec((TB, hb), lambda i: (jnp.maximum(i - CTX_TILES_B, 0), 0))
    xa, xb, spec_a, spec_b = _x_operands(x)
    return pl.pallas_call(
        _merge_kernel,
        out_shape=[jax.ShapeDtypeStruct((N_TOK, D_MODEL), F32),
                   jax.ShapeDtypeStruct((N_TOK, XW), F32),
                   jax.ShapeDtypeStruct((N_TOK, N_EXPERTS), F32)],
        grid=(N_TILES_B,),
        in_specs=[spec_a, spec_b, pl.BlockSpec((1, N_MOD, D_MODEL), lambda i: (_tile_seq(i), 0, 0)),
                  tok(HV), tok(HV), tok(HV), full(gnw), ctx_spec, lat_spec, tok(D_MODEL), tok(D_MODEL),
                  full(wpa), full(wpb), full(wo), full(nw), full(wr1), full(wr2)],
        out_specs=[tok(D_MODEL), tok(XW), tok(N_EXPERTS)],
        compiler_params=pltpu.CompilerParams(dimension_semantics=("arbitrary",),
                                             vmem_limit_bytes=VMEM_LIMIT),
        name="merge",
    )(xa, xb, mod, o_f, o_b, gg, gnw, ob_ctx, ob_lat, za, zb, wpa, wpb, wo, nw, wr1, wr2)


def _route_kernel(lg_ref, b_ref, tri_ref, h2x_in_ref, h2x_ref, grp_ref, rank_ref, cnt_ref, carry_ref):
    del h2x_in_ref
    step = pl.program_id(0)

    @pl.when(step == 0)
    def _():
        carry_ref[...] = jnp.zeros_like(carry_ref)

    aff = _sigmoid(lg_ref[...])
    biased = aff + b_ref[...]
    row = lambda a, e: a[e:e + 1, :]
    best = None
    sel = None
    for g in range(N_GROUPS):
        b = [row(biased, g * EXPERTS_PER_GROUP + i) for i in range(EXPERTS_PER_GROUP)]
        score = None
        for i in range(EXPERTS_PER_GROUP):
            for j in range(i + 1, EXPERTS_PER_GROUP):
                pair = b[i] + b[j]
                score = pair if score is None else jnp.maximum(score, pair)
        if g == 0:
            best, sel = score, jnp.zeros_like(score, dtype=jnp.int32)
        else:
            better = score > best
            best = jnp.where(better, score, best)
            sel = jnp.where(better, g, sel)
    cb, ca = [], []
    for i in range(EXPERTS_PER_GROUP):
        vb = row(biased, i)
        va = row(aff, i)
        for g in range(1, N_GROUPS):
            vb = jnp.where(sel == g, row(biased, g * EXPERTS_PER_GROUP + i), vb)
            va = jnp.where(sel == g, row(aff, g * EXPERTS_PER_GROUP + i), va)
        cb.append(vb)
        ca.append(va)
    picked = []
    for i in range(EXPERTS_PER_GROUP):
        rank = jnp.zeros_like(sel)
        for j in range(EXPERTS_PER_GROUP):
            if j == i:
                continue
            ahead = (cb[j] >= cb[i]) if j < i else (cb[j] > cb[i])
            rank = rank + ahead.astype(jnp.int32)
        picked.append(rank < 2)
    denom = None
    for i in range(EXPERTS_PER_GROUP):
        term = jnp.where(picked[i], ca[i], 0.0)
        denom = term if denom is None else denom + term
    cw = [jnp.where(picked[i], ca[i] / denom, 0.0) for i in range(EXPERTS_PER_GROUP)]
    cw_t = jnp.concatenate(cw + [jnp.zeros((128 - EXPERTS_PER_GROUP, ROUTE_W), F32)], axis=0)
    h2x_ref[...] = cw_t.T
    grp_ref[...] = sel

    onehot = jnp.concatenate([(sel == g).astype(F32) for g in range(N_GROUPS)]
                             + [jnp.zeros((8 - N_GROUPS, ROUTE_W), F32)], axis=0)
    carry = carry_ref[...]
    for c in range(ROUTE_W // 256):
        lanes = slice(c * 256, (c + 1) * 256)
        oh = onehot[:, lanes]
        before = _dot(oh.astype(BF16), tri_ref[...]) + carry
        sel_c = sel[:, lanes]
        r = before[N_GROUPS - 1:N_GROUPS, :]
        for g in range(N_GROUPS - 2, -1, -1):
            r = jnp.where(sel_c == g, before[g:g + 1, :], r)
        rank_ref[:, lanes] = r.astype(jnp.int32)
        carry = carry + jnp.sum(oh, axis=1, keepdims=True)
    carry_ref[...] = carry
    cnt_ref[...] = carry[:, 0:128]


def _route(lg_t, b_router, tri, h2x):
    return pl.pallas_call(
        _route_kernel,
        out_shape=[jax.ShapeDtypeStruct((N_TOK, XW), F32),
                   jax.ShapeDtypeStruct((1, N_TOK), jnp.int32),
                   jax.ShapeDtypeStruct((1, N_TOK), jnp.int32),
                   jax.ShapeDtypeStruct((8, 128), F32)],
        grid=(N_TOK // ROUTE_W,),
        in_specs=[pl.BlockSpec((N_EXPERTS, ROUTE_W), lambda i: (0, i)),
                  pl.BlockSpec((N_EXPERTS, 1), lambda i: (0, 0)),
                  pl.BlockSpec((256, 256), lambda i: (0, 0)),
                  pl.BlockSpec(memory_space=pl.ANY)],
        out_specs=[pl.BlockSpec((ROUTE_W, XW - D_MODEL), lambda i: (i, D_MODEL // (XW - D_MODEL))),
                   pl.BlockSpec((1, ROUTE_W), lambda i: (0, i)),
                   pl.BlockSpec((1, ROUTE_W), lambda i: (0, i)),
                   pl.BlockSpec((8, 128), lambda i: (0, 0))],
        scratch_shapes=[pltpu.VMEM((8, 256), F32)],
        input_output_aliases={3: 0},
        compiler_params=pltpu.CompilerParams(dimension_semantics=("arbitrary",)),
        name="route",
    )(lg_t, b_router, tri, h2x)


def _dispatch_kernel(lo_ref, hi_ref, cnt_ref, src_ref, dst_ref, lrow_ref, x_ref, xs_ref, xp_ref, zrow_ref, sem, zsem):
    t = pl.program_id(0)
    row = lax.broadcasted_iota(jnp.int32, (XP_ROWS, TE), 0)
    perm = jnp.where(row == lrow_ref[0], 1.0, 0.0).astype(BF16)
    x = x_ref[...]
    xp_ref[:, 0:D_MODEL] = _dot(perm, x[:, 0:D_MODEL].astype(BF16))
    cw = x[:, D_MODEL:XW]
    cw_hi = cw.astype(BF16)
    cw_lo = (cw - cw_hi.astype(F32)).astype(BF16)
    xp_ref[:, D_MODEL:XW] = _dot(perm, cw_hi) + _dot(perm, cw_lo)

    def make_copy(row, slot_row, size):
        return pltpu.make_async_copy(xp_ref.at[pl.ds(row, size), :], xs_ref.at[pl.ds(slot_row, size), :], sem)

    _run_pieces(cnt_ref, src_ref, dst_ref, t, make_copy, lambda cp: cp.start())
    _run_pieces(cnt_ref, src_ref, dst_ref, t, make_copy, lambda cp: cp.wait())

    @pl.when(t == pl.num_programs(0) - 1)
    def _():
        zrow_ref[...] = jnp.zeros_like(zrow_ref)

        def zero_copy(slot):
            return pltpu.make_async_copy(zrow_ref.at[pl.ds(0, 1), :], xs_ref.at[pl.ds(slot, 1), :], zsem)

        def z_issue(slot, c):
            zero_copy(slot).start()
            return c

        def z_drain(slot, c):
            zero_copy(slot).wait()
            return c

        for g in range(N_GROUPS + 1):
            lax.fori_loop(lo_ref[g], hi_ref[g], z_issue, 0)
        for g in range(N_GROUPS + 1):
            lax.fori_loop(lo_ref[g], hi_ref[g], z_drain, 0)


def _dispatch(pad_lo, pad_hi, run_len, run_src, run_dst, lrow, h2x):
    n_tiles = N_TOK // TE
    return pl.pallas_call(
        _dispatch_kernel,
        out_shape=jax.ShapeDtypeStruct((N_SLOTS, XW), F32),
        grid_spec=pltpu.PrefetchScalarGridSpec(
            num_scalar_prefetch=5, grid=(n_tiles,),
            in_specs=[pl.BlockSpec((1, 1, TE), lambda t, *_: (t, 0, 0)),
                      pl.BlockSpec((TE, XW), lambda t, *_: (t, 0))],
            out_specs=pl.BlockSpec(memory_space=pl.ANY),
            scratch_shapes=[pltpu.VMEM((XP_ROWS, XW), F32), pltpu.VMEM((8, XW), F32),
                            pltpu.SemaphoreType.DMA, pltpu.SemaphoreType.DMA]),
        compiler_params=pltpu.CompilerParams(dimension_semantics=("arbitrary",)),
        name="moe_dispatch",
    )(pad_lo, pad_hi, run_len, run_src, run_dst, lrow.reshape(n_tiles, 1, TE), h2x)


def _run_pieces(len_ref, a_ref, b_ref, tile, make_copy, action):
    for g in range(N_GROUPS):
        n = len_ref[tile * N_GROUPS + g]
        a0 = a_ref[tile * N_GROUPS + g]
        b0 = b_ref[tile * N_GROUPS + g]
        off = 0
        size = TE
        while size >= RUN_ALIGN:
            take = (n & size) != 0

            @pl.when(take)
            def _(off=off, size=size):
                action(make_copy(pl.multiple_of(a0 + off, RUN_ALIGN), pl.multiple_of(b0 + off, RUN_ALIGN), size))

            off = off + jnp.where(take, size, 0)
            size //= 2


def _combine_kernel(len_ref, src_ref, dst_ref, lrow_ref, ys_ref, x1_ref, mod_ref, fw_ref, *rest, final):
    if final:
        yc_ref, yl_ref, ybuf_ref, sems = rest
    else:
        o_ref, ybuf_ref, sems = rest
    t = pl.program_id(0)
    slot = t % 2

    def fetch(tile, buf, action):
        def make_copy(row, slot_row, size):
            return pltpu.make_async_copy(ys_ref.at[pl.ds(slot_row, size), :],
                                         ybuf_ref.at[buf, pl.ds(row, size), :], sems.at[buf])
        _run_pieces(len_ref, src_ref, dst_ref, tile, make_copy, action)

    @pl.when(t == 0)
    def _():
        ybuf_ref[...] = jnp.zeros_like(ybuf_ref)
        fetch(t, 0, lambda cp: cp.start())

    @pl.when(t + 1 < pl.num_programs(0))
    def _():
        fetch(t + 1, 1 - slot, lambda cp: cp.start())

    fetch(t, slot, lambda cp: cp.wait())
    row = lax.broadcasted_iota(jnp.int32, (XP_ROWS, TE), 0)
    perm = jnp.where(row == lrow_ref[0], 1.0, 0.0).astype(BF16)
    yb = ybuf_ref[slot]
    y_hi = yb.astype(BF16)
    y_lo = (yb - y_hi.astype(F32)).astype(BF16)
    rows = _dot_tn(perm, y_hi) + _dot_tn(perm, y_lo)
    x2 = x1_ref[...] + mod_ref[0][5:6, :] * rows
    if not final:
        o_ref[...] = x2
        return
    y = _rms(x2, fw_ref[...])
    is_ctx = pl.program_id(0) < N_CTX // TE

    @pl.when(is_ctx)
    def _():
        yc_ref[...] = y

    @pl.when(jnp.logical_not(is_ctx))
    def _():
        yl_ref[...] = y


def _combine(run_len, run_src, run_dst, lrow, ys, x1, mod, fw, *, final):
    per_seq = DEC_SEQ // TE
    ctx_tiles = N_CTX // TE
    seq_of = lambda i: jnp.where(i < ctx_tiles, 0, 1 + (i - ctx_tiles) // per_seq)
    n_tiles = N_TOK // TE
    if final:
        out_shape = [jax.ShapeDtypeStruct((N_CTX, D_MODEL), F32), jax.ShapeDtypeStruct((N_LAT, D_MODEL), F32)]
        out_specs = [pl.BlockSpec((TE, D_MODEL), lambda t, *_: (jnp.minimum(t, ctx_tiles - 1), 0)),
                     pl.BlockSpec((TE, D_MODEL), lambda t, *_: (jnp.maximum(t - ctx_tiles, 0), 0))]
    else:
        out_shape = jax.ShapeDtypeStruct((N_TOK, D_MODEL), F32)
        out_specs = pl.BlockSpec((TE, D_MODEL), lambda t, *_: (t, 0))
    return pl.pallas_call(
        functools.partial(_combine_kernel, final=final),
        out_shape=out_shape,
        grid_spec=pltpu.PrefetchScalarGridSpec(
            num_scalar_prefetch=3, grid=(n_tiles,),
            in_specs=[pl.BlockSpec((1, 1, TE), lambda t, *_: (t, 0, 0)),
                      pl.BlockSpec(memory_space=pl.ANY),
                      pl.BlockSpec((TE, D_MODEL), lambda t, *_: (t, 0)),
                      pl.BlockSpec((1, N_MOD, D_MODEL), lambda t, *_: (seq_of(t), 0, 0)),
                      pl.BlockSpec((1, D_MODEL), lambda t, *_: (0, 0))],
            out_specs=out_specs,
            scratch_shapes=[pltpu.VMEM((2, XP_ROWS, D_MODEL), F32), pltpu.SemaphoreType.DMA((2,))]),
        compiler_params=pltpu.CompilerParams(dimension_semantics=("arbitrary",),
                                             vmem_limit_bytes=VMEM_LIMIT),
        name="moe_combine",
    )(run_len, run_src, run_dst, lrow.reshape(n_tiles, 1, TE), ys, x1, mod, fw)


def _moe_kernel(tg_ref, used_ref, xs_ref, wg_ref, wu_ref, wd_ref, wsg_ref, wsu_ref, wsd_ref, o_ref):
    t = pl.program_id(0)

    @pl.when(t < used_ref[0])
    def _():
        x = xs_ref[:, 0:D_MODEL].astype(BF16)
        cw = xs_ref[:, D_MODEL:XW]
        acts = []
        for j in range(EXPERTS_PER_GROUP):
            gate = _dot(x, wg_ref[j])
            up = _dot(x, wu_ref[j])
            acts.append((gate * _sigmoid(gate) * up * cw[:, j:j + 1]).astype(BF16))
        y = _dot(jnp.concatenate(acts, axis=1), wd_ref[0])
        gate = _dot(x, wsg_ref[...])
        up = _dot(x, wsu_ref[...])
        o_ref[...] = y + _dot((gate * _sigmoid(gate) * up).astype(BF16), wsd_ref[...])

    @pl.when(t >= used_ref[0])
    def _():
        o_ref[...] = jnp.zeros_like(o_ref)


def _moe(tile_grp, n_used, xs, wg, wu, wd, wsg, wsu, wsd):
    full = lambda a: pl.BlockSpec(a.shape, lambda t, tg, nu: (0,) * a.ndim)
    grp_w = lambda a: pl.BlockSpec((EXPERTS_PER_GROUP,) + a.shape[1:], lambda t, tg, nu: (tg[t], 0, 0))
    return pl.pallas_call(
        _moe_kernel,
        out_shape=jax.ShapeDtypeStruct((N_SLOTS, D_MODEL), F32),
        grid_spec=pltpu.PrefetchScalarGridSpec(
            num_scalar_prefetch=2, grid=(N_SLOT_TILES,),
            in_specs=[pl.BlockSpec((TE, XW), lambda t, tg, nu: (jnp.minimum(t, nu[0] - 1), 0)),
                      grp_w(wg), grp_w(wu),
                      pl.BlockSpec((1,) + wd.shape[1:], lambda t, tg, nu: (tg[t], 0, 0)),
                      full(wsg), full(wsu), full(wsd)],
            out_specs=pl.BlockSpec((TE, D_MODEL), lambda t, tg, nu: (t, 0))),
        compiler_params=pltpu.CompilerParams(dimension_semantics=("arbitrary",),
                                             vmem_limit_bytes=VMEM_LIMIT),
        name="moe_experts",
    )(tile_grp, n_used, xs, wg, wu, wd, wsg, wsu, wsd)


def _rope_tables():
    rows = DEC_SEQ // GRID_W
    r = jnp.repeat(jnp.arange(rows, dtype=F32), GRID_W)
    col = jnp.tile(jnp.arange(GRID_W, dtype=F32), rows)
    n_freq = MLA_ROPE // 4
    inv = ROPE_BASE ** (-jnp.arange(n_freq, dtype=F32) / n_freq)
    ang = jnp.stack([r[:, None] * inv, col[:, None] * inv], axis=1)
    expand = lambda t: jnp.broadcast_to(t[:, :, None, :], (DEC_SEQ, 2, 2, n_freq)).reshape(DEC_SEQ, MLA_ROPE)
    cos = jnp.concatenate([expand(jnp.cos(ang)), jnp.ones((TB, MLA_ROPE), F32)], axis=0)
    sin = jnp.concatenate([expand(jnp.sin(ang)), jnp.zeros((TB, MLA_ROPE), F32)], axis=0)
    return cos, sin


def _rot_cols(w):
    shp = w.shape
    w4 = w.reshape(shp[:-1] + (2, 2, MLA_ROPE // 4))
    return jnp.stack([-w4[..., 1, :], w4[..., 0, :]], axis=-2).reshape(shp)


def _gla_tables():
    blk_f, blk_b, first, seq = [], [], [], []
    for s in range(N_SEQ):
        if s < BATCH:
            tiles = [s]
        else:
            base = CTX_TILES + (s - BATCH) * LAT_TILES_PER_SEQ
            tiles = list(range(base, base + LAT_TILES_PER_SEQ))
        for n, t in enumerate(tiles):
            blk_f.append(t)
            blk_b.append(tiles[len(tiles) - 1 - n])
            first.append(1 if n == 0 else 0)
            seq.append(s)
    as_i32 = lambda v: jnp.asarray(v, dtype=jnp.int32)
    return as_i32(blk_f), as_i32(blk_b), as_i32(first), as_i32(seq)


def kernel(x_prompt, x_sample, cache_ckv, cache_krope, state_gla_fwd, state_gla_bwd, c, c_ctx, w_mod, b_mod, norm_mix, norm_ffn, w_in, w_alpha, b_alpha, gla_norm, q_norm, kv_norm, w_uq, w_uk, w_uv, w_pa, w_pb, w_o, w_router, b_router, w_exp_gate, w_exp_up, w_exp_down, w_sh_gate, w_sh_up, w_sh_down, final_norm):
    x = (x_prompt.reshape(N_CTX, D_MODEL), x_sample.reshape(N_LAT, D_MODEL))

    cond = jnp.concatenate([c_ctx[None, :], c, jnp.zeros((16 - 1 - DEC_BATCH, D_MODEL), F32)], axis=0)
    mod_all = _modulation(cond, w_mod.astype(BF16), b_mod[:, None, :])
    mod_all = mod_all.reshape(DEPTH, 16, N_MOD, D_MODEL)

    cos32, sin32 = _rope_tables()
    cq_tab = jnp.tile(cos32, (1, MLA_HEADS)) * Q_SCALE
    sq_tab = jnp.tile(sin32, (1, MLA_HEADS)) * Q_SCALE

    idx256 = jnp.arange(TM)
    same_chunk = (idx256[:, None] // GLA_CHUNK) == (idx256[None, :] // GLA_CHUNK)
    tri_f = (same_chunk & (idx256[None, :] <= idx256[:, None])).astype(BF16)
    tri_b = (same_chunk & (idx256[None, :] >= idx256[:, None])).astype(BF16)
    ones_blk = same_chunk.astype(BF16)
    gla_tab = _gla_tables()
    tri_route = (idx256[:, None] < idx256[None, :]).astype(BF16)

    wr_hi = w_router.astype(BF16)
    wr_lo = (w_router - wr_hi.astype(F32)).astype(BF16)
    zpad = jnp.zeros((D_MODEL, 128 - 2 * N_EXPERTS), BF16)
    wr1 = jnp.concatenate([wr_hi, wr_lo, zpad], axis=1)
    wr2 = jnp.concatenate([wr_hi, jnp.zeros_like(wr_lo), zpad], axis=1)

    ckvs, krs, sfs, sbs = [], [], [], []
    for l in range(DEPTH):
        mod = mod_all[l]
        (w_gq, w_gk, w_gv, w_gg, w_gaf, w_gab, w_cq, w_ckv, w_kr, w_za, w_zb) = jnp.split(
            w_in[l], (256, 512, 1024, 1536, 1552, 1568, 1824, 1952, 1984, 3008), axis=1)
        w_small = jnp.concatenate([w_gaf, w_gab, w_kr, _rot_cols(w_kr), jnp.zeros((D_MODEL, 32), F32)], axis=1)
        w1 = jnp.concatenate([w_gq, w_gk, w_gv, w_gg, w_cq, w_ckv, w_small, w_za, w_zb], axis=1).astype(BF16)
        wa = jnp.zeros((128, 2 * HK), F32)
        wa = wa.at[0:GLA_GATE_RANK, 0:HK].set(w_alpha[l, 0])
        wa = wa.at[GLA_GATE_RANK:2 * GLA_GATE_RANK, HK:2 * HK].set(w_alpha[l, 1]).astype(BF16)
        ba = b_alpha[l].reshape(1, 2 * HK)
        wuq3 = w_uq[l].reshape(MLA_Q_RANK, MLA_HEADS, MLA_NOPE + MLA_ROPE)
        wuqn = wuq3[:, :, :MLA_NOPE].reshape(MLA_Q_RANK, MLA_HEADS * MLA_NOPE).astype(BF16)
        wuq_rope = wuq3[:, :, MLA_NOPE:]
        wuqr = wuq_rope.reshape(MLA_Q_RANK, MLA_HEADS * MLA_ROPE).astype(BF16)
        wuqrr = _rot_cols(wuq_rope).reshape(MLA_Q_RANK, MLA_HEADS * MLA_ROPE).astype(BF16)
        wukt = w_uk[l].reshape(MLA_KV_RANK, MLA_HEADS, MLA_NOPE).transpose(1, 2, 0).astype(BF16)
        wuvt = w_uv[l].reshape(MLA_KV_RANK, MLA_HEADS, MLA_DV).transpose(1, 2, 0).astype(BF16)

        gq, gk, gv, gg, la, qcat, ckvn, kr, za, zb = _inproj(
            x, mod, norm_mix[l][None, :], w1, wa, ba, q_norm[l][None, :], kv_norm[l][None, :],
            wuqn, wuqr, wuqrr, wukt, cq_tab, sq_tab, cos32, sin32)

        to_t = lambda s: s.transpose(0, 3, 1, 2).reshape(s.shape[0], GLA_DV, HK)
        zeros_ctx = jnp.zeros((BATCH, GLA_DV, HK), F32)
        s0f = jnp.concatenate([zeros_ctx, to_t(state_gla_fwd[:, l])], axis=0)
        s0b = jnp.concatenate([zeros_ctx, to_t(state_gla_bwd[:, l])], axis=0)
        o_f, o_r, sf, sb = _gla(gla_tab, gq, gk, gv, la, s0f, s0b, tri_f, tri_b, ones_blk)

        ckv_ctx = ckvn[:N_CTX].reshape(BATCH, SEQ, MLA_KV_RANK)
        kr_ctx = kr[:N_CTX].reshape(BATCH, SEQ, MLA_ROPE)
        ckv_lat = jnp.concatenate([cache_ckv[:, l], ckvn[N_CTX:].reshape(DEC_BATCH, DEC_SEQ, MLA_KV_RANK)], axis=1)
        kr_lat = jnp.concatenate([cache_krope[:, l], kr[N_CTX:].reshape(DEC_BATCH, DEC_SEQ, MLA_ROPE)], axis=1)

        def kv_operands(ckv_all, kr_all):
            kcat = jnp.concatenate([ckv_all, kr_all], axis=-1).astype(BF16)
            return kcat, ckv_all.astype(BF16).transpose(0, 2, 1)

        kc, vt = kv_operands(ckv_ctx, kr_ctx)
        ob_ctx = _attention(qcat, kc, vt, wuvt, tile0=0, tiles_per_seq=1, name="mla_ctx")
        kc, vt = kv_operands(ckv_lat, kr_lat)
        ob_lat = _attention(qcat, kc, vt, wuvt, tile0=CTX_TILES, tiles_per_seq=LAT_TILES_PER_SEQ, name="mla_lat")

        x1, h2x, lg = _merge(x, mod, o_f, o_r, gg, gla_norm[l][None, :], ob_ctx, ob_lat, za, zb, w_pa[l].astype(BF16), w_pb[l].astype(BF16),
                             w_o[l].astype(BF16), norm_ffn[l][None, :], wr1, wr2)
        h2x, grp, rank, _ = _route(lg.T, b_router[:, None], tri_route, h2x)

        n_tt = N_TOK // TE
        grp_t = grp[0].reshape(n_tt, TE)
        rank_t = rank[0].reshape(n_tt, TE)
        run_len = jnp.sum(grp_t[:, :, None] == jnp.arange(N_GROUPS, dtype=jnp.int32)[None, None, :],
                          axis=1, dtype=jnp.int32)
        run_len_al = (run_len + (RUN_ALIGN - 1)) // RUN_ALIGN * RUN_ALIGN
        ranks_before = jnp.cumsum(run_len, axis=0) - run_len
        cnt = jnp.sum(run_len_al, axis=0)
        padded = (cnt + (TE - 1)) // TE * TE
        g_end = jnp.cumsum(padded)
        g_start = g_end - padded
        run_src = jnp.cumsum(run_len_al, axis=1) - run_len_al
        run_dst = g_start[None, :] + jnp.cumsum(run_len_al, axis=0) - run_len_al
        of_group = lambda tbl: sum(jnp.where(grp_t == g, tbl[:, g:g + 1], 0) for g in range(N_GROUPS))
        lrow = (of_group(run_src) + rank_t - of_group(ranks_before)).astype(jnp.int32)
        tile_start = jnp.arange(N_SLOT_TILES, dtype=jnp.int32) * TE
        tile_grp = jnp.minimum(jnp.sum(tile_start[:, None] >= g_end[None, :], axis=1), N_GROUPS - 1).astype(jnp.int32)
        n_used = (g_end[N_GROUPS - 1:] // TE).astype(jnp.int32)
        pad_lo = jnp.concatenate([g_start + cnt, g_end[N_GROUPS - 1:]]).astype(jnp.int32)
        pad_hi = jnp.concatenate([g_end, jnp.full((1,), N_SLOTS, jnp.int32)]).astype(jnp.int32)

        run_len_al, run_src, run_dst = (a.reshape(-1).astype(jnp.int32) for a in (run_len_al, run_src, run_dst))
        xs = _dispatch(pad_lo, pad_hi, run_len_al, run_src, run_dst, lrow, h2x)
        ys = _moe(tile_grp, n_used, xs, w_exp_gate[l].astype(BF16), w_exp_up[l].astype(BF16),
                  w_exp_down[l].astype(BF16).reshape(N_GROUPS, EXPERTS_PER_GROUP * EXPERT_FF, D_MODEL),
                  w_sh_gate[l].astype(BF16), w_sh_up[l].astype(BF16), w_sh_down[l].astype(BF16))
        x = _combine(run_len_al, run_src, run_dst, lrow, ys, x1, mod, final_norm[None, :], final=(l == DEPTH - 1))

        ckvs.append(ckv_ctx)
        krs.append(kr_ctx)
        from_t = lambda s: s[:BATCH].reshape(BATCH, GLA_DV, GLA_HEADS, GLA_DK).transpose(0, 2, 3, 1)
        sfs.append(from_t(sf))
        sbs.append(from_t(sb))

    y_prompt = x[0].reshape(BATCH, SEQ, D_MODEL)
    y_sample = x[1].reshape(DEC_BATCH, DEC_SEQ, D_MODEL)
    return (y_prompt, y_sample, jnp.stack(ckvs, axis=1), jnp.stack(krs, axis=1),
            jnp.stack(sfs, axis=1), jnp.stack(sbs, axis=1))
```

```python
import functools
import itertools

import jax
import jax.numpy as jnp
from jax import lax
from jax.experimental import pallas as pl
from jax.experimental.pallas import tpu as pltpu

F32 = jnp.float32
BF16 = jnp.bfloat16

D_MODEL = 1024
BATCH = 16
SEQ = 256
DEPTH = 2
DEC_BATCH = 8
DEC_SEQ = 4096
PAST_LEN = 512
GRID_W = 64
GLA_HEADS = 4
GLA_DK = 64
GLA_DV = 128
GLA_GATE_RANK = 16
GLA_TAU = 16.0
GLA_CHUNK = 64
MLA_HEADS = 8
MLA_Q_RANK = 256
MLA_KV_RANK = 128
MLA_NOPE = 64
MLA_ROPE = 32
MLA_DV = 64
ROPE_BASE = 10000.0
N_EXPERTS = 16
N_GROUPS = 4
EXPERTS_PER_GROUP = 4
EXPERT_FF = 512
SHARED_FF = 512
N_MOD = 6
EPS = 1e-6

N_CTX = BATCH * SEQ
N_LAT = DEC_BATCH * DEC_SEQ
N_TOK = N_CTX + N_LAT
N_SEQ = BATCH + DEC_BATCH
TM = 256
N_TILES = N_TOK // TM
CTX_TILES = N_CTX // TM
LAT_TILES_PER_SEQ = DEC_SEQ // TM
TB = 512
N_TILES_B = N_TOK // TB
CTX_TILES_B = N_CTX // TB
LAT_TILES_B_PER_SEQ = DEC_SEQ // TB
HK = GLA_HEADS * GLA_DK
HV = GLA_HEADS * GLA_DV
QCAT = MLA_KV_RANK + MLA_ROPE
ATT_KEY_BLOCK = 1152
Q_SCALE =(MLA_NOPE + MLA_ROPE) ** -0.5 * 1.4426950408889634
TE = 512
RUN_ALIGN = 8
XP_ROWS = TE + N_GROUPS * RUN_ALIGN
N_SLOT_TILES = N_TOK // TE + N_GROUPS + -(-(N_TOK // TE) * N_GROUPS * (RUN_ALIGN - 1) // TE)
N_SLOTS = N_SLOT_TILES * TE
XW = D_MODEL + 128
ROUTE_W = 2048
VMEM_LIMIT = 56 * 1024 * 1024

C_GQ, C_GK, C_GV, C_GG, C_CQ, C_CKV, C_SMALL, C_ZA, C_ZB, C_END = (
    0, 256, 512, 1024, 1536, 1792, 1920, 2048, 3072, 4096)


def _sigmoid(x):
    return 1.0 / (1.0 + jnp.exp(-x))


def _rms(x, w):
    return x * lax.rsqrt(jnp.mean(x * x, axis=-1, keepdims=True) + EPS) * w


def _dot(a, b):
    return jnp.dot(a, b, preferred_element_type=F32)


def _dot_nt(a, b):
    return lax.dot_general(a, b, (((1,), (1,)), ((), ())), preferred_element_type=F32)


def _dot_tn(a, b):
    return lax.dot_general(a, b, (((0,), (0,)), ((), ())), preferred_element_type=F32)


def _tile_seq(i):
    return jnp.where(i < CTX_TILES_B, 0, 1 + (i - CTX_TILES_B) // LAT_TILES_B_PER_SEQ)


def _tile_pos(i):
    return jnp.where(i < CTX_TILES_B, LAT_TILES_B_PER_SEQ, (i - CTX_TILES_B) % LAT_TILES_B_PER_SEQ)


def _x_operands(x):
    if isinstance(x, tuple):
        xa, xb, off = x[0], x[1], 0
    else:
        xa, xb, off = x, x, CTX_TILES_B
    spec_a = pl.BlockSpec((TB, D_MODEL), lambda i: (jnp.minimum(i, CTX_TILES_B - 1), 0))
    spec_b = pl.BlockSpec((TB, D_MODEL), lambda i: (jnp.maximum(i - CTX_TILES_B, 0) + off, 0))
    return xa, xb, spec_a, spec_b


def _x_tile(xa_ref, xb_ref):
    return jnp.where(pl.program_id(0) < CTX_TILES_B, xa_ref[...], xb_ref[...])


def _mod_kernel(c_ref, w_ref, b_ref, o_ref):
    c = c_ref[...]
    sc = (c * _sigmoid(c)).astype(BF16)
    o_ref[...] = _dot(sc, w_ref[...]) + b_ref[...]


def _modulation(cond, w_mod, b_mod):
    nb = 1024
    return pl.pallas_call(
        _mod_kernel,
        out_shape=jax.ShapeDtypeStruct((DEPTH, 16, N_MOD * D_MODEL), F32),
        grid=(DEPTH, N_MOD * D_MODEL // nb),
        in_specs=[
            pl.BlockSpec((16, D_MODEL), lambda l, j: (0, 0)),
            pl.BlockSpec((None, D_MODEL, nb), lambda l, j: (l, 0, j)),
            pl.BlockSpec((None, 1, nb), lambda l, j: (l, 0, j)),
        ],
        out_specs=pl.BlockSpec((None, 16, nb), lambda l, j: (l, 0, j)),
        name="modulation",
    )(cond, w_mod, b_mod)


def _inproj_kernel(xa_ref, xb_ref, mod_ref, nw_ref, w1_ref, wa_ref, ba_ref, qnw_ref, kvnw_ref,
                   wuqn_ref, wuqr_ref, wuqrr_ref, wukt_ref, cq_ref, sq_ref, ck_ref, sk_ref,
                   gq_ref, gk_ref, gv_ref, gg_ref, la_ref, q_ref, ckv_ref, kr_ref, za_ref, zb_ref):
    x = _x_tile(xa_ref, xb_ref)
    mod = mod_ref[0]
    h = (_rms(x, nw_ref[...]) * (1.0 + mod[1:2, :]) + mod[0:1, :]).astype(BF16)

    def proj(lo, hi):
        return _dot(h, w1_ref[:, lo:hi])

    gq_ref[...] = (proj(C_GQ, C_GK) * (GLA_DK ** -0.5)).astype(BF16)
    gk_ref[...] = proj(C_GK, C_GV).astype(BF16)
    gv_ref[...] = proj(C_GV, C_GG).astype(BF16)
    gg_ref[...] = proj(C_GG, C_CQ).astype(BF16)
    za_ref[...] = _sigmoid(proj(C_ZA, C_ZB)).astype(BF16)
    zb_ref[...] = _sigmoid(proj(C_ZB, C_END)).astype(BF16)

    small = proj(C_SMALL, C_ZA)
    lin = _dot(small.astype(BF16), wa_ref[...]) + ba_ref[...]
    la_ref[...] = (jnp.minimum(lin, 0.0) - jnp.log(1.0 + jnp.exp(-jnp.abs(lin)))) * (1.0 / GLA_TAU)
    kr_ref[...] = small[:, 32:64] * ck_ref[...] + small[:, 64:96] * sk_ref[...]

    ckv_ref[...] = _rms(proj(C_CKV, C_SMALL), kvnw_ref[...])

    cqn = _rms(proj(C_CQ, C_CKV), qnw_ref[...]).astype(BF16)
    qn = _dot(cqn, wuqn_ref[...]).astype(BF16)
    qr = (_dot(cqn, wuqr_ref[...]) * cq_ref[...]
          + _dot(cqn, wuqrr_ref[...]) * sq_ref[...])
    for hd in range(MLA_HEADS):
        qa = _dot(qn[:, hd * MLA_NOPE:(hd + 1) * MLA_NOPE], wukt_ref[hd]) * Q_SCALE
        q_ref[hd, :, 0:MLA_KV_RANK] = qa.astype(BF16)
        q_ref[hd, :, MLA_KV_RANK:QCAT] = qr[:, hd * MLA_ROPE:(hd + 1) * MLA_ROPE].astype(BF16)


def _inproj(x, mod, nw, w1, wa, ba, qnw, kvnw, wuqn, wuqr, wuqrr, wukt, cq, sq, ck, sk):
    full = lambda a: pl.BlockSpec(a.shape, lambda i: (0,) * a.ndim)
    tok = lambda w: pl.BlockSpec((TB, w), lambda i: (i, 0))
    pos = lambda w: pl.BlockSpec((TB, w), lambda i: (_tile_pos(i), 0))
    xa, xb, spec_a, spec_b = _x_operands(x)
    out_shape = [
        jax.ShapeDtypeStruct((N_TOK, HK), BF16),
        jax.ShapeDtypeStruct((N_TOK, HK), BF16),
        jax.ShapeDtypeStruct((N_TOK, HV), BF16),
        jax.ShapeDtypeStruct((N_TOK, HV), BF16),
        jax.ShapeDtypeStruct((N_TOK, 2 * HK), F32),
        jax.ShapeDtypeStruct((MLA_HEADS, N_TOK, QCAT), BF16),
        jax.ShapeDtypeStruct((N_TOK, MLA_KV_RANK), F32),
        jax.ShapeDtypeStruct((N_TOK, MLA_ROPE), F32),
        jax.ShapeDtypeStruct((N_TOK, D_MODEL), BF16),
        jax.ShapeDtypeStruct((N_TOK, D_MODEL), BF16),
    ]
    out_specs = [tok(HK), tok(HK), tok(HV), tok(HV), tok(2 * HK),
                 pl.BlockSpec((MLA_HEADS, TB, QCAT), lambda i: (0, i, 0)),
                 tok(MLA_KV_RANK), tok(MLA_ROPE), tok(D_MODEL), tok(D_MODEL)]
    in_specs = [spec_a, spec_b,
                pl.BlockSpec((1, N_MOD, D_MODEL), lambda i: (_tile_seq(i), 0, 0)),
                full(nw), full(w1), full(wa), full(ba), full(qnw), full(kvnw),
                full(wuqn), full(wuqr), full(wuqrr), full(wukt),
                pos(HK), pos(HK), pos(MLA_ROPE), pos(MLA_ROPE)]
    return pl.pallas_call(
        _inproj_kernel, out_shape=out_shape, grid=(N_TILES_B,),
        in_specs=in_specs, out_specs=out_specs,
        compiler_params=pltpu.CompilerParams(dimension_semantics=("arbitrary",),
                                             vmem_limit_bytes=VMEM_LIMIT),
        name="inproj",
    )(xa, xb, mod, nw, w1, wa, ba, qnw, kvnw, wuqn, wuqr, wuqrr, wukt, cq, sq, ck, sk)


def _gla_tile(q_ref, k_ref, v_ref, la_ref, st_ref, o_ref, tri, ones_blk, *, reverse):
    n_chunks = TM // GLA_CHUNK
    la = la_ref[...]
    la_hi = la.astype(BF16)
    la_lo = (la - la_hi.astype(F32)).astype(BF16)
    cum = _dot(tri, la_hi) + _dot(tri, la_lo)
    tot = _dot(ones_blk, la_hi) + _dot(ones_blk, la_lo)
    yield
    q = q_ref[...].astype(F32)
    k = k_ref[...].astype(F32)
    qd = q * jnp.exp(cum)
    kin = (k * jnp.exp(-cum)).astype(BF16)
    kout = k * jnp.exp(tot - cum)
    decay = jnp.exp(tot)
    lane = lax.broadcasted_iota(jnp.int32, (TM, HK), 1)
    head_of_lane = lane // GLA_DK
    qd_h = [jnp.where(head_of_lane == hd, qd, 0.0).astype(BF16) for hd in range(GLA_HEADS)]
    kout_h = [jnp.where(head_of_lane == hd, kout, 0.0).astype(BF16) for hd in range(GLA_HEADS)]
    row = lax.broadcasted_iota(jnp.int32, (TM, TM), 0)
    col = lax.broadcasted_iota(jnp.int32, (TM, TM), 1)
    keep = (row // GLA_CHUNK == col // GLA_CHUNK) & ((col >= row) if reverse else (col <= row))
    head_vs = [slice(hd * GLA_DV, (hd + 1) * GLA_DV) for hd in range(GLA_HEADS)]
    yield

    att_all = _dot_nt(jnp.concatenate(qd_h, axis=0), kin)
    yield
    o_intra = [_dot(jnp.where(keep, att_all[hd * TM:(hd + 1) * TM], 0.0).astype(BF16), v_ref[:, head_vs[hd]])
               for hd in range(GLA_HEADS)]
    yield

    st = st_ref[...]
    for ci in range(n_chunks):
        c = (n_chunks - 1 - ci) if reverse else ci
        rows = slice(c * GLA_CHUNK, (c + 1) * GLA_CHUNK)
        q_stack = jnp.concatenate([qd_h[hd][rows] for hd in range(GLA_HEADS)], axis=0)
        o_inter = _dot_nt(q_stack, st.astype(BF16))
        for hd in range(GLA_HEADS):
            o = o_intra[hd][rows] + o_inter[hd * GLA_CHUNK:(hd + 1) * GLA_CHUNK]
            o_ref[rows, head_vs[hd]] = o.astype(o_ref.dtype)
        v_stack = jnp.concatenate([v_ref[rows, head_vs[hd]] for hd in range(GLA_HEADS)], axis=0)
        k_stack = jnp.concatenate([kout_h[hd][rows] for hd in range(GLA_HEADS)], axis=0)
        st = st * decay[c * GLA_CHUNK:c * GLA_CHUNK + 1, :] + _dot_tn(v_stack, k_stack)
        yield
    st_ref[...] = st


def _gla_kernel(blkf_ref, blkb_ref, first_ref, seq_ref,
                qf_ref, kf_ref, vf_ref, laf_ref, qb_ref, kb_ref, vb_ref, lab_ref, s0f_ref, s0b_ref,
                trif_ref, trib_ref, ones_ref,
                of_ref, ob_ref, sff_ref, sfb_ref, stf_ref, stb_ref):
    step = pl.program_id(0)

    @pl.when(first_ref[step] == 1)
    def _():
        stf_ref[...] = s0f_ref[0]
        stb_ref[...] = s0b_ref[0]

    fwd = _gla_tile(qf_ref, kf_ref, vf_ref, laf_ref, stf_ref, of_ref, trif_ref[...], ones_ref[...], reverse=False)
    bwd = _gla_tile(qb_ref, kb_ref, vb_ref, lab_ref, stb_ref, ob_ref, trib_ref[...], ones_ref[...], reverse=True)
    for _ in itertools.zip_longest(fwd, bwd):
        pass
    sff_ref[0] = stf_ref[...]
    sfb_ref[0] = stb_ref[...]


def _gla(tables, q, k, v, la, s0f, s0b, trif, trib, ones_blk):
    blkf, blkb, first, seq = tables
    fwd = lambda w, j=0: pl.BlockSpec((TM, w), lambda s, bf, bb, f, q_: (bf[s], j))
    bwd = lambda w, j=0: pl.BlockSpec((TM, w), lambda s, bf, bb, f, q_: (bb[s], j))
    per_seq = pl.BlockSpec((1, GLA_DV, HK), lambda s, bf, bb, f, q_: (q_[s], 0, 0))
    const = pl.BlockSpec((TM, TM), lambda s, bf, bb, f, q_: (0, 0))
    return pl.pallas_call(
        _gla_kernel,
        out_shape=[jax.ShapeDtypeStruct((N_TOK, HV), BF16), jax.ShapeDtypeStruct((N_TOK, HV), BF16),
                   jax.ShapeDtypeStruct((N_SEQ, GLA_DV, HK), F32), jax.ShapeDtypeStruct((N_SEQ, GLA_DV, HK), F32)],
        grid_spec=pltpu.PrefetchScalarGridSpec(
            num_scalar_prefetch=4, grid=(N_TILES,),
            in_specs=[fwd(HK), fwd(HK), fwd(HV), fwd(HK, 0), bwd(HK), bwd(HK), bwd(HV), bwd(HK, 1),
                      per_seq, per_seq, const, const, const],
            out_specs=[fwd(HV), bwd(HV), per_seq, per_seq],
            scratch_shapes=[pltpu.VMEM((GLA_DV, HK), F32), pltpu.VMEM((GLA_DV, HK), F32)]),
        compiler_params=pltpu.CompilerParams(dimension_semantics=("arbitrary",)),
        name="gla",
    )(blkf, blkb, first, seq, q, k, v, la, q, k, v, la, s0f, s0b, trif, trib, ones_blk)


def _attn_kernel(q_ref, k_ref, vt_ref, wuvt_ref, o_ref, s_ref, p_ref, ot_ref, vh_ref):
    @pl.when(pl.program_id(1) == 0)
    def _():
        for hd in range(MLA_HEADS):
            vh_ref[hd] = _dot(wuvt_ref[hd], vt_ref[0]).astype(BF16)

    s_len = k_ref.shape[1]
    kb = min(s_len, ATT_KEY_BLOCK)
    n_kb = s_len // kb

    def scores(hd, j, m):
        keys = slice(j * kb, (j + 1) * kb)
        s = _dot_nt(k_ref[0, keys, :], q_ref[hd])
        s_ref[hd % 2, keys, :] = s
        m_blk = jnp.max(s, axis=0, keepdims=True)
        return m_blk if m is None else jnp.maximum(m, m_blk)

    m = None
    for j in range(n_kb):
        m = scores(0, j, m)
    l_prev = None
    for hd in range(MLA_HEADS + 1):
        m_next, acc, l = None, None, None
        for j in range(n_kb):
            keys = slice(j * kb, (j + 1) * kb)
            if hd + 1 < MLA_HEADS:
                m_next = scores(hd + 1, j, m_next)
            if hd < MLA_HEADS:
                p = jnp.exp2(s_ref[hd % 2, keys, :] - m)
                l_blk = jnp.sum(p, axis=0, keepdims=True)
                l = l_blk if l is None else l + l_blk
                p_ref[hd % 2, keys, :] = p.astype(BF16)
            if hd > 0:
                part = _dot(vh_ref[hd - 1, :, keys], p_ref[(hd - 1) % 2, keys, :])
                acc = part if acc is None else acc + part
        if hd > 0:
            ot_ref[hd - 1] = acc / l_prev
        m, l_prev = m_next, l
    o_ref[...] = ot_ref[...].reshape(MLA_HEADS * MLA_DV, TM).T.astype(o_ref.dtype)


def _attention(q, kcat, vt, wuvt, *, tile0, tiles_per_seq, name):
    n_seq, s_len, _ = kcat.shape
    return pl.pallas_call(
        _attn_kernel,
        out_shape=jax.ShapeDtypeStruct((n_seq * tiles_per_seq * TM, MLA_HEADS * MLA_DV), BF16),
        grid=(n_seq, tiles_per_seq),
        in_specs=[
            pl.BlockSpec((MLA_HEADS, TM, QCAT), lambda b, i: (0, tile0 + b * tiles_per_seq + i, 0)),
            pl.BlockSpec((1, s_len, QCAT), lambda b, i: (b, 0, 0)),
            pl.BlockSpec((1, MLA_KV_RANK, s_len), lambda b, i: (b, 0, 0)),
            pl.BlockSpec(wuvt.shape, lambda b, i: (0, 0, 0)),
        ],
        out_specs=pl.BlockSpec((TM, MLA_HEADS * MLA_DV), lambda b, i: (b * tiles_per_seq + i, 0)),
        scratch_shapes=[pltpu.VMEM((2, s_len, TM), F32), pltpu.VMEM((2, s_len, TM), BF16),
                        pltpu.VMEM((MLA_HEADS, MLA_DV, TM), F32),
                        pltpu.VMEM((MLA_HEADS, MLA_DV, s_len), BF16)],
        compiler_params=pltpu.CompilerParams(dimension_semantics=("arbitrary", "arbitrary"),
                                             vmem_limit_bytes=VMEM_LIMIT),
        name=name,
    )(q, kcat, vt, wuvt)


def _merge_kernel(xa_ref, xb_ref, mod_ref, of_ref, ob_ref, gg_ref, gnw_ref, obc_ref, obl_ref, za_ref, zb_ref,
                  wpa_ref, wpb_ref, wo_ref, nw_ref, wr1_ref, wr2_ref, x1_ref, h2x_ref, lg_ref):
    mod = mod_ref[0]
    o_sum = of_ref[...].astype(F32) + ob_ref[...].astype(F32)
    gate = gg_ref[...].astype(F32)
    gate = gate * _sigmoid(gate)
    oa = jnp.concatenate(
        [(_rms(o_sum[:, hd * GLA_DV:(hd + 1) * GLA_DV], gnw_ref[...])
          * gate[:, hd * GLA_DV:(hd + 1) * GLA_DV]).astype(BF16) for hd in range(GLA_HEADS)], axis=1)
    ob = jnp.where(pl.program_id(0) < CTX_TILES_B, obc_ref[...], obl_ref[...])
    y = (za_ref[...].astype(F32) * _dot(oa, wpa_ref[...])
         + zb_ref[...].astype(F32) * _dot(ob, wpb_ref[...]))
    out = _dot(y.astype(BF16), wo_ref[...])
    x1 = _x_tile(xa_ref, xb_ref) + mod[2:3, :] * out
    x1_ref[...] = x1
    h2 = _rms(x1, nw_ref[...]) * (1.0 + mod[4:5, :]) + mod[3:4, :]
    h2_hi = h2.astype(BF16)
    h2_lo = (h2 - h2_hi.astype(F32)).astype(BF16)
    h2x_ref[:, 0:D_MODEL] = h2
    h2x_ref[:, D_MODEL:XW] = jnp.zeros((TB, XW - D_MODEL), F32)
    d1 = _dot(h2_hi, wr1_ref[...])
    d2 = _dot(h2_lo, wr2_ref[...])
    lg_ref[...] = (d1[:, 0:N_EXPERTS] + d1[:, N_EXPERTS:2 * N_EXPERTS]) + d2[:, 0:N_EXPERTS]


def _merge(x, mod, o_f, o_b, gg, gnw, ob_ctx, ob_lat, za, zb, wpa, wpb, wo, nw, wr1, wr2):
    full = lambda a: pl.BlockSpec(a.shape, lambda i: (0,) * a.ndim)
    tok = lambda w: pl.BlockSpec((TB, w), lambda i: (i, 0))
    hb = MLA_HEADS * MLA_DV
    ctx_spec = pl.BlockSpec((TB, hb), lambda i: (jnp.minimum(i, CTX_TILES_B - 1), 0))
    lat_spec = pl.BlockSpec((TB, hb), lambda i: (jnp.maximum(i - CTX_TILES_B, 0), 0))
    xa, xb, spec_a, spec_b = _x_operands(x)
    return pl.pallas_call(
        _merge_kernel,
        out_shape=[jax.ShapeDtypeStruct((N_TOK, D_MODEL), F32),
                   jax.ShapeDtypeStruct((N_TOK, XW), F32),
                   jax.ShapeDtypeStruct((N_TOK, N_EXPERTS), F32)],
        grid=(N_TILES_B,),
        in_specs=[spec_a, spec_b, pl.BlockSpec((1, N_MOD, D_MODEL), lambda i: (_tile_seq(i), 0, 0)),
                  tok(HV), tok(HV), tok(HV), full(gnw), ctx_spec, lat_spec, tok(D_MODEL), tok(D_MODEL),
                  full(wpa), full(wpb), full(wo), full(nw), full(wr1), full(wr2)],
        out_specs=[tok(D_MODEL), tok(XW), tok(N_EXPERTS)],
        compiler_params=pltpu.CompilerParams(dimension_semantics=("arbitrary",),
                                             vmem_limit_bytes=VMEM_LIMIT),
        name="merge",
    )(xa, xb, mod, o_f, o_b, gg, gnw, ob_ctx, ob_lat, za, zb, wpa, wpb, wo, nw, wr1, wr2)


def _route_kernel(lg_ref, b_ref, tri_ref, h2x_in_ref, h2x_ref, grp_ref, rank_ref, cnt_ref, carry_ref):
    del h2x_in_ref
    step = pl.program_id(0)

    @pl.when(step == 0)
    def _():
        carry_ref[...] = jnp.zeros_like(carry_ref)

    aff = _sigmoid(lg_ref[...])
    biased = aff + b_ref[...]
    row = lambda a, e: a[e:e + 1, :]
    best = None
    sel = None
    for g in range(N_GROUPS):
        b = [row(biased, g * EXPERTS_PER_GROUP + i) for i in range(EXPERTS_PER_GROUP)]
        score = None
        for i in range(EXPERTS_PER_GROUP):
            for j in range(i + 1, EXPERTS_PER_GROUP):
                pair = b[i] + b[j]
                score = pair if score is None else jnp.maximum(score, pair)
        if g == 0:
            best, sel = score, jnp.zeros_like(score, dtype=jnp.int32)
        else:
            better = score > best
            best = jnp.where(better, score, best)
            sel = jnp.where(better, g, sel)
    cb, ca = [], []
    for i in range(EXPERTS_PER_GROUP):
        vb = row(biased, i)
        va = row(aff, i)
        for g in range(1, N_GROUPS):
            vb = jnp.where(sel == g, row(biased, g * EXPERTS_PER_GROUP + i), vb)
            va = jnp.where(sel == g, row(aff, g * EXPERTS_PER_GROUP + i), va)
        cb.append(vb)
        ca.append(va)
    picked = []
    for i in range(EXPERTS_PER_GROUP):
        rank = jnp.zeros_like(sel)
        for j in range(EXPERTS_PER_GROUP):
            if j == i:
                continue
            ahead = (cb[j] >= cb[i]) if j < i else (cb[j] > cb[i])
            rank = rank + ahead.astype(jnp.int32)
        picked.append(rank < 2)
    denom = None
    for i in range(EXPERTS_PER_GROUP):
        term = jnp.where(picked[i], ca[i], 0.0)
        denom = term if denom is None else denom + term
    cw = [jnp.where(picked[i], ca[i] / denom, 0.0) for i in range(EXPERTS_PER_GROUP)]
    cw_t = jnp.concatenate(cw + [jnp.zeros((128 - EXPERTS_PER_GROUP, ROUTE_W), F32)], axis=0)
    h2x_ref[...] = cw_t.T
    grp_ref[...] = sel

    onehot = jnp.concatenate([(sel == g).astype(F32) for g in range(N_GROUPS)]
                             + [jnp.zeros((8 - N_GROUPS, ROUTE_W), F32)], axis=0)
    carry = carry_ref[...]
    for c in range(ROUTE_W // 256):
        lanes = slice(c * 256, (c + 1) * 256)
        oh = onehot[:, lanes]
        before = _dot(oh.astype(BF16), tri_ref[...]) + carry
        sel_c = sel[:, lanes]
        r = before[N_GROUPS - 1:N_GROUPS, :]
        for g in range(N_GROUPS - 2, -1, -1):
            r = jnp.where(sel_c == g, before[g:g + 1, :], r)
        rank_ref[:, lanes] = r.astype(jnp.int32)
        carry = carry + jnp.sum(oh, axis=1, keepdims=True)
    carry_ref[...] = carry
    cnt_ref[...] = carry[:, 0:128]


def _route(lg_t, b_router, tri, h2x):
    return pl.pallas_call(
        _route_kernel,
        out_shape=[jax.ShapeDtypeStruct((N_TOK, XW), F32),
                   jax.ShapeDtypeStruct((1, N_TOK), jnp.int32),
                   jax.ShapeDtypeStruct((1, N_TOK), jnp.int32),
                   jax.ShapeDtypeStruct((8, 128), F32)],
        grid=(N_TOK // ROUTE_W,),
        in_specs=[pl.BlockSpec((N_EXPERTS, ROUTE_W), lambda i: (0, i)),
                  pl.BlockSpec((N_EXPERTS, 1), lambda i: (0, 0)),
                  pl.BlockSpec((256, 256), lambda i: (0, 0)),
                  pl.BlockSpec(memory_space=pl.ANY)],
        out_specs=[pl.BlockSpec((ROUTE_W, XW - D_MODEL), lambda i: (i, D_MODEL // (XW - D_MODEL))),
                   pl.BlockSpec((1, ROUTE_W), lambda i: (0, i)),
                   pl.BlockSpec((1, ROUTE_W), lambda i: (0, i)),
                   pl.BlockSpec((8, 128), lambda i: (0, 0))],
        scratch_shapes=[pltpu.VMEM((8, 256), F32)],
        input_output_aliases={3: 0},
        compiler_params=pltpu.CompilerParams(dimension_semantics=("arbitrary",)),
        name="route",
    )(lg_t, b_router, tri, h2x)


def _dispatch_kernel(lo_ref, hi_ref, cnt_ref, src_ref, dst_ref, lrow_ref, x_ref, xs_ref, xp_ref, zrow_ref, sem, zsem):
    t = pl.program_id(0)
    row = lax.broadcasted_iota(jnp.int32, (XP_ROWS, TE), 0)
    perm = jnp.where(row == lrow_ref[0], 1.0, 0.0).astype(BF16)
    x = x_ref[...]
    xp_ref[:, 0:D_MODEL] = _dot(perm, x[:, 0:D_MODEL].astype(BF16))
    cw = x[:, D_MODEL:XW]
    cw_hi = cw.astype(BF16)
    cw_lo = (cw - cw_hi.astype(F32)).astype(BF16)
    xp_ref[:, D_MODEL:XW] = _dot(perm, cw_hi) + _dot(perm, cw_lo)

    def make_copy(row, slot_row, size):
        return pltpu.make_async_copy(xp_ref.at[pl.ds(row, size), :], xs_ref.at[pl.ds(slot_row, size), :], sem)

    _run_pieces(cnt_ref, src_ref, dst_ref, t, make_copy, lambda cp: cp.start())
    _run_pieces(cnt_ref, src_ref, dst_ref, t, make_copy, lambda cp: cp.wait())

    @pl.when(t == pl.num_programs(0) - 1)
    def _():
        zrow_ref[...] = jnp.zeros_like(zrow_ref)

        def zero_copy(slot):
            return pltpu.make_async_copy(zrow_ref.at[pl.ds(0, 1), :], xs_ref.at[pl.ds(slot, 1), :], zsem)

        def z_issue(slot, c):
            zero_copy(slot).start()
            return c

        def z_drain(slot, c):
            zero_copy(slot).wait()
            return c

        for g in range(N_GROUPS + 1):
            lax.fori_loop(lo_ref[g], hi_ref[g], z_issue, 0)
        for g in range(N_GROUPS + 1):
            lax.fori_loop(lo_ref[g], hi_ref[g], z_drain, 0)


def _dispatch(pad_lo, pad_hi, run_len, run_src, run_dst, lrow, h2x):
    n_tiles = N_TOK // TE
    return pl.pallas_call(
        _dispatch_kernel,
        out_shape=jax.ShapeDtypeStruct((N_SLOTS, XW), F32),
        grid_spec=pltpu.PrefetchScalarGridSpec(
            num_scalar_prefetch=5, grid=(n_tiles,),
            in_specs=[pl.BlockSpec((1, 1, TE), lambda t, *_: (t, 0, 0)),
                      pl.BlockSpec((TE, XW), lambda t, *_: (t, 0))],
            out_specs=pl.BlockSpec(memory_space=pl.ANY),
            scratch_shapes=[pltpu.VMEM((XP_ROWS, XW), F32), pltpu.VMEM((8, XW), F32),
                            pltpu.SemaphoreType.DMA, pltpu.SemaphoreType.DMA]),
        compiler_params=pltpu.CompilerParams(dimension_semantics=("arbitrary",)),
        name="moe_dispatch",
    )(pad_lo, pad_hi, run_len, run_src, run_dst, lrow.reshape(n_tiles, 1, TE), h2x)


def _run_pieces(len_ref, a_ref, b_ref, tile, make_copy, action):
    for g in range(N_GROUPS):
        n = len_ref[tile * N_GROUPS + g]
        a0 = a_ref[tile * N_GROUPS + g]
        b0 = b_ref[tile * N_GROUPS + g]
        off = 0
        size = TE
        while size >= RUN_ALIGN:
            take = (n & size) != 0

            @pl.when(take)
            def _(off=off, size=size):
                action(make_copy(pl.multiple_of(a0 + off, RUN_ALIGN), pl.multiple_of(b0 + off, RUN_ALIGN), size))

            off = off + jnp.where(take, size, 0)
            size //= 2


def _combine_kernel(len_ref, src_ref, dst_ref, lrow_ref, ys_ref, x1_ref, mod_ref, fw_ref, *rest, final):
    if final:
        yc_ref, yl_ref, ybuf_ref, sems = rest
    else:
        o_ref, ybuf_ref, sems = rest
    t = pl.program_id(0)
    slot = t % 2

    def fetch(tile, buf, action):
        def make_copy(row, slot_row, size):
            return pltpu.make_async_copy(ys_ref.at[pl.ds(slot_row, size), :],
                                         ybuf_ref.at[buf, pl.ds(row, size), :], sems.at[buf])
        _run_pieces(len_ref, src_ref, dst_ref, tile, make_copy, action)

    @pl.when(t == 0)
    def _():
        ybuf_ref[...] = jnp.zeros_like(ybuf_ref)
        fetch(t, 0, lambda cp: cp.start())

    @pl.when(t + 1 < pl.num_programs(0))
    def _():
        fetch(t + 1, 1 - slot, lambda cp: cp.start())

    fetch(t, slot, lambda cp: cp.wait())
    row = lax.broadcasted_iota(jnp.int32, (XP_ROWS, TE), 0)
    perm = jnp.where(row == lrow_ref[0], 1.0, 0.0).astype(BF16)
    yb = ybuf_ref[slot]
    y_hi = yb.astype(BF16)
    y_lo = (yb - y_hi.astype(F32)).astype(BF16)
    rows = _dot_tn(perm, y_hi) + _dot_tn(perm, y_lo)
    x2 = x1_ref[...] + mod_ref[0][5:6, :] * rows
    if not final:
        o_ref[...] = x2
        return
    y = _rms(x2, fw_ref[...])
    is_ctx = pl.program_id(0) < N_CTX // TE

    @pl.when(is_ctx)
    def _():
        yc_ref[...] = y

    @pl.when(jnp.logical_not(is_ctx))
    def _():
        yl_ref[...] = y


def _combine(run_len, run_src, run_dst, lrow, ys, x1, mod, fw, *, final):
    per_seq = DEC_SEQ // TE
    ctx_tiles = N_CTX // TE
    seq_of = lambda i: jnp.where(i < ctx_tiles, 0, 1 + (i - ctx_tiles) // per_seq)
    n_tiles = N_TOK // TE
    if final:
        out_shape = [jax.ShapeDtypeStruct((N_CTX, D_MODEL), F32), jax.ShapeDtypeStruct((N_LAT, D_MODEL), F32)]
        out_specs = [pl.BlockSpec((TE, D_MODEL), lambda t, *_: (jnp.minimum(t, ctx_tiles - 1), 0)),
                     pl.BlockSpec((TE, D_MODEL), lambda t, *_: (jnp.maximum(t - ctx_tiles, 0), 0))]
    else:
        out_shape = jax.ShapeDtypeStruct((N_TOK, D_MODEL), F32)
        out_specs = pl.BlockSpec((TE, D_MODEL), lambda t, *_: (t, 0))
    return pl.pallas_call(
        functools.partial(_combine_kernel, final=final),
        out_shape=out_shape,
        grid_spec=pltpu.PrefetchScalarGridSpec(
            num_scalar_prefetch=3, grid=(n_tiles,),
            in_specs=[pl.BlockSpec((1, 1, TE), lambda t, *_: (t, 0, 0)),
                      pl.BlockSpec(memory_space=pl.ANY),
                      pl.BlockSpec((TE, D_MODEL), lambda t, *_: (t, 0)),
                      pl.BlockSpec((1, N_MOD, D_MODEL), lambda t, *_: (seq_of(t), 0, 0)),
                      pl.BlockSpec((1, D_MODEL), lambda t, *_: (0, 0))],
            out_specs=out_specs,
            scratch_shapes=[pltpu.VMEM((2, XP_ROWS, D_MODEL), F32), pltpu.SemaphoreType.DMA((2,))]),
        compiler_params=pltpu.CompilerParams(dimension_semantics=("arbitrary",),
                                             vmem_limit_bytes=VMEM_LIMIT),
        name="moe_combine",
    )(run_len, run_src, run_dst, lrow.reshape(n_tiles, 1, TE), ys, x1, mod, fw)


def _moe_kernel(tg_ref, used_ref, xs_ref, wg_ref, wu_ref, wd_ref, wsg_ref, wsu_ref, wsd_ref, o_ref):
    t = pl.program_id(0)

    @pl.when(t < used_ref[0])
    def _():
        x = xs_ref[:, 0:D_MODEL].astype(BF16)
        cw = xs_ref[:, D_MODEL:XW]
        acts = []
        for j in range(EXPERTS_PER_GROUP):
            gate = _dot(x, wg_ref[j])
            up = _dot(x, wu_ref[j])
            acts.append((gate * _sigmoid(gate) * up * cw[:, j:j + 1]).astype(BF16))
        y = _dot(jnp.concatenate(acts, axis=1), wd_ref[0])
        gate = _dot(x, wsg_ref[...])
        up = _dot(x, wsu_ref[...])
        o_ref[...] = y + _dot((gate * _sigmoid(gate) * up).astype(BF16), wsd_ref[...])

    @pl.when(t >= used_ref[0])
    def _():
        o_ref[...] = jnp.zeros_like(o_ref)


def _moe(tile_grp, n_used, xs, wg, wu, wd, wsg, wsu, wsd):
    full = lambda a: pl.BlockSpec(a.shape, lambda t, tg, nu: (0,) * a.ndim)
    grp_w = lambda a: pl.BlockSpec((EXPERTS_PER_GROUP,) + a.shape[1:], lambda t, tg, nu: (tg[t], 0, 0))
    return pl.pallas_call(
        _moe_kernel,
        out_shape=jax.ShapeDtypeStruct((N_SLOTS, D_MODEL), F32),
        grid_spec=pltpu.PrefetchScalarGridSpec(
            num_scalar_prefetch=2, grid=(N_SLOT_TILES,),
            in_specs=[pl.BlockSpec((TE, XW), lambda t, tg, nu: (jnp.minimum(t, nu[0] - 1), 0)),
                      grp_w(wg), grp_w(wu),
                      pl.BlockSpec((1,) + wd.shape[1:], lambda t, tg, nu: (tg[t], 0, 0)),
                      full(wsg), full(wsu), full(wsd)],
            out_specs=pl.BlockSpec((TE, D_MODEL), lambda t, tg, nu: (t, 0))),
        compiler_params=pltpu.CompilerParams(dimension_semantics=("arbitrary",),
                                             vmem_limit_bytes=VMEM_LIMIT),
        name="moe_experts",
    )(tile_grp, n_used, xs, wg, wu, wd, wsg, wsu, wsd)


def _rope_tables():
    rows = DEC_SEQ // GRID_W
    r = jnp.repeat(jnp.arange(rows, dtype=F32), GRID_W)
    col = jnp.tile(jnp.arange(GRID_W, dtype=F32), rows)
    n_freq = MLA_ROPE // 4
    inv = ROPE_BASE ** (-jnp.arange(n_freq, dtype=F32) / n_freq)
    ang = jnp.stack([r[:, None] * inv, col[:, None] * inv], axis=1)
    expand = lambda t: jnp.broadcast_to(t[:, :, None, :], (DEC_SEQ, 2, 2, n_freq)).reshape(DEC_SEQ, MLA_ROPE)
    cos = jnp.concatenate([expand(jnp.cos(ang)), jnp.ones((TB, MLA_ROPE), F32)], axis=0)
    sin = jnp.concatenate([expand(jnp.sin(ang)), jnp.zeros((TB, MLA_ROPE), F32)], axis=0)
    return cos, sin


def _rot_cols(w):
    shp = w.shape
    w4 = w.reshape(shp[:-1] + (2, 2, MLA_ROPE // 4))
    return jnp.stack([-w4[..., 1, :], w4[..., 0, :]], axis=-2).reshape(shp)


def _gla_tables():
    blk_f, blk_b, first, seq = [], [], [], []
    for s in range(N_SEQ):
        if s < BATCH:
            tiles = [s]
        else:
            base = CTX_TILES + (s - BATCH) * LAT_TILES_PER_SEQ
            tiles = list(range(base, base + LAT_TILES_PER_SEQ))
        for n, t in enumerate(tiles):
            blk_f.append(t)
            blk_b.append(tiles[len(tiles) - 1 - n])
            first.append(1 if n == 0 else 0)
            seq.append(s)
    as_i32 = lambda v: jnp.asarray(v, dtype=jnp.int32)
    return as_i32(blk_f), as_i32(blk_b), as_i32(first), as_i32(seq)


def kernel(x_prompt, x_sample, cache_ckv, cache_krope, state_gla_fwd, state_gla_bwd, c, c_ctx, w_mod, b_mod, norm_mix, norm_ffn, w_in, w_alpha, b_alpha, gla_norm, q_norm, kv_norm, w_uq, w_uk, w_uv, w_pa, w_pb, w_o, w_router, b_router, w_exp_gate, w_exp_up, w_exp_down, w_sh_gate, w_sh_up, w_sh_down, final_norm):
    x = (x_prompt.reshape(N_CTX, D_MODEL), x_sample.reshape(N_LAT, D_MODEL))

    cond = jnp.concatenate([c_ctx[None, :], c, jnp.zeros((16 - 1 - DEC_BATCH, D_MODEL), F32)], axis=0)
    mod_all = _modulation(cond, w_mod.astype(BF16), b_mod[:, None, :])
    mod_all = mod_all.reshape(DEPTH, 16, N_MOD, D_MODEL)

    cos32, sin32 = _rope_tables()
    cq_tab = jnp.tile(cos32, (1, MLA_HEADS)) * Q_SCALE
    sq_tab = jnp.tile(sin32, (1, MLA_HEADS)) * Q_SCALE

    idx256 = jnp.arange(TM)
    same_chunk = (idx256[:, None] // GLA_CHUNK) == (idx256[None, :] // GLA_CHUNK)
    tri_f = (same_chunk & (idx256[None, :] <= idx256[:, None])).astype(BF16)
    tri_b = (same_chunk & (idx256[None, :] >= idx256[:, None])).astype(BF16)
    ones_blk = same_chunk.astype(BF16)
    gla_tab = _gla_tables()
    tri_route = (idx256[:, None] < idx256[None, :]).astype(BF16)

    wr_hi = w_router.astype(BF16)
    wr_lo = (w_router - wr_hi.astype(F32)).astype(BF16)
    zpad = jnp.zeros((D_MODEL, 128 - 2 * N_EXPERTS), BF16)
    wr1 = jnp.concatenate([wr_hi, wr_lo, zpad], axis=1)
    wr2 = jnp.concatenate([wr_hi, jnp.zeros_like(wr_lo), zpad], axis=1)

    ckvs, krs, sfs, sbs = [], [], [], []
    for l in range(DEPTH):
        mod = mod_all[l]
        (w_gq, w_gk, w_gv, w_gg, w_gaf, w_gab, w_cq, w_ckv, w_kr, w_za, w_zb) = jnp.split(
            w_in[l], (256, 512, 1024, 1536, 1552, 1568, 1824, 1952, 1984, 3008), axis=1)
        w_small = jnp.concatenate([w_gaf, w_gab, w_kr, _rot_cols(w_kr), jnp.zeros((D_MODEL, 32), F32)], axis=1)
        w1 = jnp.concatenate([w_gq, w_gk, w_gv, w_gg, w_cq, w_ckv, w_small, w_za, w_zb], axis=1).astype(BF16)
        wa = jnp.zeros((128, 2 * HK), F32)
        wa = wa.at[0:GLA_GATE_RANK, 0:HK].set(w_alpha[l, 0])
        wa = wa.at[GLA_GATE_RANK:2 * GLA_GATE_RANK, HK:2 * HK].set(w_alpha[l, 1]).astype(BF16)
        ba = b_alpha[l].reshape(1, 2 * HK)
        wuq3 = w_uq[l].reshape(MLA_Q_RANK, MLA_HEADS, MLA_NOPE + MLA_ROPE)
        wuqn = wuq3[:, :, :MLA_NOPE].reshape(MLA_Q_RANK, MLA_HEADS * MLA_NOPE).astype(BF16)
        wuq_rope = wuq3[:, :, MLA_NOPE:]
        wuqr = wuq_rope.reshape(MLA_Q_RANK, MLA_HEADS * MLA_ROPE).astype(BF16)
        wuqrr = _rot_cols(wuq_rope).reshape(MLA_Q_RANK, MLA_HEADS * MLA_ROPE).astype(BF16)
        wukt = w_uk[l].reshape(MLA_KV_RANK, MLA_HEADS, MLA_NOPE).transpose(1, 2, 0).astype(BF16)
        wuvt = w_uv[l].reshape(MLA_KV_RANK, MLA_HEADS, MLA_DV).transpose(1, 2, 0).astype(BF16)

        gq, gk, gv, gg, la, qcat, ckvn, kr, za, zb = _inproj(
            x, mod, norm_mix[l][None, :], w1, wa, ba, q_norm[l][None, :], kv_norm[l][None, :],
            wuqn, wuqr, wuqrr, wukt, cq_tab, sq_tab, cos32, sin32)

        to_t = lambda s: s.transpose(0, 3, 1, 2).reshape(s.shape[0], GLA_DV, HK)
        zeros_ctx = jnp.zeros((BATCH, GLA_DV, HK), F32)
        s0f = jnp.concatenate([zeros_ctx, to_t(state_gla_fwd[:, l])], axis=0)
        s0b = jnp.concatenate([zeros_ctx, to_t(state_gla_bwd[:, l])], axis=0)
        o_f, o_r, sf, sb = _gla(gla_tab, gq, gk, gv, la, s0f, s0b, tri_f, tri_b, ones_blk)

        ckv_ctx = ckvn[:N_CTX].reshape(BATCH, SEQ, MLA_KV_RANK)
        kr_ctx = kr[:N_CTX].reshape(BATCH, SEQ, MLA_ROPE)
        ckv_lat = jnp.concatenate([cache_ckv[:, l], ckvn[N_CTX:].reshape(DEC_BATCH, DEC_SEQ, MLA_KV_RANK)], axis=1)
        kr_lat = jnp.concatenate([cache_krope[:, l], kr[N_CTX:].reshape(DEC_BATCH, DEC_SEQ, MLA_ROPE)], axis=1)

        def kv_operands(ckv_all, kr_all):
            kcat = jnp.concatenate([ckv_all, kr_all], axis=-1).astype(BF16)
            return kcat, ckv_all.astype(BF16).transpose(0, 2, 1)

        kc, vt = kv_operands(ckv_ctx, kr_ctx)
        ob_ctx = _attention(qcat, kc, vt, wuvt, tile0=0, tiles_per_seq=1, name="mla_ctx")
        kc, vt = kv_operands(ckv_lat, kr_lat)
        ob_lat = _attention(qcat, kc, vt, wuvt, tile0=CTX_TILES, tiles_per_seq=LAT_TILES_PER_SEQ, name="mla_lat")

        x1, h2x, lg = _merge(x, mod, o_f, o_r, gg, gla_norm[l][None, :], ob_ctx, ob_lat, za, zb, w_pa[l].astype(BF16), w_pb[l].astype(BF16),
                             w_o[l].astype(BF16), norm_ffn[l][None, :], wr1, wr2)
        h2x, grp, rank, _ = _route(lg.T, b_router[:, None], tri_route, h2x)

        n_tt = N_TOK // TE
        grp_t = grp[0].reshape(n_tt, TE)
        rank_t = rank[0].reshape(n_tt, TE)
        run_len = jnp.sum(grp_t[:, :, None] == jnp.arange(N_GROUPS, dtype=jnp.int32)[None, None, :],
                          axis=1, dtype=jnp.int32)
        run_len_al = (run_len + (RUN_ALIGN - 1)) // RUN_ALIGN * RUN_ALIGN
        ranks_before = jnp.cumsum(run_len, axis=0) - run_len
        cnt = jnp.sum(run_len_al, axis=0)
        padded = (cnt + (TE - 1)) // TE * TE
        g_end = jnp.cumsum(padded)
        g_start = g_end - padded
        run_src = jnp.cumsum(run_len_al, axis=1) - run_len_al
        run_dst = g_start[None, :] + jnp.cumsum(run_len_al, axis=0) - run_len_al
        of_group = lambda tbl: sum(jnp.where(grp_t == g, tbl[:, g:g + 1], 0) for g in range(N_GROUPS))
        lrow = (of_group(run_src) + rank_t - of_group(ranks_before)).astype(jnp.int32)
        tile_start = jnp.arange(N_SLOT_TILES, dtype=jnp.int32) * TE
        tile_grp = jnp.minimum(jnp.sum(tile_start[:, None] >= g_end[None, :], axis=1), N_GROUPS - 1).astype(jnp.int32)
        n_used = (g_end[N_GROUPS - 1:] // TE).astype(jnp.int32)
        pad_lo = jnp.concatenate([g_start + cnt, g_end[N_GROUPS - 1:]]).astype(jnp.int32)
        pad_hi = jnp.concatenate([g_end, jnp.full((1,), N_SLOTS, jnp.int32)]).astype(jnp.int32)

        run_len_al, run_src, run_dst = (a.reshape(-1).astype(jnp.int32) for a in (run_len_al, run_src, run_dst))
        xs = _dispatch(pad_lo, pad_hi, run_len_al, run_src, run_dst, lrow, h2x)
        ys = _moe(tile_grp, n_used, xs, w_exp_gate[l].astype(BF16), w_exp_up[l].astype(BF16),
                  w_exp_down[l].astype(BF16).reshape(N_GROUPS, EXPERTS_PER_GROUP * EXPERT_FF, D_MODEL),
                  w_sh_gate[l].astype(BF16), w_sh_up[l].astype(BF16), w_sh_down[l].astype(BF16))
        x = _combine(run_len_al, run_src, run_dst, lrow, ys, x1, mod, final_norm[None, :], final=(l == DEPTH - 1))

        ckvs.append(ckv_ctx)
        krs.append(kr_ctx)
        from_t = lambda s: s[:BATCH].reshape(BATCH, GLA_DV, GLA_HEADS, GLA_DK).transpose(0, 2, 3, 1)
        sfs.append(from_t(sf))
        sbs.append(from_t(sb))

    y_prompt = x[0].reshape(BATCH, SEQ, D_MODEL)
    y_sample = x[1].reshape(DEC_BATCH, DEC_SEQ, D_MODEL)
    return (y_prompt, y_sample, jnp.stack(ckvs, axis=1), jnp.stack(krs, axis=1),
            jnp.stack(sfs, axis=1), jnp.stack(sbs, axis=1))
```

```python
import functools
import itertools

import jax
import jax.numpy as jnp
from jax import lax
from jax.experimental import pallas as pl
from jax.experimental.pallas import tpu as pltpu

F32 = jnp.float32
BF16 = jnp.bfloat16

D_MODEL = 1024
BATCH = 16
SEQ = 256
DEPTH = 2
DEC_BATCH = 8
DEC_SEQ = 4096
PAST_LEN = 512
GRID_W = 64
GLA_HEADS = 4
GLA_DK = 64
GLA_DV = 128
GLA_GATE_RANK = 16
GLA_TAU = 16.0
GLA_CHUNK = 64
MLA_HEADS = 8
MLA_Q_RANK = 256
MLA_KV_RANK = 128
MLA_NOPE = 64
MLA_ROPE = 32
MLA_DV = 64
ROPE_BASE = 10000.0
N_EXPERTS = 16
N_GROUPS = 4
EXPERTS_PER_GROUP = 4
EXPERT_FF = 512
SHARED_FF = 512
N_MOD = 6
EPS = 1e-6

N_CTX = BATCH * SEQ
N_LAT = DEC_BATCH * DEC_SEQ
N_TOK = N_CTX + N_LAT
N_SEQ = BATCH + DEC_BATCH
TM = 256
N_TILES = N_TOK // TM
CTX_TILES = N_CTX // TM
LAT_TILES_PER_SEQ = DEC_SEQ // TM
TB = 512
N_TILES_B = N_TOK // TB
CTX_TILES_B = N_CTX // TB
LAT_TILES_B_PER_SEQ = DEC_SEQ // TB
HK = GLA_HEADS * GLA_DK
HV = GLA_HEADS * GLA_DV
QCAT = MLA_KV_RANK + MLA_ROPE
ATT_KEY_BLOCK = 1152
Q_SCALE =(MLA_NOPE + MLA_ROPE) ** -0.5 * 1.4426950408889634
TE = 512
RUN_ALIGN = 8
XP_ROWS = TE + N_GROUPS * RUN_ALIGN
N_SLOT_TILES = N_TOK // TE + N_GROUPS + -(-(N_TOK // TE) * N_GROUPS * (RUN_ALIGN - 1) // TE)
N_SLOTS = N_SLOT_TILES * TE
XW = D_MODEL + 128
ROUTE_W = 2048
VMEM_LIMIT = 56 * 1024 * 1024

C_GQ, C_GK, C_GV, C_GG, C_CQ, C_CKV, C_SMALL, C_ZA, C_ZB, C_END = (
    0, 256, 512, 1024, 1536, 1792, 1920, 2048, 3072, 4096)


def _sigmoid(x):
    return 1.0 / (1.0 + jnp.exp(-x))


def _rms(x, w):
    return x * lax.rsqrt(jnp.mean(x * x, axis=-1, keepdims=True) + EPS) * w


def _dot(a, b):
    return jnp.dot(a, b, preferred_element_type=F32)


def _dot_nt(a, b):
    return lax.dot_general(a, b, (((1,), (1,)), ((), ())), preferred_element_type=F32)


def _dot_tn(a, b):
    return lax.dot_general(a, b, (((0,), (0,)), ((), ())), preferred_element_type=F32)


def _tile_seq(i):
    return jnp.where(i < CTX_TILES_B, 0, 1 + (i - CTX_TILES_B) // LAT_TILES_B_PER_SEQ)


def _tile_pos(i):
    return jnp.where(i < CTX_TILES_B, LAT_TILES_B_PER_SEQ, (i - CTX_TILES_B) % LAT_TILES_B_PER_SEQ)


def _x_operands(x):
    if isinstance(x, tuple):
        xa, xb, off = x[0], x[1], 0
    else:
        xa, xb, off = x, x, CTX_TILES_B
    spec_a = pl.BlockSpec((TB, D_MODEL), lambda i: (jnp.minimum(i, CTX_TILES_B - 1), 0))
    spec_b = pl.BlockSpec((TB, D_MODEL), lambda i: (jnp.maximum(i - CTX_TILES_B, 0) + off, 0))
    return xa, xb, spec_a, spec_b


def _x_tile(xa_ref, xb_ref):
    return jnp.where(pl.program_id(0) < CTX_TILES_B, xa_ref[...], xb_ref[...])


def _mod_kernel(c_ref, w_ref, b_ref, o_ref):
    c = c_ref[...]
    sc = (c * _sigmoid(c)).astype(BF16)
    o_ref[...] = _dot(sc, w_ref[...]) + b_ref[...]


def _modulation(cond, w_mod, b_mod):
    nb = 1024
    return pl.pallas_call(
        _mod_kernel,
        out_shape=jax.ShapeDtypeStruct((DEPTH, 16, N_MOD * D_MODEL), F32),
        grid=(DEPTH, N_MOD * D_MODEL // nb),
        in_specs=[
            pl.BlockSpec((16, D_MODEL), lambda l, j: (0, 0)),
            pl.BlockSpec((None, D_MODEL, nb), lambda l, j: (l, 0, j)),
            pl.BlockSpec((None, 1, nb), lambda l, j: (l, 0, j)),
        ],
        out_specs=pl.BlockSpec((None, 16, nb), lambda l, j: (l, 0, j)),
        name="modulation",
    )(cond, w_mod, b_mod)


def _inproj_kernel(xa_ref, xb_ref, mod_ref, nw_ref, w1_ref, wa_ref, ba_ref, qnw_ref, kvnw_ref,
                   wuqn_ref, wuqr_ref, wuqrr_ref, wukt_ref, cq_ref, sq_ref, ck_ref, sk_ref,
                   gq_ref, gk_ref, gv_ref, gg_ref, la_ref, q_ref, ckv_ref, kr_ref, za_ref, zb_ref):
    x = _x_tile(xa_ref, xb_ref)
    mod = mod_ref[0]
    h = (_rms(x, nw_ref[...]) * (1.0 + mod[1:2, :]) + mod[0:1, :]).astype(BF16)

    def proj(lo, hi):
        return _dot(h, w1_ref[:, lo:hi])

    def q_heads(heads):
        for hd in heads:
            qa = _dot(qn[:, hd * MLA_NOPE:(hd + 1) * MLA_NOPE], wukt_ref[hd]) * Q_SCALE
            q_ref[hd, :, 0:MLA_KV_RANK] = qa.astype(BF16)
            q_ref[hd, :, MLA_KV_RANK:QCAT] = qr[:, hd * MLA_ROPE:(hd + 1) * MLA_ROPE].astype(BF16)

    cqn = _rms(proj(C_CQ, C_CKV), qnw_ref[...]).astype(BF16)
    small = proj(C_SMALL, C_ZA)
    gq_ref[...] = (proj(C_GQ, C_GK) * (GLA_DK ** -0.5)).astype(BF16)
    qn = _dot(cqn, wuqn_ref[...]).astype(BF16)
    lin = _dot(small.astype(BF16), wa_ref[...]) + ba_ref[...]
    gk_ref[...] = proj(C_GK, C_GV).astype(BF16)
    qr = (_dot(cqn, wuqr_ref[...]) * cq_ref[...]
          + _dot(cqn, wuqrr_ref[...]) * sq_ref[...])
    la_ref[...] = (jnp.minimum(lin, 0.0) - jnp.log(1.0 + jnp.exp(-jnp.abs(lin)))) * (1.0 / GLA_TAU)
    kr_ref[...] = small[:, 32:64] * ck_ref[...] + small[:, 64:96] * sk_ref[...]
    gv_ref[...] = proj(C_GV, C_GG).astype(BF16)
    q_heads(range(0, MLA_HEADS // 2))
    ckv_ref[...] = _rms(proj(C_CKV, C_SMALL), kvnw_ref[...])
    gg_ref[...] = proj(C_GG, C_CQ).astype(BF16)
    q_heads(range(MLA_HEADS // 2, MLA_HEADS))
    za_ref[...] = _sigmoid(proj(C_ZA, C_ZB)).astype(BF16)
    zb_ref[...] = _sigmoid(proj(C_ZB, C_END)).astype(BF16)


def _inproj(x, mod, nw, w1, wa, ba, qnw, kvnw, wuqn, wuqr, wuqrr, wukt, cq, sq, ck, sk):
    full = lambda a: pl.BlockSpec(a.shape, lambda i: (0,) * a.ndim)
    tok = lambda w: pl.BlockSpec((TB, w), lambda i: (i, 0))
    pos = lambda w: pl.BlockSpec((TB, w), lambda i: (_tile_pos(i), 0))
    xa, xb, spec_a, spec_b = _x_operands(x)
    out_shape = [
        jax.ShapeDtypeStruct((N_TOK, HK), BF16),
        jax.ShapeDtypeStruct((N_TOK, HK), BF16),
        jax.ShapeDtypeStruct((N_TOK, HV), BF16),
        jax.ShapeDtypeStruct((N_TOK, HV), BF16),
        jax.ShapeDtypeStruct((N_TOK, 2 * HK), F32),
        jax.ShapeDtypeStruct((MLA_HEADS, N_TOK, QCAT), BF16),
        jax.ShapeDtypeStruct((N_TOK, MLA_KV_RANK), F32),
        jax.ShapeDtypeStruct((N_TOK, MLA_ROPE), F32),
        jax.ShapeDtypeStruct((N_TOK, D_MODEL), BF16),
        jax.ShapeDtypeStruct((N_TOK, D_MODEL), BF16),
    ]
    out_specs = [tok(HK), tok(HK), tok(HV), tok(HV), tok(2 * HK),
                 pl.BlockSpec((MLA_HEADS, TB, QCAT), lambda i: (0, i, 0)),
                 tok(MLA_KV_RANK), tok(MLA_ROPE), tok(D_MODEL), tok(D_MODEL)]
    in_specs = [spec_a, spec_b,
                pl.BlockSpec((1, N_MOD, D_MODEL), lambda i: (_tile_seq(i), 0, 0)),
                full(nw), full(w1), full(wa), full(ba), full(qnw), full(kvnw),
                full(wuqn), full(wuqr), full(wuqrr), full(wukt),
                pos(HK), pos(HK), pos(MLA_ROPE), pos(MLA_ROPE)]
    return pl.pallas_call(
        _inproj_kernel, out_shape=out_shape, grid=(N_TILES_B,),
        in_specs=in_specs, out_specs=out_specs,
        compiler_params=pltpu.CompilerParams(dimension_semantics=("arbitrary",),
                                             vmem_limit_bytes=VMEM_LIMIT),
        name="inproj",
    )(xa, xb, mod, nw, w1, wa, ba, qnw, kvnw, wuqn, wuqr, wuqrr, wukt, cq, sq, ck, sk)


def _gla_tile(q_ref, k_ref, v_ref, la_ref, st_ref, o_ref, tri, ones_blk, *, reverse):
    n_chunks = TM // GLA_CHUNK
    la = la_ref[...]
    la_hi = la.astype(BF16)
    la_lo = (la - la_hi.astype(F32)).astype(BF16)
    cum = _dot(tri, la_hi) + _dot(tri, la_lo)
    tot = _dot(ones_blk, la_hi) + _dot(ones_blk, la_lo)
    yield
    q = q_ref[...].astype(F32)
    k = k_ref[...].astype(F32)
    qd = q * jnp.exp(cum)
    kin = (k * jnp.exp(-cum)).astype(BF16)
    kout = k * jnp.exp(tot - cum)
    decay = jnp.exp(tot)
    lane = lax.broadcasted_iota(jnp.int32, (TM, HK), 1)
    head_of_lane = lane // GLA_DK
    qd_h = [jnp.where(head_of_lane == hd, qd, 0.0).astype(BF16) for hd in range(GLA_HEADS)]
    kout_h = [jnp.where(head_of_lane == hd, kout, 0.0).astype(BF16) for hd in range(GLA_HEADS)]
    row = lax.broadcasted_iota(jnp.int32, (TM, TM), 0)
    col = lax.broadcasted_iota(jnp.int32, (TM, TM), 1)
    keep = (row // GLA_CHUNK == col // GLA_CHUNK) & ((col >= row) if reverse else (col <= row))
    head_vs = [slice(hd * GLA_DV, (hd + 1) * GLA_DV) for hd in range(GLA_HEADS)]
    yield

    att_all = _dot_nt(jnp.concatenate(qd_h, axis=0), kin)
    yield
    o_intra = [_dot(jnp.where(keep, att_all[hd * TM:(hd + 1) * TM], 0.0).astype(BF16), v_ref[:, head_vs[hd]])
               for hd in range(GLA_HEADS)]
    yield

    st = st_ref[...]
    for ci in range(n_chunks):
        c = (n_chunks - 1 - ci) if reverse else ci
        rows = slice(c * GLA_CHUNK, (c + 1) * GLA_CHUNK)
        q_stack = jnp.concatenate([qd_h[hd][rows] for hd in range(GLA_HEADS)], axis=0)
        o_inter = _dot_nt(q_stack, st.astype(BF16))
        for hd in range(GLA_HEADS):
            o = o_intra[hd][rows] + o_inter[hd * GLA_CHUNK:(hd + 1) * GLA_CHUNK]
            o_ref[rows, head_vs[hd]] = o.astype(o_ref.dtype)
        v_stack = jnp.concatenate([v_ref[rows, head_vs[hd]] for hd in range(GLA_HEADS)], axis=0)
        k_stack = jnp.concatenate([kout_h[hd][rows] for hd in range(GLA_HEADS)], axis=0)
        st = st * decay[c * GLA_CHUNK:c * GLA_CHUNK + 1, :] + _dot_tn(v_stack, k_stack)
        yield
    st_ref[...] = st


def _gla_kernel(blkf_ref, blkb_ref, first_ref, seq_ref,
                qf_ref, kf_ref, vf_ref, laf_ref, qb_ref, kb_ref, vb_ref, lab_ref, s0f_ref, s0b_ref,
                trif_ref, trib_ref, ones_ref,
                of_ref, ob_ref, sff_ref, sfb_ref, stf_ref, stb_ref):
    step = pl.program_id(0)

    @pl.when(first_ref[step] == 1)
    def _():
        stf_ref[...] = s0f_ref[0]
        stb_ref[...] = s0b_ref[0]

    fwd = _gla_tile(qf_ref, kf_ref, vf_ref, laf_ref, stf_ref, of_ref, trif_ref[...], ones_ref[...], reverse=False)
    bwd = _gla_tile(qb_ref, kb_ref, vb_ref, lab_ref, stb_ref, ob_ref, trib_ref[...], ones_ref[...], reverse=True)
    for _ in itertools.zip_longest(fwd, bwd):
        pass
    sff_ref[0] = stf_ref[...]
    sfb_ref[0] = stb_ref[...]


def _gla(tables, q, k, v, la, s0f, s0b, trif, trib, ones_blk):
    blkf, blkb, first, seq = tables
    fwd = lambda w, j=0: pl.BlockSpec((TM, w), lambda s, bf, bb, f, q_: (bf[s], j))
    bwd = lambda w, j=0: pl.BlockSpec((TM, w), lambda s, bf, bb, f, q_: (bb[s], j))
    per_seq = pl.BlockSpec((1, GLA_DV, HK), lambda s, bf, bb, f, q_: (q_[s], 0, 0))
    const = pl.BlockSpec((TM, TM), lambda s, bf, bb, f, q_: (0, 0))
    return pl.pallas_call(
        _gla_kernel,
        out_shape=[jax.ShapeDtypeStruct((N_TOK, HV), BF16), jax.ShapeDtypeStruct((N_TOK, HV), BF16),
                   jax.ShapeDtypeStruct((N_SEQ, GLA_DV, HK), F32), jax.ShapeDtypeStruct((N_SEQ, GLA_DV, HK), F32)],
        grid_spec=pltpu.PrefetchScalarGridSpec(
            num_scalar_prefetch=4, grid=(N_TILES,),
            in_specs=[fwd(HK), fwd(HK), fwd(HV), fwd(HK, 0), bwd(HK), bwd(HK), bwd(HV), bwd(HK, 1),
                      per_seq, per_seq, const, const, const],
            out_specs=[fwd(HV), bwd(HV), per_seq, per_seq],
            scratch_shapes=[pltpu.VMEM((GLA_DV, HK), F32), pltpu.VMEM((GLA_DV, HK), F32)]),
        compiler_params=pltpu.CompilerParams(dimension_semantics=("arbitrary",)),
        name="gla",
    )(blkf, blkb, first, seq, q, k, v, la, q, k, v, la, s0f, s0b, trif, trib, ones_blk)


def _attn_kernel(q_ref, k_ref, vt_ref, wuvt_ref, o_ref, s_ref, p_ref, ot_ref, vh_ref):
    @pl.when(pl.program_id(1) == 0)
    def _():
        for hd in range(MLA_HEADS):
            vh_ref[hd] = _dot(wuvt_ref[hd], vt_ref[0]).astype(BF16)

    s_len = k_ref.shape[1]
    kb = min(s_len, ATT_KEY_BLOCK)
    n_kb = s_len // kb

    def scores(hd, j, m):
        keys = slice(j * kb, (j + 1) * kb)
        s = _dot_nt(k_ref[0, keys, :], q_ref[hd])
        s_ref[hd % 2, keys, :] = s
        m_blk = jnp.max(s, axis=0, keepdims=True)
        return m_blk if m is None else jnp.maximum(m, m_blk)

    m = None
    for j in range(n_kb):
        m = scores(0, j, m)
    l_prev = None
    for hd in range(MLA_HEADS + 1):
        m_next, acc, l = None, None, None
        for j in range(n_kb):
            keys = slice(j * kb, (j + 1) * kb)
            if hd + 1 < MLA_HEADS:
                m_next = scores(hd + 1, j, m_next)
            if hd < MLA_HEADS:
                p = jnp.exp2(s_ref[hd % 2, keys, :] - m)
                l_blk = jnp.sum(p, axis=0, keepdims=True)
                l = l_blk if l is None else l + l_blk
                p_ref[hd % 2, keys, :] = p.astype(BF16)
            if hd > 0:
                part = _dot(vh_ref[hd - 1, :, keys], p_ref[(hd - 1) % 2, keys, :])
                acc = part if acc is None else acc + part
        if hd > 0:
            ot_ref[hd - 1] = acc / l_prev
        m, l_prev = m_next, l
    o_ref[...] = ot_ref[...].reshape(MLA_HEADS * MLA_DV, TM).T.astype(o_ref.dtype)


def _attention(q, kcat, vt, wuvt, *, tile0, tiles_per_seq, name):
    n_seq, s_len, _ = kcat.shape
    return pl.pallas_call(
        _attn_kernel,
        out_shape=jax.ShapeDtypeStruct((n_seq * tiles_per_seq * TM, MLA_HEADS * MLA_DV), BF16),
        grid=(n_seq, tiles_per_seq),
        in_specs=[
            pl.BlockSpec((MLA_HEADS, TM, QCAT), lambda b, i: (0, tile0 + b * tiles_per_seq + i, 0)),
            pl.BlockSpec((1, s_len, QCAT), lambda b, i: (b, 0, 0)),
            pl.BlockSpec((1, MLA_KV_RANK, s_len), lambda b, i: (b, 0, 0)),
            pl.BlockSpec(wuvt.shape, lambda b, i: (0, 0, 0)),
        ],
        out_specs=pl.BlockSpec((TM, MLA_HEADS * MLA_DV), lambda b, i: (b * tiles_per_seq + i, 0)),
        scratch_shapes=[pltpu.VMEM((2, s_len, TM), F32), pltpu.VMEM((2, s_len, TM), BF16),
                        pltpu.VMEM((MLA_HEADS, MLA_DV, TM), F32),
                        pltpu.VMEM((MLA_HEADS, MLA_DV, s_len), BF16)],
        compiler_params=pltpu.CompilerParams(dimension_semantics=("arbitrary", "arbitrary"),
                                             vmem_limit_bytes=VMEM_LIMIT),
        name=name,
    )(q, kcat, vt, wuvt)


def _merge_kernel(xa_ref, xb_ref, mod_ref, of_ref, ob_ref, gg_ref, gnw_ref, obc_ref, obl_ref, za_ref, zb_ref,
                  wpa_ref, wpb_ref, wo_ref, nw_ref, wr1_ref, wr2_ref, x1_ref, h2x_ref, lg_ref):
    mod = mod_ref[0]
    o_sum = of_ref[...].astype(F32) + ob_ref[...].astype(F32)
    gate = gg_ref[...].astype(F32)
    gate = gate * _sigmoid(gate)
    oa = jnp.concatenate(
        [(_rms(o_sum[:, hd * GLA_DV:(hd + 1) * GLA_DV], gnw_ref[...])
          * gate[:, hd * GLA_DV:(hd + 1) * GLA_DV]).astype(BF16) for hd in range(GLA_HEADS)], axis=1)
    ob = jnp.where(pl.program_id(0) < CTX_TILES_B, obc_ref[...], obl_ref[...])
    y = (za_ref[...].astype(F32) * _dot(oa, wpa_ref[...])
         + zb_ref[...].astype(F32) * _dot(ob, wpb_ref[...]))
    out = _dot(y.astype(BF16), wo_ref[...])
    x1 = _x_tile(xa_ref, xb_ref) + mod[2:3, :] * out
    x1_ref[...] = x1
    h2 = _rms(x1, nw_ref[...]) * (1.0 + mod[4:5, :]) + mod[3:4, :]
    h2_hi = h2.astype(BF16)
    h2_lo = (h2 - h2_hi.astype(F32)).astype(BF16)
    h2x_ref[:, 0:D_MODEL] = h2
    h2x_ref[:, D_MODEL:XW] = jnp.zeros((TB, XW - D_MODEL), F32)
    d1 = _dot(h2_hi, wr1_ref[...])
    d2 = _dot(h2_lo, wr2_ref[...])
    lg_ref[...] = (d1[:, 0:N_EXPERTS] + d1[:, N_EXPERTS:2 * N_EXPERTS]) + d2[:, 0:N_EXPERTS]


def _merge(x, mod, o_f, o_b, gg, gnw, ob_ctx, ob_lat, za, zb, wpa, wpb, wo, nw, wr1, wr2):
    full = lambda a: pl.BlockSpec(a.shape, lambda i: (0,) * a.ndim)
    tok = lambda w: pl.BlockSpec((TB, w), lambda i: (i, 0))
    hb = MLA_HEADS * MLA_DV
    ctx_spec = pl.BlockSpec((TB, hb), lambda i: (jnp.minimum(i, CTX_TILES_B - 1), 0))
    lat_spec = pl.BlockSpec((TB, hb), lambda i: (jnp.maximum(i - CTX_TILES_B, 0), 0))
    xa, xb, spec_a, spec_b = _x_operands(x)
    return pl.pallas_call(
        _merge_kernel,
        out_shape=[jax.ShapeDtypeStruct((N_TOK, D_MODEL), F32),
                   jax.ShapeDtypeStruct((N_TOK, XW), F32),
                   jax.ShapeDtypeStruct((N_TOK, N_EXPERTS), F32)],
        grid=(N_TILES_B,),
        in_specs=[spec_a, spec_b, pl.BlockSpec((1, N_MOD, D_MODEL), lambda i: (_tile_seq(i), 0, 0)),
                  tok(HV), tok(HV), tok(HV), full(gnw), ctx_spec, lat_spec, tok(D_MODEL), tok(D_MODEL),
                  full(wpa), full(wpb), full(wo), full(nw), full(wr1), full(wr2)],
        out_specs=[tok(D_MODEL), tok(XW), tok(N_EXPERTS)],
        compiler_params=pltpu.CompilerParams(dimension_semantics=("arbitrary",),
                                             vmem_limit_bytes=VMEM_LIMIT),
        name="merge",
    )(xa, xb, mod, o_f, o_b, gg, gnw, ob_ctx, ob_lat, za, zb, wpa, wpb, wo, nw, wr1, wr2)


def _route_kernel(lg_ref, b_ref, tri_ref, h2x_in_ref, h2x_ref, grp_ref, rank_ref, cnt_ref, carry_ref):
    del h2x_in_ref
    step = pl.program_id(0)

    @pl.when(step == 0)
    def _():
        carry_ref[...] = jnp.zeros_like(carry_ref)

    aff = _sigmoid(lg_ref[...])
    biased = aff + b_ref[...]
    row = lambda a, e: a[e:e + 1, :]
    best = None
    sel = None
    for g in range(N_GROUPS):
        b = [row(biased, g * EXPERTS_PER_GROUP + i) for i in range(EXPERTS_PER_GROUP)]
        score = None
        for i in range(EXPERTS_PER_GROUP):
            for j in range(i + 1, EXPERTS_PER_GROUP):
                pair = b[i] + b[j]
                score = pair if score is None else jnp.maximum(score, pair)
        if g == 0:
            best, sel = score, jnp.zeros_like(score, dtype=jnp.int32)
        else:
            better = score > best
            best = jnp.where(better, score, best)
            sel = jnp.where(better, g, sel)
    cb, ca = [], []
    for i in range(EXPERTS_PER_GROUP):
        vb = row(biased, i)
        va = row(aff, i)
        for g in range(1, N_GROUPS):
            vb = jnp.where(sel == g, row(biased, g * EXPERTS_PER_GROUP + i), vb)
            va = jnp.where(sel == g, row(aff, g * EXPERTS_PER_GROUP + i), va)
        cb.append(vb)
        ca.append(va)
    picked = []
    for i in range(EXPERTS_PER_GROUP):
        rank = jnp.zeros_like(sel)
        for j in range(EXPERTS_PER_GROUP):
            if j == i:
                continue
            ahead = (cb[j] >= cb[i]) if j < i else (cb[j] > cb[i])
            rank = rank + ahead.astype(jnp.int32)
        picked.append(rank < 2)
    denom = None
    for i in range(EXPERTS_PER_GROUP):
        term = jnp.where(picked[i], ca[i], 0.0)
        denom = term if denom is None else denom + term
    cw = [jnp.where(picked[i], ca[i] / denom, 0.0) for i in range(EXPERTS_PER_GROUP)]
    cw_t = jnp.concatenate(cw + [jnp.zeros((128 - EXPERTS_PER_GROUP, ROUTE_W), F32)], axis=0)
    h2x_ref[...] = cw_t.T
    grp_ref[...] = sel

    onehot = jnp.concatenate([(sel == g).astype(F32) for g in range(N_GROUPS)]
                             + [jnp.zeros((8 - N_GROUPS, ROUTE_W), F32)], axis=0)
    carry = carry_ref[...]
    for c in range(ROUTE_W // 256):
        lanes = slice(c * 256, (c + 1) * 256)
        oh = onehot[:, lanes]
        before = _dot(oh.astype(BF16), tri_ref[...]) + carry
        sel_c = sel[:, lanes]
        r = before[N_GROUPS - 1:N_GROUPS, :]
        for g in range(N_GROUPS - 2, -1, -1):
            r = jnp.where(sel_c == g, before[g:g + 1, :], r)
        rank_ref[:, lanes] = r.astype(jnp.int32)
        carry = carry + jnp.sum(oh, axis=1, keepdims=True)
    carry_ref[...] = carry
    cnt_ref[...] = carry[:, 0:128]


def _route(lg_t, b_router, tri, h2x):
    return pl.pallas_call(
        _route_kernel,
        out_shape=[jax.ShapeDtypeStruct((N_TOK, XW), F32),
                   jax.ShapeDtypeStruct((1, N_TOK), jnp.int32),
                   jax.ShapeDtypeStruct((1, N_TOK), jnp.int32),
                   jax.ShapeDtypeStruct((8, 128), F32)],
        grid=(N_TOK // ROUTE_W,),
        in_specs=[pl.BlockSpec((N_EXPERTS, ROUTE_W), lambda i: (0, i)),
                  pl.BlockSpec((N_EXPERTS, 1), lambda i: (0, 0)),
                  pl.BlockSpec((256, 256), lambda i: (0, 0)),
                  pl.BlockSpec(memory_space=pl.ANY)],
        out_specs=[pl.BlockSpec((ROUTE_W, XW - D_MODEL), lambda i: (i, D_MODEL // (XW - D_MODEL))),
                   pl.BlockSpec((1, ROUTE_W), lambda i: (0, i)),
                   pl.BlockSpec((1, ROUTE_W), lambda i: (0, i)),
                   pl.BlockSpec((8, 128), lambda i: (0, 0))],
        scratch_shapes=[pltpu.VMEM((8, 256), F32)],
        input_output_aliases={3: 0},
        compiler_params=pltpu.CompilerParams(dimension_semantics=("arbitrary",)),
        name="route",
    )(lg_t, b_router, tri, h2x)


def _dispatch_kernel(lo_ref, hi_ref, cnt_ref, src_ref, dst_ref, lrow_ref, x_ref, xs_ref, xp_ref, zrow_ref, sem, zsem):
    t = pl.program_id(0)
    row = lax.broadcasted_iota(jnp.int32, (XP_ROWS, TE), 0)
    perm = jnp.where(row == lrow_ref[0], 1.0, 0.0).astype(BF16)
    x = x_ref[...]
    xp_ref[:, 0:D_MODEL] = _dot(perm, x[:, 0:D_MODEL].astype(BF16))
    cw = x[:, D_MODEL:XW]
    cw_hi = cw.astype(BF16)
    cw_lo = (cw - cw_hi.astype(F32)).astype(BF16)
    xp_ref[:, D_MODEL:XW] = _dot(perm, cw_hi) + _dot(perm, cw_lo)

    def make_copy(row, slot_row, size):
        return pltpu.make_async_copy(xp_ref.at[pl.ds(row, size), :], xs_ref.at[pl.ds(slot_row, size), :], sem)

    _run_pieces(cnt_ref, src_ref, dst_ref, t, make_copy, lambda cp: cp.start())
    _run_pieces(cnt_ref, src_ref, dst_ref, t, make_copy, lambda cp: cp.wait())

    @pl.when(t == pl.num_programs(0) - 1)
    def _():
        zrow_ref[...] = jnp.zeros_like(zrow_ref)

        def zero_copy(slot):
            return pltpu.make_async_copy(zrow_ref.at[pl.ds(0, 1), :], xs_ref.at[pl.ds(slot, 1), :], zsem)

        def z_issue(slot, c):
            zero_copy(slot).start()
            return c

        def z_drain(slot, c):
            zero_copy(slot).wait()
            return c

        for g in range(N_GROUPS + 1):
            lax.fori_loop(lo_ref[g], hi_ref[g], z_issue, 0)
        for g in range(N_GROUPS + 1):
            lax.fori_loop(lo_ref[g], hi_ref[g], z_drain, 0)


def _dispatch(pad_lo, pad_hi, run_len, run_src, run_dst, lrow, h2x):
    n_tiles = N_TOK // TE
    return pl.pallas_call(
        _dispatch_kernel,
        out_shape=jax.ShapeDtypeStruct((N_SLOTS, XW), F32),
        grid_spec=pltpu.PrefetchScalarGridSpec(
            num_scalar_prefetch=5, grid=(n_tiles,),
            in_specs=[pl.BlockSpec((1, 1, TE), lambda t, *_: (t, 0, 0)),
                      pl.BlockSpec((TE, XW), lambda t, *_: (t, 0))],
            out_specs=pl.BlockSpec(memory_space=pl.ANY),
            scratch_shapes=[pltpu.VMEM((XP_ROWS, XW), F32), pltpu.VMEM((8, XW), F32),
                            pltpu.SemaphoreType.DMA, pltpu.SemaphoreType.DMA]),
        compiler_params=pltpu.CompilerParams(dimension_semantics=("arbitrary",)),
        name="moe_dispatch",
    )(pad_lo, pad_hi, run_len, run_src, run_dst, lrow.reshape(n_tiles, 1, TE), h2x)


def _run_pieces(len_ref, a_ref, b_ref, tile, make_copy, action):
    for g in range(N_GROUPS):
        n = len_ref[tile * N_GROUPS + g]
        a0 = a_ref[tile * N_GROUPS + g]
        b0 = b_ref[tile * N_GROUPS + g]
        off = 0
        size = TE
        while size >= RUN_ALIGN:
            take = (n & size) != 0

            @pl.when(take)
            def _(off=off, size=size):
                action(make_copy(pl.multiple_of(a0 + off, RUN_ALIGN), pl.multiple_of(b0 + off, RUN_ALIGN), size))

            off = off + jnp.where(take, size, 0)
            size //= 2


def _combine_kernel(len_ref, src_ref, dst_ref, lrow_ref, ys_ref, x1_ref, mod_ref, fw_ref, *rest, final):
    if final:
        yc_ref, yl_ref, ybuf_ref, sems = rest
    else:
        o_ref, ybuf_ref, sems = rest
    t = pl.program_id(0)
    slot = t % 2

    def fetch(tile, buf, action):
        def make_copy(row, slot_row, size):
            return pltpu.make_async_copy(ys_ref.at[pl.ds(slot_row, size), :],
                                         ybuf_ref.at[buf, pl.ds(row, size), :], sems.at[buf])
        _run_pieces(len_ref, src_ref, dst_ref, tile, make_copy, action)

    @pl.when(t == 0)
    def _():
        ybuf_ref[...] = jnp.zeros_like(ybuf_ref)
        fetch(t, 0, lambda cp: cp.start())

    @pl.when(t + 1 < pl.num_programs(0))
    def _():
        fetch(t + 1, 1 - slot, lambda cp: cp.start())

    fetch(t, slot, lambda cp: cp.wait())
    row = lax.broadcasted_iota(jnp.int32, (XP_ROWS, TE), 0)
    perm = jnp.where(row == lrow_ref[0], 1.0, 0.0).astype(BF16)
    yb = ybuf_ref[slot]
    y_hi = yb.astype(BF16)
    y_lo = (yb - y_hi.astype(F32)).astype(BF16)
    rows = _dot_tn(perm, y_hi) + _dot_tn(perm, y_lo)
    x2 = x1_ref[...] + mod_ref[0][5:6, :] * rows
    if not final:
        o_ref[...] = x2
        return
    y = _rms(x2, fw_ref[...])
    is_ctx = pl.program_id(0) < N_CTX // TE

    @pl.when(is_ctx)
    def _():
        yc_ref[...] = y

    @pl.when(jnp.logical_not(is_ctx))
    def _():
        yl_ref[...] = y


def _combine(run_len, run_src, run_dst, lrow, ys, x1, mod, fw, *, final):
    per_seq = DEC_SEQ // TE
    ctx_tiles = N_CTX // TE
    seq_of = lambda i: jnp.where(i < ctx_tiles, 0, 1 + (i - ctx_tiles) // per_seq)
    n_tiles = N_TOK // TE
    if final:
        out_shape = [jax.ShapeDtypeStruct((N_CTX, D_MODEL), F32), jax.ShapeDtypeStruct((N_LAT, D_MODEL), F32)]
        out_specs = [pl.BlockSpec((TE, D_MODEL), lambda t, *_: (jnp.minimum(t, ctx_tiles - 1), 0)),
                     pl.BlockSpec((TE, D_MODEL), lambda t, *_: (jnp.maximum(t - ctx_tiles, 0), 0))]
    else:
        out_shape = jax.ShapeDtypeStruct((N_TOK, D_MODEL), F32)
        out_specs = pl.BlockSpec((TE, D_MODEL), lambda t, *_: (t, 0))
    return pl.pallas_call(
        functools.partial(_combine_kernel, final=final),
        out_shape=out_shape,
        grid_spec=pltpu.PrefetchScalarGridSpec(
            num_scalar_prefetch=3, grid=(n_tiles,),
            in_specs=[pl.BlockSpec((1, 1, TE), lambda t, *_: (t, 0, 0)),
                      pl.BlockSpec(memory_space=pl.ANY),
                      pl.BlockSpec((TE, D_MODEL), lambda t, *_: (t, 0)),
                      pl.BlockSpec((1, N_MOD, D_MODEL), lambda t, *_: (seq_of(t), 0, 0)),
                      pl.BlockSpec((1, D_MODEL), lambda t, *_: (0, 0))],
            out_specs=out_specs,
            scratch_shapes=[pltpu.VMEM((2, XP_ROWS, D_MODEL), F32), pltpu.SemaphoreType.DMA((2,))]),
        compiler_params=pltpu.CompilerParams(dimension_semantics=("arbitrary",),
                                             vmem_limit_bytes=VMEM_LIMIT),
        name="moe_combine",
    )(run_len, run_src, run_dst, lrow.reshape(n_tiles, 1, TE), ys, x1, mod, fw)


def _moe_kernel(tg_ref, used_ref, xs_ref, wg_ref, wu_ref, wd_ref, wsg_ref, wsu_ref, wsd_ref, o_ref):
    t = pl.program_id(0)

    @pl.when(t < used_ref[0])
    def _():
        x = xs_ref[:, 0:D_MODEL].astype(BF16)
        cw = xs_ref[:, D_MODEL:XW]
        acts = []
        for j in range(EXPERTS_PER_GROUP):
            gate = _dot(x, wg_ref[j])
            up = _dot(x, wu_ref[j])
            acts.append((gate * _sigmoid(gate) * up * cw[:, j:j + 1]).astype(BF16))
        y = _dot(jnp.concatenate(acts, axis=1), wd_ref[0])
        gate = _dot(x, wsg_ref[...])
        up = _dot(x, wsu_ref[...])
        o_ref[...] = y + _dot((gate * _sigmoid(gate) * up).astype(BF16), wsd_ref[...])

    @pl.when(t >= used_ref[0])
    def _():
        o_ref[...] = jnp.zeros_like(o_ref)


def _moe(tile_grp, n_used, xs, wg, wu, wd, wsg, wsu, wsd):
    full = lambda a: pl.BlockSpec(a.shape, lambda t, tg, nu: (0,) * a.ndim)
    grp_w = lambda a: pl.BlockSpec((EXPERTS_PER_GROUP,) + a.shape[1:], lambda t, tg, nu: (tg[t], 0, 0))
    return pl.pallas_call(
        _moe_kernel,
        out_shape=jax.ShapeDtypeStruct((N_SLOTS, D_MODEL), F32),
        grid_spec=pltpu.PrefetchScalarGridSpec(
            num_scalar_prefetch=2, grid=(N_SLOT_TILES,),
            in_specs=[pl.BlockSpec((TE, XW), lambda t, tg, nu: (jnp.minimum(t, nu[0] - 1), 0)),
                      grp_w(wg), grp_w(wu),
                      pl.BlockSpec((1,) + wd.shape[1:], lambda t, tg, nu: (tg[t], 0, 0)),
                      full(wsg), full(wsu), full(wsd)],
            out_specs=pl.BlockSpec((TE, D_MODEL), lambda t, tg, nu: (t, 0))),
        compiler_params=pltpu.CompilerParams(dimension_semantics=("arbitrary",),
                                             vmem_limit_bytes=VMEM_LIMIT),
        name="moe_experts",
    )(tile_grp, n_used, xs, wg, wu, wd, wsg, wsu, wsd)


def _rope_tables():
    rows = DEC_SEQ // GRID_W
    r = jnp.repeat(jnp.arange(rows, dtype=F32), GRID_W)
    col = jnp.tile(jnp.arange(GRID_W, dtype=F32), rows)
    n_freq = MLA_ROPE // 4
    inv = ROPE_BASE ** (-jnp.arange(n_freq, dtype=F32) / n_freq)
    ang = jnp.stack([r[:, None] * inv, col[:, None] * inv], axis=1)
    expand = lambda t: jnp.broadcast_to(t[:, :, None, :], (DEC_SEQ, 2, 2, n_freq)).reshape(DEC_SEQ, MLA_ROPE)
    cos = jnp.concatenate([expand(jnp.cos(ang)), jnp.ones((TB, MLA_ROPE), F32)], axis=0)
    sin = jnp.concatenate([expand(jnp.sin(ang)), jnp.zeros((TB, MLA_ROPE), F32)], axis=0)
    return cos, sin


def _rot_cols(w):
    shp = w.shape
    w4 = w.reshape(shp[:-1] + (2, 2, MLA_ROPE // 4))
    return jnp.stack([-w4[..., 1, :], w4[..., 0, :]], axis=-2).reshape(shp)


def _gla_tables():
    blk_f, blk_b, first, seq = [], [], [], []
    for s in range(N_SEQ):
        if s < BATCH:
            tiles = [s]
        else:
            base = CTX_TILES + (s - BATCH) * LAT_TILES_PER_SEQ
            tiles = list(range(base, base + LAT_TILES_PER_SEQ))
        for n, t in enumerate(tiles):
            blk_f.append(t)
            blk_b.append(tiles[len(tiles) - 1 - n])
            first.append(1 if n == 0 else 0)
            seq.append(s)
    as_i32 = lambda v: jnp.asarray(v, dtype=jnp.int32)
    return as_i32(blk_f), as_i32(blk_b), as_i32(first), as_i32(seq)


def kernel(x_prompt, x_sample, cache_ckv, cache_krope, state_gla_fwd, state_gla_bwd, c, c_ctx, w_mod, b_mod, norm_mix, norm_ffn, w_in, w_alpha, b_alpha, gla_norm, q_norm, kv_norm, w_uq, w_uk, w_uv, w_pa, w_pb, w_o, w_router, b_router, w_exp_gate, w_exp_up, w_exp_down, w_sh_gate, w_sh_up, w_sh_down, final_norm):
    x = (x_prompt.reshape(N_CTX, D_MODEL), x_sample.reshape(N_LAT, D_MODEL))

    cond = jnp.concatenate([c_ctx[None, :], c, jnp.zeros((16 - 1 - DEC_BATCH, D_MODEL), F32)], axis=0)
    mod_all = _modulation(cond, w_mod.astype(BF16), b_mod[:, None, :])
    mod_all = mod_all.reshape(DEPTH, 16, N_MOD, D_MODEL)

    cos32, sin32 = _rope_tables()
    cq_tab = jnp.tile(cos32, (1, MLA_HEADS)) * Q_SCALE
    sq_tab = jnp.tile(sin32, (1, MLA_HEADS)) * Q_SCALE

    idx256 = jnp.arange(TM)
    same_chunk = (idx256[:, None] // GLA_CHUNK) == (idx256[None, :] // GLA_CHUNK)
    tri_f = (same_chunk & (idx256[None, :] <= idx256[:, None])).astype(BF16)
    tri_b = (same_chunk & (idx256[None, :] >= idx256[:, None])).astype(BF16)
    ones_blk = same_chunk.astype(BF16)
    gla_tab = _gla_tables()
    tri_route = (idx256[:, None] < idx256[None, :]).astype(BF16)

    wr_hi = w_router.astype(BF16)
    wr_lo = (w_router - wr_hi.astype(F32)).astype(BF16)
    zpad = jnp.zeros((D_MODEL, 128 - 2 * N_EXPERTS), BF16)
    wr1 = jnp.concatenate([wr_hi, wr_lo, zpad], axis=1)
    wr2 = jnp.concatenate([wr_hi, jnp.zeros_like(wr_lo), zpad], axis=1)

    ckvs, krs, sfs, sbs = [], [], [], []
    for l in range(DEPTH):
        mod = mod_all[l]
        (w_gq, w_gk, w_gv, w_gg, w_gaf, w_gab, w_cq, w_ckv, w_kr, w_za, w_zb) = jnp.split(
            w_in[l], (256, 512, 1024, 1536, 1552, 1568, 1824, 1952, 1984, 3008), axis=1)
        w_small = jnp.concatenate([w_gaf, w_gab, w_kr, _rot_cols(w_kr), jnp.zeros((D_MODEL, 32), F32)], axis=1)
        w1 = jnp.concatenate([w_gq, w_gk, w_gv, w_gg, w_cq, w_ckv, w_small, w_za, w_zb], axis=1).astype(BF16)
        wa = jnp.zeros((128, 2 * HK), F32)
        wa = wa.at[0:GLA_GATE_RANK, 0:HK].set(w_alpha[l, 0])
        wa = wa.at[GLA_GATE_RANK:2 * GLA_GATE_RANK, HK:2 * HK].set(w_alpha[l, 1]).astype(BF16)
        ba = b_alpha[l].reshape(1, 2 * HK)
        wuq3 = w_uq[l].reshape(MLA_Q_RANK, MLA_HEADS, MLA_NOPE + MLA_ROPE)
        wuqn = wuq3[:, :, :MLA_NOPE].reshape(MLA_Q_RANK, MLA_HEADS * MLA_NOPE).astype(BF16)
        wuq_rope = wuq3[:, :, MLA_NOPE:]
        wuqr = wuq_rope.reshape(MLA_Q_RANK, MLA_HEADS * MLA_ROPE).astype(BF16)
        wuqrr = _rot_cols(wuq_rope).reshape(MLA_Q_RANK, MLA_HEADS * MLA_ROPE).astype(BF16)
        wukt = w_uk[l].reshape(MLA_KV_RANK, MLA_HEADS, MLA_NOPE).transpose(1, 2, 0).astype(BF16)
        wuvt = w_uv[l].reshape(MLA_KV_RANK, MLA_HEADS, MLA_DV).transpose(1, 2, 0).astype(BF16)

        gq, gk, gv, gg, la, qcat, ckvn, kr, za, zb = _inproj(
            x, mod, norm_mix[l][None, :], w1, wa, ba, q_norm[l][None, :], kv_norm[l][None, :],
            wuqn, wuqr, wuqrr, wukt, cq_tab, sq_tab, cos32, sin32)

        to_t = lambda s: s.transpose(0, 3, 1, 2).reshape(s.shape[0], GLA_DV, HK)
        zeros_ctx = jnp.zeros((BATCH, GLA_DV, HK), F32)
        s0f = jnp.concatenate([zeros_ctx, to_t(state_gla_fwd[:, l])], axis=0)
        s0b = jnp.concatenate([zeros_ctx, to_t(state_gla_bwd[:, l])], axis=0)
        o_f, o_r, sf, sb = _gla(gla_tab, gq, gk, gv, la, s0f, s0b, tri_f, tri_b, ones_blk)

        ckv_ctx = ckvn[:N_CTX].reshape(BATCH, SEQ, MLA_KV_RANK)
        kr_ctx = kr[:N_CTX].reshape(BATCH, SEQ, MLA_ROPE)
        ckv_lat = jnp.concatenate([cache_ckv[:, l], ckvn[N_CTX:].reshape(DEC_BATCH, DEC_SEQ, MLA_KV_RANK)], axis=1)
        kr_lat = jnp.concatenate([cache_krope[:, l], kr[N_CTX:].reshape(DEC_BATCH, DEC_SEQ, MLA_ROPE)], axis=1)

        def kv_operands(ckv_all, kr_all):
            kcat = jnp.concatenate([ckv_all, kr_all], axis=-1).astype(BF16)
            return kcat, ckv_all.astype(BF16).transpose(0, 2, 1)

        kc, vt = kv_operands(ckv_ctx, kr_ctx)
        ob_ctx = _attention(qcat, kc, vt, wuvt, tile0=0, tiles_per_seq=1, name="mla_ctx")
        kc, vt = kv_operands(ckv_lat, kr_lat)
        ob_lat = _attention(qcat, kc, vt, wuvt, tile0=CTX_TILES, tiles_per_seq=LAT_TILES_PER_SEQ, name="mla_lat")

        x1, h2x, lg = _merge(x, mod, o_f, o_r, gg, gla_norm[l][None, :], ob_ctx, ob_lat, za, zb, w_pa[l].astype(BF16), w_pb[l].astype(BF16),
                             w_o[l].astype(BF16), norm_ffn[l][None, :], wr1, wr2)
        h2x, grp, rank, _ = _route(lg.T, b_router[:, None], tri_route, h2x)

        n_tt = N_TOK // TE
        grp_t = grp[0].reshape(n_tt, TE)
        rank_t = rank[0].reshape(n_tt, TE)
        run_len = jnp.sum(grp_t[:, :, None] == jnp.arange(N_GROUPS, dtype=jnp.int32)[None, None, :],
                          axis=1, dtype=jnp.int32)
        run_len_al = (run_len + (RUN_ALIGN - 1)) // RUN_ALIGN * RUN_ALIGN
        ranks_before = jnp.cumsum(run_len, axis=0) - run_len
        cnt = jnp.sum(run_len_al, axis=0)
        padded = (cnt + (TE - 1)) // TE * TE
        g_end = jnp.cumsum(padded)
        g_start = g_end - padded
        run_src = jnp.cumsum(run_len_al, axis=1) - run_len_al
        run_dst = g_start[None, :] + jnp.cumsum(run_len_al, axis=0) - run_len_al
        of_group = lambda tbl: sum(jnp.where(grp_t == g, tbl[:, g:g + 1], 0) for g in range(N_GROUPS))
        lrow = (of_group(run_src) + rank_t - of_group(ranks_before)).astype(jnp.int32)
        tile_start = jnp.arange(N_SLOT_TILES, dtype=jnp.int32) * TE
        tile_grp = jnp.minimum(jnp.sum(tile_start[:, None] >= g_end[None, :], axis=1), N_GROUPS - 1).astype(jnp.int32)
        n_used = (g_end[N_GROUPS - 1:] // TE).astype(jnp.int32)
        pad_lo = jnp.concatenate([g_start + cnt, g_end[N_GROUPS - 1:]]).astype(jnp.int32)
        pad_hi = jnp.concatenate([g_end, jnp.full((1,), N_SLOTS, jnp.int32)]).astype(jnp.int32)

        run_len_al, run_src, run_dst = (a.reshape(-1).astype(jnp.int32) for a in (run_len_al, run_src, run_dst))
        xs = _dispatch(pad_lo, pad_hi, run_len_al, run_src, run_dst, lrow, h2x)
        ys = _moe(tile_grp, n_used, xs, w_exp_gate[l].astype(BF16), w_exp_up[l].astype(BF16),
                  w_exp_down[l].astype(BF16).reshape(N_GROUPS, EXPERTS_PER_GROUP * EXPERT_FF, D_MODEL),
                  w_sh_gate[l].astype(BF16), w_sh_up[l].astype(BF16), w_sh_down[l].astype(BF16))
        x = _combine(run_len_al, run_src, run_dst, lrow, ys, x1, mod, final_norm[None, :], final=(l == DEPTH - 1))

        ckvs.append(ckv_ctx)
        krs.append(kr_ctx)
        from_t = lambda s: s[:BATCH].reshape(BATCH, GLA_DV, GLA_HEADS, GLA_DK).transpose(0, 2, 3, 1)
        sfs.append(from_t(sf))
        sbs.append(from_t(sb))

    y_prompt = x[0].reshape(BATCH, SEQ, D_MODEL)
    y_sample = x[1].reshape(DEC_BATCH, DEC_SEQ, D_MODEL)
    return (y_prompt, y_sample, jnp.stack(ckvs, axis=1), jnp.stack(krs, axis=1),
            jnp.stack(sfs, axis=1), jnp.stack(sbs, axis=1))
```
